```python
import math, functools
import jax, jax.numpy as jnp
from jax import lax
import numpy as np

D_MODEL = 1024
BATCH = 16
SEQ = 2048
DEPTH = 1
DEC_BATCH = 128
DEC_SEQ = 8
PAST_LEN = 8192
PAGE_SIZE = 128

N_META = 16
RMS_EPS = 1e-6
D_INNER = 2 * D_MODEL
SSD_HEADDIM = 64
SSD_HEADS = D_INNER // SSD_HEADDIM
SSD_GROUPS = 8
D_STATE = 128
CONV_W = 4
CONV_DIM = D_INNER + 2 * SSD_GROUPS * D_STATE
SSD_CHUNK = 128
ATTN_HEADS = 16
ATTN_KV_HEADS = 4
HEAD_DIM = 64
ATTN_REP = ATTN_HEADS // ATTN_KV_HEADS
ATTN_DIM = ATTN_HEADS * HEAD_DIM
KV_DIM = ATTN_KV_HEADS * HEAD_DIM
ATTN_SCALE = HEAD_DIM ** -0.5
Q_BLOCK = 128
FORGET_BIAS = 8.0
IN_DIM = D_INNER + CONV_DIM + SSD_HEADS + ATTN_DIM + 2 * KV_DIM + ATTN_HEADS + 2 * D_MODEL
N_EXPERT_GROUPS = 4
EXPERTS_PER_GROUP = 8
N_EXPERTS = N_EXPERT_GROUPS * EXPERTS_PER_GROUP
TOP_K = 2
D_EXPERT = 512
MOE_BLOCK = 128

kernel_name = 'hybrid_ssd_fox_hmoe_decoder_step'


def rmsnorm(x, g):
    xf = x.astype(jnp.float32)
    xf = xf * lax.rsqrt(jnp.mean(xf * xf, axis=-1, keepdims=True) + RMS_EPS)
    return xf.astype(x.dtype) * g


def grouped_rmsnorm(y, g):
    yf = y.astype(jnp.float32).reshape(y.shape[:-1] + (SSD_GROUPS, D_INNER // SSD_GROUPS))
    yf = yf * lax.rsqrt(jnp.mean(yf * yf, axis=-1, keepdims=True) + RMS_EPS)
    return yf.reshape(y.shape).astype(y.dtype) * g


def causal_dwconv(xpad, w, b):
    l = xpad.shape[1] - (CONV_W - 1)
    out = b
    for i in range(CONV_W):
        out = out + xpad[:, i:i + l] * w[i]
    return out


def ssd_chunked(x, dt, A, Bm, Cm, h0, chunk):
    b, l, nh, p = x.shape
    g, n = Bm.shape[2], Bm.shape[3]
    r = nh // g
    c = l // chunk
    f32 = jnp.float32
    xc = x.astype(f32).reshape(b, c, chunk, g, r, p)
    dtc = dt.astype(f32).reshape(b, c, chunk, g, r)
    Bc = Bm.astype(f32).reshape(b, c, chunk, g, n)
    Cc = Cm.astype(f32).reshape(b, c, chunk, g, n)
    acum = jnp.cumsum(dtc * A.reshape(g, r), axis=2)
    mask = jnp.tril(jnp.ones((chunk, chunk), bool))[:, :, None, None]
    seg = acum[:, :, :, None] - acum[:, :, None, :]
    lmat = jnp.exp(jnp.where(mask, seg, -jnp.inf))
    cb = jnp.einsum('bcign,bcjgn->bcijg', Cc, Bc)
    y_diag = jnp.einsum('bcijgr,bcjgrp->bcigrp', cb[..., None] * lmat * dtc[:, :, None], xc)
    decay_to_end = jnp.exp(acum[:, :, -1:] - acum)
    states = jnp.einsum('bcjgn,bcjgr,bcjgrp->bcgrpn', Bc, decay_to_end * dtc, xc)
    chunk_decay = jnp.exp(acum[:, :, -1])

    def step(h, inp):
        st, dec = inp
        return dec[..., None, None] * h + st, h

    h_last, h_starts = lax.scan(step, h0.astype(f32).reshape(b, g, r, p, n),
                                (jnp.moveaxis(states, 1, 0), jnp.moveaxis(chunk_decay, 1, 0)))
    y_off = jnp.einsum('bcign,bcigr,bcgrpn->bcigrp', Cc, jnp.exp(acum), jnp.moveaxis(h_starts, 0, 1))
    return (y_diag + y_off).reshape(b, l, nh, p), h_last.reshape(b, nh, p, n)


def ssd_prompt(x, dt, A, Bm, Cm):
    h0 = jnp.zeros((x.shape[0], SSD_HEADS, SSD_HEADDIM, D_STATE), jnp.float32)
    m = N_META
    y_m, h_m = ssd_chunked(x[:, :m], dt[:, :m], A, Bm[:, :m], Cm[:, :m], h0, m)
    y_r, h_r = ssd_chunked(x[:, m:], dt[:, m:], A, Bm[:, m:], Cm[:, m:], h_m, SSD_CHUNK)
    return jnp.concatenate([y_m, y_r], axis=1), h_r


def ssd_sample(h0, x, dt, A, Bm, Cm):
    return ssd_chunked(x, dt, A, Bm, Cm, h0, x.shape[1])


def fox_prompt(q, k, v, logf):
    b, l = q.shape[:2]
    nb = -(-l // Q_BLOCK)
    lp = nb * Q_BLOCK
    padw = ((0, 0), (0, lp - l), (0, 0), (0, 0))
    qp = jnp.pad(q, padw).reshape(b, nb, Q_BLOCK, ATTN_KV_HEADS, ATTN_REP, HEAD_DIM)
    kp = jnp.pad(k, padw)
    vp = jnp.pad(v, padw)
    c = jnp.cumsum(jnp.pad(logf, ((0, 0), (0, lp - l), (0, 0))), axis=1).reshape(b, lp, ATTN_KV_HEADS, ATTN_REP)
    c_key = jnp.transpose(c, (0, 2, 3, 1))[:, :, :, None, :]
    c_qry = jnp.moveaxis(c.reshape(b, nb, Q_BLOCK, ATTN_KV_HEADS, ATTN_REP), 1, 0)
    kpos = jnp.arange(lp)

    def block(args):
        qi, ci, i = args
        s = jnp.einsum('bqgrd,bkgd->bgrqk', qi, kp, preferred_element_type=jnp.float32) * ATTN_SCALE
        s = s + jnp.transpose(ci, (0, 2, 3, 1))[..., None] - c_key
        qpos = i * Q_BLOCK + jnp.arange(Q_BLOCK)
        s = jnp.where(kpos[None, :] <= qpos[:, None], s, -jnp.inf)
        p = jax.nn.softmax(s, axis=-1)
        return jnp.einsum('bgrqk,bkgd->bqgrd', p.astype(vp.dtype), vp)

    o = lax.map(block, (jnp.moveaxis(qp, 1, 0), c_qry, jnp.arange(nb)))
    return jnp.moveaxis(o, 0, 1).reshape(b, lp, ATTN_DIM)[:, :l]


def fox_sample(cache_k, cache_v, cache_logf, page_table, layer, q, k, v, logf):
    b, s_new = q.shape[:2]
    past = page_table.shape[1] * PAGE_SIZE
    kp = cache_k[layer][page_table].reshape(b, past, ATTN_KV_HEADS, HEAD_DIM)
    vp = cache_v[layer][page_table].reshape(b, past, ATTN_KV_HEADS, HEAD_DIM)
    lfp = cache_logf[layer][page_table].reshape(b, past, ATTN_HEADS).astype(jnp.float32)
    r = lax.cumsum(lfp, axis=1, reverse=True) - lfp
    cn = jnp.cumsum(logf.astype(jnp.float32), axis=1)
    r_key = jnp.transpose(r.reshape(b, past, ATTN_KV_HEADS, ATTN_REP), (0, 2, 3, 1))[:, :, :, None, :]
    cn_t = jnp.transpose(cn.reshape(b, s_new, ATTN_KV_HEADS, ATTN_REP), (0, 2, 3, 1))
    qg = q.reshape(b, s_new, ATTN_KV_HEADS, ATTN_REP, HEAD_DIM)
    s_past = jnp.einsum('bqgrd,bkgd->bgrqk', qg, kp, preferred_element_type=jnp.float32) * ATTN_SCALE
    s_past = s_past + r_key + cn_t[..., None]
    s_self = jnp.einsum('bqgrd,bkgd->bgrqk', qg, k, preferred_element_type=jnp.float32) * ATTN_SCALE
    s_self = s_self + cn_t[..., :, None] - cn_t[..., None, :]
    s_self = jnp.where(jnp.tril(jnp.ones((s_new, s_new), bool)), s_self, -jnp.inf)
    p = jax.nn.softmax(jnp.concatenate([s_past, s_self], axis=-1), axis=-1)
    o = (jnp.einsum('bgrqk,bkgd->bqgrd', p[..., :past].astype(vp.dtype), vp)
         + jnp.einsum('bgrqk,bkgd->bqgrd', p[..., past:].astype(v.dtype), v))
    return o.reshape(b, s_new, ATTN_DIM)


def grouped_experts(x2, eid, w, w_gate, w_up, w_down):
    t = x2.shape[0]
    a = t * TOP_K
    fe = eid.reshape(a)
    ft = jnp.broadcast_to(jnp.arange(t, dtype=jnp.int32)[:, None], (t, TOP_K)).reshape(a)
    fw = w.reshape(a)
    order = jnp.argsort(fe)
    se = fe[order]
    counts = jnp.bincount(fe, length=N_EXPERTS)
    padded = (counts + MOE_BLOCK - 1) // MOE_BLOCK * MOE_BLOCK
    start = jnp.cumsum(counts) - counts
    pend = jnp.cumsum(padded)
    dest = (pend - padded)[se] + jnp.arange(a) - start[se]
    n_blocks = (a + N_EXPERTS * (MOE_BLOCK - 1) + MOE_BLOCK - 1) // MOE_BLOCK
    rows = n_blocks * MOE_BLOCK
    row_tok = jnp.zeros((rows,), jnp.int32).at[dest].set(ft[order])
    row_w = jnp.zeros((rows,), x2.dtype).at[dest].set(fw[order].astype(x2.dtype))
    block_e = jnp.minimum(jnp.searchsorted(pend, jnp.arange(n_blocks) * MOE_BLOCK, side='right'), N_EXPERTS - 1)

    def block(args):
        tok, wr, e = args
        xb = x2[tok]
        hb = jax.nn.silu(xb @ w_gate[e]) * (xb @ w_up[e])
        return (hb @ w_down[e]) * wr[:, None]

    y_rows = lax.map(block, (row_tok.reshape(n_blocks, MOE_BLOCK), row_w.reshape(n_blocks, MOE_BLOCK), block_e))
    return jax.ops.segment_sum(y_rows.reshape(rows, D_MODEL), row_tok, num_segments=t)


def hier_moe(x2, w_rg, b_rg, w_re, b_re, w_eg, w_eu, w_ed):
    t = x2.shape[0]
    tid = jnp.arange(t)
    gl = (x2 @ w_rg).astype(jnp.float32) + b_rg
    pg = jax.nn.softmax(gl, axis=-1)
    gsel = jnp.argmax(gl, axis=-1)
    wgrp = pg[tid, gsel]
    el = ((x2 @ w_re).astype(jnp.float32) + b_re).reshape(t, N_EXPERT_GROUPS, EXPERTS_PER_GROUP)
    el_sel = el[tid, gsel]
    topl, topi = lax.top_k(el_sel, TOP_K)
    topw = jax.nn.softmax(topl, axis=-1) * wgrp[:, None]
    eid = gsel[:, None] * EXPERTS_PER_GROUP + topi
    return grouped_experts(x2, eid, topw, w_eg, w_eu, w_ed)


def decoder_layer(h, conv_prev, ssm_run, attn_run, norm1_g, w_in, conv_w, conv_b, dt_bias, a_log, d_skip,
                  ssd_norm_g, b_forget, w_ssd_br, w_attn_br, w_out, norm2_g, w_router_group, b_router_group,
                  w_router_expert, b_router_expert, w_exp_gate, w_exp_up, w_exp_down):
    b, l, _ = h.shape
    u = rmsnorm(h, norm1_g) @ w_in
    splits = np.cumsum([D_INNER, CONV_DIM, SSD_HEADS, ATTN_DIM, KV_DIM, KV_DIM, ATTN_HEADS, D_MODEL]).tolist()
    z, xbc, dt_raw, q, k, v, f_raw, gate_s, gate_a = jnp.split(u, splits, axis=-1)
    xpad = jnp.concatenate([conv_prev.astype(xbc.dtype), xbc], axis=1)
    conv_state = xpad[:, -(CONV_W - 1):]
    xbc = jax.nn.silu(causal_dwconv(xpad, conv_w, conv_b))
    xs, bm, cm = jnp.split(xbc, [D_INNER, D_INNER + SSD_GROUPS * D_STATE], axis=-1)
    xs = xs.reshape(b, l, SSD_HEADS, SSD_HEADDIM)
    dt = jax.nn.softplus(dt_raw.astype(jnp.float32) + dt_bias)
    A = -jnp.exp(a_log.astype(jnp.float32))
    y, ssm_state = ssm_run(xs, dt, A, bm.reshape(b, l, SSD_GROUPS, D_STATE), cm.reshape(b, l, SSD_GROUPS, D_STATE))
    y = y.astype(xs.dtype) + d_skip[:, None] * xs
    y = grouped_rmsnorm(y.reshape(b, l, D_INNER) * jax.nn.silu(z), ssd_norm_g)
    logf = jax.nn.log_sigmoid(f_raw.astype(jnp.float32) + b_forget)
    k = k.reshape(b, l, ATTN_KV_HEADS, HEAD_DIM)
    v = v.reshape(b, l, ATTN_KV_HEADS, HEAD_DIM)
    o = attn_run(q.reshape(b, l, ATTN_HEADS, HEAD_DIM), k, v, logf)
    mix = jax.nn.sigmoid(gate_s) * (y @ w_ssd_br) + jax.nn.sigmoid(gate_a) * (o @ w_attn_br)
    h = h + mix @ w_out
    hn = rmsnorm(h, norm2_g).reshape(b * l, D_MODEL)
    h = h + hier_moe(hn, w_router_group, b_router_group, w_router_expert, b_router_expert,
                     w_exp_gate, w_exp_up, w_exp_down).reshape(b, l, D_MODEL)
    return h, (k, v, logf.astype(k.dtype), ssm_state.astype(xs.dtype), conv_state)


def setup_inputs(seed: int = 0) -> dict:
    key = jax.random.key(seed)
    ks = iter(jax.random.split(key, 40))

    def nrm(shape, scale=1.0):
        return jax.random.normal(next(ks), shape, jnp.float32) * scale

    n_pages = PAST_LEN // PAGE_SIZE
    n_used = DEC_BATCH * n_pages
    n_pool = n_used + max(1, n_used // 4)
    x_prompt = nrm((BATCH, SEQ, D_MODEL))
    x_sample = nrm((DEC_BATCH, DEC_SEQ, D_MODEL))
    cache_k = nrm((DEPTH, n_pool, PAGE_SIZE, ATTN_KV_HEADS, HEAD_DIM))
    cache_v = nrm((DEPTH, n_pool, PAGE_SIZE, ATTN_KV_HEADS, HEAD_DIM))
    cache_logf = jax.nn.log_sigmoid(FORGET_BIAS + nrm((DEPTH, n_pool, PAGE_SIZE, ATTN_HEADS)))
    state_ssm = nrm((DEPTH, DEC_BATCH, SSD_HEADS, SSD_HEADDIM, D_STATE), 0.5)
    state_conv = nrm((DEPTH, DEC_BATCH, CONV_W - 1, CONV_DIM))
    page_table = jax.random.permutation(next(ks), n_pool)[:n_used].reshape(DEC_BATCH, n_pages).astype(jnp.int32)
    dt0 = jnp.exp(jax.random.uniform(next(ks), (DEPTH, SSD_HEADS), jnp.float32, math.log(1e-3), math.log(1e-1)))
    dt_bias = dt0 + jnp.log(-jnp.expm1(-dt0))
    a_log = jnp.log(jax.random.uniform(next(ks), (DEPTH, SSD_HEADS), jnp.float32, 1.0, 16.0))
    return {
        'x_prompt': x_prompt,
        'x_sample': x_sample,
        'cache_k': cache_k,
        'cache_v': cache_v,
        'cache_logf': cache_logf,
        'state_ssm': state_ssm,
        'state_conv': state_conv,
        'page_table': page_table,
        'meta_tokens': nrm((N_META, D_MODEL)),
        'norm1_g': 1.0 + nrm((DEPTH, D_MODEL), 0.02),
        'w_in': nrm((DEPTH, D_MODEL, IN_DIM), D_MODEL ** -0.5),
        'conv_w': nrm((DEPTH, CONV_W, CONV_DIM), CONV_W ** -0.5),
        'conv_b': nrm((DEPTH, CONV_DIM), 0.01),
        'dt_bias': dt_bias,
        'a_log': a_log,
        'd_skip': 1.0 + nrm((DEPTH, SSD_HEADS), 0.01),
        'ssd_norm_g': 1.0 + nrm((DEPTH, D_INNER), 0.02),
        'b_forget': FORGET_BIAS + nrm((DEPTH, ATTN_HEADS), 0.1),
        'w_ssd_br': nrm((DEPTH, D_INNER, D_MODEL), D_INNER ** -0.5),
        'w_attn_br': nrm((DEPTH, ATTN_DIM, D_MODEL), ATTN_DIM ** -0.5),
        'w_out': nrm((DEPTH, D_MODEL, D_MODEL), D_MODEL ** -0.5),
        'norm2_g': 1.0 + nrm((DEPTH, D_MODEL), 0.02),
        'w_router_group': nrm((DEPTH, D_MODEL, N_EXPERT_GROUPS), D_MODEL ** -0.5),
        'b_router_group': nrm((DEPTH, N_EXPERT_GROUPS), 0.01),
        'w_router_expert': nrm((DEPTH, D_MODEL, N_EXPERTS), D_MODEL ** -0.5),
        'b_router_expert': nrm((DEPTH, N_EXPERTS), 0.01),
        'w_exp_gate': nrm((DEPTH, N_EXPERTS, D_MODEL, D_EXPERT), D_MODEL ** -0.5),
        'w_exp_up': nrm((DEPTH, N_EXPERTS, D_MODEL, D_EXPERT), D_MODEL ** -0.5),
        'w_exp_down': nrm((DEPTH, N_EXPERTS, D_EXPERT, D_MODEL), D_EXPERT ** -0.5),
        'final_norm_g': 1.0 + nrm((D_MODEL,), 0.02),
    }


def reference(x_prompt, x_sample, cache_k, cache_v, cache_logf, state_ssm, state_conv, page_table, meta_tokens,
              norm1_g, w_in, conv_w, conv_b, dt_bias, a_log, d_skip, ssd_norm_g, b_forget, w_ssd_br, w_attn_br,
              w_out, norm2_g, w_router_group, b_router_group, w_router_expert, b_router_expert, w_exp_gate,
              w_exp_up, w_exp_down, final_norm_g):
    bp = x_prompt.shape[0]
    meta = jnp.broadcast_to(meta_tokens[None].astype(x_prompt.dtype), (bp, N_META, D_MODEL))
    hp = jnp.concatenate([meta, x_prompt], axis=1)
    hs = x_sample
    st_p, st_s = [], []
    for layer in range(DEPTH):
        lw = (norm1_g[layer], w_in[layer], conv_w[layer], conv_b[layer], dt_bias[layer], a_log[layer],
              d_skip[layer], ssd_norm_g[layer], b_forget[layer], w_ssd_br[layer], w_attn_br[layer],
              w_out[layer], norm2_g[layer], w_router_group[layer], b_router_group[layer],
              w_router_expert[layer], b_router_expert[layer], w_exp_gate[layer], w_exp_up[layer],
              w_exp_down[layer])
        conv0 = jnp.zeros((bp, CONV_W - 1, CONV_DIM), hp.dtype)
        hp, sp = decoder_layer(hp, conv0, ssd_prompt, fox_prompt, *lw)
        hs, ss = decoder_layer(hs, state_conv[layer], functools.partial(ssd_sample, state_ssm[layer]),
                               functools.partial(fox_sample, cache_k, cache_v, cache_logf, page_table, layer), *lw)
        st_p.append(sp)
        st_s.append(ss)
    k_prompt, v_prompt, logf_prompt, ssm_prompt, conv_prompt = [jnp.stack(a) for a in zip(*st_p)]
    k_sample, v_sample, logf_sample, ssm_sample, conv_sample = [jnp.stack(a) for a in zip(*st_s)]
    y_prompt = rmsnorm(hp, final_norm_g)[:, N_META:]
    y_sample = rmsnorm(hs, final_norm_g)
    return (y_prompt, y_sample, k_prompt, v_prompt, logf_prompt, ssm_prompt, conv_prompt,
            k_sample, v_sample, logf_sample, ssm_sample, conv_sample)
```

```python
import functools

import numpy as np
import jax
import jax.numpy as jnp
from jax import lax
from jax.experimental import pallas as pl
from jax.experimental.pallas import tpu as pltpu

F32 = jnp.float32
BF16 = jnp.bfloat16
HI = lax.Precision.HIGHEST

D_MODEL = 1024
N_META = 16
RMS_EPS = 1e-6
D_INNER = 2048
SSD_HEADDIM = 64
SSD_HEADS = 32
SSD_GROUPS = 8
HEADS_PER_GROUP = SSD_HEADS // SSD_GROUPS
GROUP_W = HEADS_PER_GROUP * SSD_HEADDIM
D_STATE = 128
CONV_W = 4
CONV_DIM = 4096
SSD_CHUNK = 128
ATTN_HEADS = 16
ATTN_KV_HEADS = 4
HEAD_DIM = 64
ATTN_REP = 4
ATTN_DIM = 1024
KV_DIM = 256
ATTN_SCALE = HEAD_DIM ** -0.5
PAGE_SIZE = 128
N_EXPERT_GROUPS = 4
EXPERTS_PER_GROUP = 8
N_EXPERTS = 32
D_EXPERT = 512
LANES = 128

U_XBC = 0
U_Z = 4096
U_Q = 6144
U_GS = 7168
U_GA = 8192
U_K = 9216
U_V = 9472
U_DTF = 9728
U_COLS = 9856
U_COLS_PADDED = 10240
INPROJ_TN = 1024


def _cparams(sem, vmem_mb=None):
    kw = dict(dimension_semantics=sem)
    if vmem_mb is not None:
        kw["vmem_limit_bytes"] = vmem_mb * 1024 * 1024
    return pltpu.CompilerParams(**kw)


def _inproj_body(x_ref, g_ref, w_ref, o_ref, xn_ref):
    @pl.when(pl.program_id(1) == 0)
    def _():
        x = x_ref[...]
        ms = jnp.mean(x * x, axis=-1, keepdims=True)
        xn_ref[...] = ((x * lax.rsqrt(ms + RMS_EPS)) * g_ref[...]).astype(BF16)

    o_ref[...] = jnp.dot(xn_ref[...], w_ref[...], preferred_element_type=F32)


def _inproj(x, g, wp):
    t = x.shape[0]
    tm = min(t, 1024)
    assert t % tm == 0
    return pl.pallas_call(
        _inproj_body,
        out_shape=jax.ShapeDtypeStruct((t, U_COLS), F32),
        grid=(t // tm, U_COLS_PADDED // INPROJ_TN),
        in_specs=[
            pl.BlockSpec((tm, D_MODEL), lambda i, j: (i, 0)),
            pl.BlockSpec((1, D_MODEL), lambda i, j: (0, 0)),
            pl.BlockSpec((D_MODEL, INPROJ_TN), lambda i, j: (0, j)),
        ],
        out_specs=pl.BlockSpec((tm, INPROJ_TN), lambda i, j: (i, j)),
        scratch_shapes=[pltpu.VMEM((tm, D_MODEL), BF16)],
        compiler_params=_cparams(("parallel", "arbitrary"), 48),
        name="inproj",
    )(x, g.reshape(1, D_MODEL), wp)


def _pack_w_in(w_in):
    o = np.cumsum([0, D_INNER, CONV_DIM, SSD_HEADS, ATTN_DIM, KV_DIM, KV_DIM, ATTN_HEADS, D_MODEL, D_MODEL])
    z, xbc, dt, q, k, v, f, gs, ga = [w_in[:, o[i]:o[i + 1]] for i in range(9)]
    pad_dtf = jnp.zeros((D_MODEL, LANES - SSD_HEADS - ATTN_HEADS), w_in.dtype)
    pad = jnp.zeros((D_MODEL, U_COLS_PADDED - U_COLS), w_in.dtype)
    return jnp.concatenate([xbc, z, q, gs, ga, k, v, dt, f, pad_dtf, pad], axis=1).astype(BF16)


def _log_sigmoid(x):
    return jnp.minimum(x, 0.0) - jnp.log1p(jnp.exp(-jnp.abs(x)))


def _logf_body(dtf_ref, bf_ref, slab_ref, ckt_ref, carry_ref, *, tiles_per_seq, tl, seg):
    i = pl.program_id(0)

    @pl.when(i % tiles_per_seq == 0)
    def _():
        carry_ref[...] = jnp.zeros_like(carry_ref)

    lf = _log_sigmoid(dtf_ref[...] + bf_ref[...])
    row = lax.broadcasted_iota(jnp.int32, (tl, tl), 0)
    col = lax.broadcasted_iota(jnp.int32, (tl, tl), 1)
    keep = col <= row
    if seg < tl:
        keep = keep & ((row // seg) == (col // seg))
    tri = keep.astype(F32)
    c = jnp.dot(tri, lf, precision=HI, preferred_element_type=F32) + carry_ref[...]
    carry_ref[...] = c[tl - 1:tl, :]
    lane = lax.broadcasted_iota(jnp.int32, (tl, LANES), 1)
    c_lo = pltpu.roll(c, LANES - SSD_HEADS, 1)
    slab_ref[...] = jnp.where(lane < ATTN_HEADS, c_lo, jnp.where((lane >= 32) & (lane < 48), lf, 0.0))
    ckt_ref[...] = c_lo.T[:ATTN_HEADS, :]


def _logf(dtf_src, dtf_col_block, b_forget, n_tokens, seq_len):
    tl = min(n_tokens, 256)
    assert n_tokens % tl == 0 and n_tokens % seq_len == 0
    assert seq_len % tl == 0 or tl % seq_len == 0
    tiles_per_seq = max(seq_len // tl, 1)
    bf = jnp.zeros((1, LANES), F32).at[0, SSD_HEADS:SSD_HEADS + ATTN_HEADS].set(b_forget)
    return pl.pallas_call(
        functools.partial(_logf_body, tiles_per_seq=tiles_per_seq, tl=tl, seg=seq_len),
        out_shape=(jax.ShapeDtypeStruct((n_tokens, LANES), F32),
                   jax.ShapeDtypeStruct((ATTN_HEADS, n_tokens), F32)),
        grid=(n_tokens // tl,),
        in_specs=[
            pl.BlockSpec((tl, LANES), lambda i: (i, dtf_col_block)),
            pl.BlockSpec((1, LANES), lambda i: (0, 0)),
        ],
        out_specs=(pl.BlockSpec((tl, LANES), lambda i: (i, 0)),
                   pl.BlockSpec((ATTN_HEADS, tl), lambda i: (0, i))),
        scratch_shapes=[pltpu.VMEM((1, LANES), F32)],
        compiler_params=_cparams(("arbitrary",)),
        name="logf_cumsum",
    )(dtf_src, bf)


CONV_PAD = 8


def _silu(x):
    return x * (1.0 / (1.0 + jnp.exp(-x)))


def _softplus(x):
    return jnp.maximum(x, 0.0) + jnp.log1p(jnp.exp(-jnp.abs(x)))


def _transpose_rows(x, q):
    if q < LANES:
        x = jnp.concatenate([x, jnp.zeros((LANES - q, x.shape[1]), x.dtype)], axis=0)
    return x.T[:, :q]


def _ssd_body(xbc_ref, z_ref, dtf_ref, conv0_ref, h0_ref, cw_ref, cb_ref, dtb_ref, alog_ref, dskip_ref,
              gn_ref, e_ref, y_ref, hout_ref, xconv_ref, ht_ref, *, q, n_chunks):
    c = pl.program_id(1)

    @pl.when(c == 0)
    def _():
        xconv_ref[0:CONV_PAD, :] = conv0_ref[0]
        for g in range(SSD_GROUPS):
            ht_ref[g] = h0_ref[0, g * GROUP_W:(g + 1) * GROUP_W, :].T

    xconv_ref[CONV_PAD:CONV_PAD + q, :] = xbc_ref[...]
    acc = cb_ref[...]
    for k in range(CONV_W):
        off = CONV_PAD - (CONV_W - 1) + k
        acc = acc + xconv_ref[off:off + q, :] * cw_ref[k:k + 1, :]
    xc = _silu(acc)
    xconv_ref[CONV_PAD - (CONV_W - 1):CONV_PAD, :] = xconv_ref[CONV_PAD + q - (CONV_W - 1):CONV_PAD + q, :]

    dt = _softplus(dtf_ref[...] + dtb_ref[...])
    a = -jnp.exp(alog_ref[...])
    row = lax.broadcasted_iota(jnp.int32, (q, q), 0)
    col = lax.broadcasted_iota(jnp.int32, (q, q), 1)
    causal = col <= row
    acum = jnp.dot(causal.astype(F32), dt * a, precision=HI, preferred_element_type=F32)
    acum_t = _transpose_rows(acum, q)
    a_last = acum[q - 1:q, :]
    fac = jnp.concatenate([jnp.exp(acum), jnp.exp(a_last - acum) * dt, dt], axis=0)
    fac = jnp.dot(fac, e_ref[...], precision=HI, preferred_element_type=F32)
    ea_full, wst_full, dt_full = fac[0:q], fac[q:2 * q], fac[2 * q:3 * q]

    for g in range(SSD_GROUPS):
        gs = slice(g * GROUP_W, (g + 1) * GROUP_W)
        xg = xc[:, gs]
        bg = xc[:, D_INNER + g * D_STATE:D_INNER + (g + 1) * D_STATE]
        cg = xc[:, D_INNER + SSD_GROUPS * D_STATE + g * D_STATE:D_INNER + SSD_GROUPS * D_STATE + (g + 1) * D_STATE]
        bg16 = bg.astype(BF16)
        cg16 = cg.astype(BF16)
        cbm = lax.dot_general(cg16, bg16, (((1,), (1,)), ((), ())), preferred_element_type=F32)
        xdt = (xg * dt_full[:, gs]).astype(BF16)
        yd = []
        for r in range(HEADS_PER_GROUP):
            h = g * HEADS_PER_GROUP + r
            seg = acum[:, h:h + 1] - acum_t[h:h + 1, :]
            m = cbm * jnp.exp(jnp.where(causal, seg, -jnp.inf))
            yd.append(jnp.dot(m.astype(BF16), xdt[:, r * SSD_HEADDIM:(r + 1) * SSD_HEADDIM],
                              preferred_element_type=F32))
        yd = jnp.concatenate(yd, axis=1)
        htg = ht_ref[g]
        yoff = jnp.dot(cg16, htg.astype(BF16), preferred_element_type=F32) * ea_full[:, gs]
        xw = (xg * wst_full[:, gs]).astype(BF16)
        bgt = _transpose_rows(bg, q).astype(BF16)
        st = jnp.dot(bgt, xw, preferred_element_type=F32)
        ht_ref[g] = ea_full[q - 1:q, gs] * htg + st
        yg = yd + yoff + dskip_ref[:, gs] * xg
        yz = yg * _silu(z_ref[:, gs])
        ms = jnp.mean(yz * yz, axis=-1, keepdims=True)
        y_ref[:, gs] = (yz * lax.rsqrt(ms + RMS_EPS) * gn_ref[:, gs]).astype(y_ref.dtype)

    @pl.when(c == n_chunks - 1)
    def _():
        for g in range(SSD_GROUPS):
            hout_ref[0, g * GROUP_W:(g + 1) * GROUP_W, :] = ht_ref[g].T


def _head_expand_matrix():
    e = np.zeros((LANES, D_INNER), np.float32)
    for h in range(SSD_HEADS):
        e[h, h * SSD_HEADDIM:(h + 1) * SSD_HEADDIM] = 1.0
    return jnp.asarray(e)


def _ssd(u, n_seq, seq_len, q, conv0, h0, p, y_dtype):
    n_chunks = seq_len // q
    assert seq_len % q == 0 and q % 8 == 0
    conv_ix = (lambda b, c: (b, 0, 0)) if conv0.shape[0] == n_seq and n_seq > 1 else (lambda b, c: (0, 0, 0))
    h_ix = (lambda b, c: (b, 0, 0)) if h0.shape[0] == n_seq and n_seq > 1 else (lambda b, c: (0, 0, 0))
    const2 = lambda b, c: (0, 0)
    return pl.pallas_call(
        functools.partial(_ssd_body, q=q, n_chunks=n_chunks),
        out_shape=(jax.ShapeDtypeStruct((n_seq * seq_len, D_INNER), y_dtype),
                   jax.ShapeDtypeStruct((n_seq, D_INNER, D_STATE), F32)),
        grid=(n_seq, n_chunks),
        in_specs=[
            pl.BlockSpec((q, CONV_DIM), lambda b, c: (b * n_chunks + c, U_XBC // CONV_DIM)),
            pl.BlockSpec((q, D_INNER), lambda b, c: (b * n_chunks + c, U_Z // D_INNER)),
            pl.BlockSpec((q, LANES), lambda b, c: (b * n_chunks + c, U_DTF // LANES)),
            pl.BlockSpec((1, CONV_PAD, CONV_DIM), conv_ix),
            pl.BlockSpec((1, D_INNER, D_STATE), h_ix),
            pl.BlockSpec((CONV_W, CONV_DIM), const2),
            pl.BlockSpec((1, CONV_DIM), const2),
            pl.BlockSpec((1, LANES), const2),
            pl.BlockSpec((1, LANES), const2),
            pl.BlockSpec((1, D_INNER), const2),
            pl.BlockSpec((1, D_INNER), const2),
            pl.BlockSpec((LANES, D_INNER), const2),
        ],
        out_specs=(pl.BlockSpec((q, D_INNER), lambda b, c: (b * n_chunks + c, 0)),
                   pl.BlockSpec((1, D_INNER, D_STATE), lambda b, c: (b, 0, 0))),
        scratch_shapes=[pltpu.VMEM((CONV_PAD + q, CONV_DIM), F32),
                        pltpu.VMEM((SSD_GROUPS, D_STATE, GROUP_W), F32)],
        compiler_params=_cparams(("parallel", "arbitrary"), 48),
        name="ssd_q%d" % q,
    )(u, u, u, conv0, h0, p["conv_w"], p["conv_b"], p["dt_bias"], p["a_log"], p["d_skip"], p["ssd_norm_g"],
      p["head_expand"])


def _ssd_params(conv_w, conv_b, dt_bias, a_log, d_skip, ssd_norm_g):
    pad32 = lambda v: jnp.zeros((1, LANES), F32).at[0, :SSD_HEADS].set(v)
    return dict(conv_w=conv_w, conv_b=conv_b.reshape(1, CONV_DIM), dt_bias=pad32(dt_bias), a_log=pad32(a_log),
                d_skip=jnp.repeat(d_skip, SSD_HEADDIM).reshape(1, D_INNER),
                ssd_norm_g=ssd_norm_g.reshape(1, D_INNER), head_expand=_head_expand_matrix())


def _fox_prompt_body(qi_ref, kj_ref, q_ref, k_ref, v_ref, cq_ref, ck_ref, km_ref, vm_ref, bm_ref, o_ref,
                     qs_ref, m_ref, l_ref, acc_ref, *, tq, tk):
    step = pl.program_id(1)
    i = qi_ref[step]
    j = kj_ref[step]

    def update(h, s, v16):
        hs = slice(h * HEAD_DIM, (h + 1) * HEAD_DIM)
        m_prev = m_ref[h]
        m_new = jnp.maximum(m_prev, jnp.max(s, axis=-1, keepdims=True))
        alpha = jnp.exp(m_prev - m_new)
        p = jnp.exp(s - m_new[:, 0:1])
        l_ref[h] = alpha * l_ref[h] + jnp.sum(p, axis=-1, keepdims=True)
        m_ref[h] = m_new
        pv = jnp.dot(p.astype(BF16), v16, preferred_element_type=F32)
        acc_ref[:, hs] = alpha[:, 0:HEAD_DIM] * acc_ref[:, hs] + pv

    @pl.when(j == 0)
    def _():
        qs_ref[...] = (q_ref[...] * ATTN_SCALE).astype(BF16)
        m_ref[...] = jnp.full(m_ref.shape, -jnp.inf, F32)
        l_ref[...] = jnp.zeros(l_ref.shape, F32)
        acc_ref[...] = jnp.zeros(acc_ref.shape, F32)
        km16 = km_ref[...].astype(BF16)
        vm16 = vm_ref[...].astype(BF16)
        for h in range(ATTN_HEADS):
            g = h // ATTN_REP
            gsl = slice(g * HEAD_DIM, (g + 1) * HEAD_DIM)
            s = lax.dot_general(qs_ref[:, h * HEAD_DIM:(h + 1) * HEAD_DIM], km16[:, gsl],
                                (((1,), (1,)), ((), ())), preferred_element_type=F32)
            s = s + cq_ref[:, h:h + 1] - bm_ref[h:h + 1, :]
            update(h, s, vm16[:, gsl])

    k16 = k_ref[...].astype(BF16)
    v16 = v_ref[...].astype(BF16)
    dcol = (lax.broadcasted_iota(jnp.int32, (tq, tk), 1) - lax.broadcasted_iota(jnp.int32, (tq, tk), 0))
    lim = jnp.where(j < i, tk, 0)
    visible = dcol <= lim
    for h in range(ATTN_HEADS):
        g = h // ATTN_REP
        gsl = slice(g * HEAD_DIM, (g + 1) * HEAD_DIM)
        s = lax.dot_general(qs_ref[:, h * HEAD_DIM:(h + 1) * HEAD_DIM], k16[:, gsl],
                            (((1,), (1,)), ((), ())), preferred_element_type=F32)
        s = s + cq_ref[:, h:h + 1] - ck_ref[h:h + 1, :]
        s = jnp.where(visible, s, -jnp.inf)
        update(h, s, v16[:, gsl])

    @pl.when(j == i)
    def _():
        for h in range(ATTN_HEADS):
            hs = slice(h * HEAD_DIM, (h + 1) * HEAD_DIM)
            o_ref[:, hs] = (acc_ref[:, hs] * (1.0 / l_ref[h])[:, 0:HEAD_DIM]).astype(o_ref.dtype)


def _fox_prompt(u, slab, ckt, km, vm, bm_t, n_seq, seq_len, tq):
    nq = seq_len // tq
    assert seq_len % tq == 0
    qi = np.array([i for i in range(nq) for _ in range(i + 1)], np.int32)
    kj = np.array([j for i in range(nq) for j in range(i + 1)], np.int32)
    n_meta = km.shape[0]
    grid_spec = pltpu.PrefetchScalarGridSpec(
        num_scalar_prefetch=2,
        grid=(n_seq, len(qi)),
        in_specs=[
            pl.BlockSpec((tq, ATTN_DIM), lambda b, s, qi, kj: (b * nq + qi[s], U_Q // ATTN_DIM)),
            pl.BlockSpec((tq, KV_DIM), lambda b, s, qi, kj: (b * nq + kj[s], U_K // KV_DIM)),
            pl.BlockSpec((tq, KV_DIM), lambda b, s, qi, kj: (b * nq + kj[s], U_V // KV_DIM)),
            pl.BlockSpec((tq, LANES), lambda b, s, qi, kj: (b * nq + qi[s], 0)),
            pl.BlockSpec((ATTN_HEADS, tq), lambda b, s, qi, kj: (0, b * nq + kj[s])),
            pl.BlockSpec((n_meta, KV_DIM), lambda b, s, qi, kj: (0, 0)),
            pl.BlockSpec((n_meta, KV_DIM), lambda b, s, qi, kj: (0, 0)),
            pl.BlockSpec((ATTN_HEADS, n_meta), lambda b, s, qi, kj: (0, 0)),
        ],
        out_specs=pl.BlockSpec((tq, ATTN_DIM), lambda b, s, qi, kj: (b * nq + qi[s], 0)),
        scratch_shapes=[pltpu.VMEM((tq, ATTN_DIM), BF16),
                        pltpu.VMEM((ATTN_HEADS, tq, LANES), F32),
                        pltpu.VMEM((ATTN_HEADS, tq, LANES), F32),
                        pltpu.VMEM((tq, ATTN_DIM), F32)],
    )
    return pl.pallas_call(
        functools.partial(_fox_prompt_body, tq=tq, tk=tq),
        out_shape=jax.ShapeDtypeStruct((n_seq * seq_len, ATTN_DIM), BF16),
        grid_spec=grid_spec,
        compiler_params=_cparams(("parallel", "arbitrary"), 48),
        name="fox_prompt",
    )(jnp.asarray(qi), jnp.asarray(kj), u, u, u, slab, ckt, km, vm, bm_t)


def _fox_sample_body(pt_ref, qbd_ref, kn_ref, vn_ref, dtf_ref, bf_ref, et_ref, suf_ref, *rest, npp, n_steps, s_new):
    k_refs = rest[0:npp]
    v_refs = rest[npp:2 * npp]
    lf_refs = rest[2 * npp:3 * npp]
    o_ref, lfo_ref, m_ref, l_ref, acc_ref, carry_ref, cnrow_ref = rest[3 * npp:]
    step = pl.program_id(1)
    rows = ATTN_HEADS * s_new
    q16 = qbd_ref[...].astype(BF16)
    et = et_ref[...]
    nt = (((1,), (1,)), ((), ()))

    @pl.when(step == 0)
    def _():
        lane = lax.broadcasted_iota(jnp.int32, (s_new, LANES), 1)
        lfn = _log_sigmoid(dtf_ref[...] + bf_ref[...])
        lfo_ref[...] = jnp.where((lane >= SSD_HEADS) & (lane < SSD_HEADS + ATTN_HEADS), lfn, 0.0)
        lf16 = lfn[:, SSD_HEADS:SSD_HEADS + ATTN_HEADS]
        tri = (lax.broadcasted_iota(jnp.int32, (s_new, s_new), 1)
               <= lax.broadcasted_iota(jnp.int32, (s_new, s_new), 0)).astype(F32)
        cn = jnp.dot(tri, lf16, precision=HI, preferred_element_type=F32)
        cne = lax.dot_general(et, cn, nt, precision=HI, preferred_element_type=F32)
        trow = lax.broadcasted_iota(jnp.int32, (rows, s_new), 0) % s_new
        tcol = lax.broadcasted_iota(jnp.int32, (rows, s_new), 1)
        cn_row = jnp.sum(jnp.where(tcol == trow, cne, 0.0), axis=-1, keepdims=True)
        cnrow_ref[...] = jnp.broadcast_to(cn_row, cnrow_ref.shape)
        ss = lax.dot_general(q16, kn_ref[...].astype(BF16), nt, preferred_element_type=F32) * ATTN_SCALE
        ss = jnp.where(tcol <= trow, ss + cn_row - cne, -jnp.inf)
        m = jnp.max(ss, axis=-1, keepdims=True)
        p = jnp.exp(ss - m)
        m_ref[...] = jnp.broadcast_to(m, m_ref.shape)
        l_ref[...] = jnp.broadcast_to(jnp.sum(p, axis=-1, keepdims=True), l_ref.shape)
        acc_ref[...] = jnp.dot(p.astype(BF16), vn_ref[...].astype(BF16), preferred_element_type=F32)
        carry_ref[...] = jnp.zeros(carry_ref.shape, F32)

    carry = carry_ref[:, 0:1]
    cn_row = cnrow_ref[:, 0:1]
    suf = suf_ref[...]
    scores = [None] * npp
    for i in reversed(range(npp)):
        a = jnp.dot(et, lf_refs[i][...], precision=HI, preferred_element_type=F32)
        r = jnp.dot(a, suf, precision=HI, preferred_element_type=F32) + carry
        carry = carry + jnp.sum(a, axis=-1, keepdims=True)
        kt = k_refs[i][...].reshape(KV_DIM, PAGE_SIZE).astype(BF16)
        scores[i] = jnp.dot(q16, kt, preferred_element_type=F32) * ATTN_SCALE + r + cn_row
    s_all = jnp.concatenate(scores, axis=1)
    vt_all = jnp.concatenate([v_refs[i][...].reshape(KV_DIM, PAGE_SIZE).astype(BF16) for i in range(npp)], axis=1)
    m_prev = m_ref[...]
    m_new = jnp.maximum(m_prev, jnp.max(s_all, axis=-1, keepdims=True))
    alpha = jnp.exp(m_prev - m_new)
    p = jnp.exp(s_all - m_new[:, 0:1])
    l_ref[...] = alpha * l_ref[...] + jnp.sum(p, axis=-1, keepdims=True)
    m_ref[...] = m_new
    pv = lax.dot_general(p.astype(BF16), vt_all, nt, preferred_element_type=F32)
    acc_ref[...] = jnp.concatenate([alpha, alpha], axis=1) * acc_ref[...] + pv
    carry_ref[...] = jnp.broadcast_to(carry, carry_ref.shape)

    @pl.when(step == n_steps - 1)
    def _():
        inv = 1.0 / l_ref[...]
        o_ref[...] = acc_ref[...] * jnp.concatenate([inv, inv], axis=1)


def _fox_sample(u_s, cache_k, cache_v, cache_logf, page_table, b_forget, n_seq, s_new, npp):
    n_pages = page_table.shape[1]
    assert n_pages % npp == 0 and ATTN_HEADS * s_new == LANES
    n_steps = n_pages // npp
    n_pool = cache_k.shape[0]
    rows = ATTN_HEADS * s_new
    ck = jnp.transpose(cache_k, (0, 2, 3, 1))
    cv = jnp.transpose(cache_v, (0, 2, 3, 1))
    clf = jnp.transpose(cache_logf, (0, 2, 1))
    q = u_s[:, U_Q:U_Q + ATTN_DIM].reshape(n_seq, s_new, ATTN_KV_HEADS, ATTN_REP, HEAD_DIM)
    qbd = jnp.einsum("btgrd,gh->bgrthd", q, jnp.eye(ATTN_KV_HEADS, dtype=F32)).reshape(n_seq, rows, KV_DIM)
    et = np.zeros((rows, ATTN_HEADS), np.float32)
    et[np.arange(rows), np.arange(rows) // s_new] = 1.0
    suf = np.triu(np.ones((PAGE_SIZE, PAGE_SIZE), np.float32), 1).T
    bf = jnp.zeros((1, LANES), F32).at[0, SSD_HEADS:SSD_HEADS + ATTN_HEADS].set(b_forget)

    def page_ix(i):
        return lambda b, s, pt: (pt[b * n_pages + n_pages - (s + 1) * npp + i], 0, 0)

    const2 = lambda b, s, pt: (0, 0)
    in_specs = [
        pl.BlockSpec((None, rows, KV_DIM), lambda b, s, pt: (b, 0, 0)),
        pl.BlockSpec((s_new, KV_DIM), lambda b, s, pt: (b, U_K // KV_DIM)),
        pl.BlockSpec((s_new, KV_DIM), lambda b, s, pt: (b, U_V // KV_DIM)),
        pl.BlockSpec((s_new, LANES), lambda b, s, pt: (b, U_DTF // LANES)),
        pl.BlockSpec((1, LANES), const2),
        pl.BlockSpec((rows, ATTN_HEADS), const2),
        pl.BlockSpec((PAGE_SIZE, PAGE_SIZE), const2),
    ]
    page_ix4 = lambda i: (lambda b, s, pt: page_ix(i)(b, s, pt) + (0,))
    in_specs += [pl.BlockSpec((None, ATTN_KV_HEADS, HEAD_DIM, PAGE_SIZE), page_ix4(i)) for i in range(npp)]
    in_specs += [pl.BlockSpec((None, ATTN_KV_HEADS, HEAD_DIM, PAGE_SIZE), page_ix4(i)) for i in range(npp)]
    in_specs += [pl.BlockSpec((None, ATTN_HEADS, PAGE_SIZE), page_ix(i)) for i in range(npp)]
    grid_spec = pltpu.PrefetchScalarGridSpec(
        num_scalar_prefetch=1,
        grid=(n_seq, n_steps),
        in_specs=in_specs,
        out_specs=(pl.BlockSpec((None, rows, KV_DIM), lambda b, s, pt: (b, 0, 0)),
                   pl.BlockSpec((s_new, LANES), lambda b, s, pt: (b, 0))),
        scratch_shapes=[pltpu.VMEM((rows, LANES), F32), pltpu.VMEM((rows, LANES), F32),
                        pltpu.VMEM((rows, KV_DIM), F32), pltpu.VMEM((rows, LANES), F32),
                        pltpu.VMEM((rows, LANES), F32)],
    )
    o_raw, lf_slab = pl.pallas_call(
        functools.partial(_fox_sample_body, npp=npp, n_steps=n_steps, s_new=s_new),
        out_shape=(jax.ShapeDtypeStruct((n_seq, rows, KV_DIM), F32),
                   jax.ShapeDtypeStruct((n_seq * s_new, LANES), F32)),
        grid_spec=grid_spec,
        compiler_params=_cparams(("parallel", "arbitrary"), 48),
        name="fox_sample",
    )(page_table.reshape(-1), qbd, u_s, u_s, u_s, bf, jnp.asarray(et), jnp.asarray(suf),
      *([ck] * npp), *([cv] * npp), *([clf] * npp))
    o = o_raw.reshape(n_seq, ATTN_KV_HEADS, ATTN_REP, s_new, ATTN_KV_HEADS, HEAD_DIM)
    o = jnp.einsum("bgrtgd->btgrd", o).reshape(n_seq * s_new, ATTN_DIM)
    return o, lf_slab


ROUTE_E0, ROUTE_E1, ROUTE_W0, ROUTE_W1 = 0, 1, 2, 3
ROUTER_EXPERT_LANE0 = N_EXPERT_GROUPS
_BIG_LANE = 4 * LANES


def _sigmoid(x):
    return 1.0 / (1.0 + jnp.exp(-x))


def _mix_body(y_ref, o_ref, gs_ref, ga_ref, h_ref, wssd_ref, wattn_ref, wout_ref, g2_ref, wr_ref, br_ref,
              h1_ref, hn_ref, route_ref):
    ys = jnp.dot(y_ref[...].astype(BF16), wssd_ref[...], preferred_element_type=F32)
    oa = jnp.dot(o_ref[...].astype(BF16), wattn_ref[...], preferred_element_type=F32)
    mix = _sigmoid(gs_ref[...]) * ys + _sigmoid(ga_ref[...]) * oa
    h1 = h_ref[...] + jnp.dot(mix.astype(BF16), wout_ref[...], preferred_element_type=F32)
    h1_ref[...] = h1
    ms = jnp.mean(h1 * h1, axis=-1, keepdims=True)
    hn = (h1 * lax.rsqrt(ms + RMS_EPS)) * g2_ref[...]
    hn_ref[...] = hn
    logits = jnp.dot(hn, wr_ref[...], precision=HI, preferred_element_type=F32) + br_ref[...]
    lane = lax.broadcasted_iota(jnp.int32, logits.shape, 1)
    gl = jnp.where(lane < N_EXPERT_GROUPS, logits, -jnp.inf)
    gmax = jnp.max(gl, axis=-1, keepdims=True)
    gsel = jnp.min(jnp.where(gl == gmax, lane, _BIG_LANE), axis=-1, keepdims=True)
    wgrp = 1.0 / jnp.sum(jnp.exp(gl - gmax), axis=-1, keepdims=True)
    elane = lane - ROUTER_EXPERT_LANE0
    in_group = (elane >= gsel * EXPERTS_PER_GROUP) & (elane < (gsel + 1) * EXPERTS_PER_GROUP)
    el = jnp.where(in_group, logits, -jnp.inf)
    t1 = jnp.max(el, axis=-1, keepdims=True)
    i1 = jnp.min(jnp.where(el == t1, lane, _BIG_LANE), axis=-1, keepdims=True)
    el2 = jnp.where(lane == i1, -jnp.inf, el)
    t2 = jnp.max(el2, axis=-1, keepdims=True)
    i2 = jnp.min(jnp.where(el2 == t2, lane, _BIG_LANE), axis=-1, keepdims=True)
    e21 = jnp.exp(t2 - t1)
    w1 = wgrp / (1.0 + e21)
    w2 = w1 * e21
    route = jnp.where(lane == ROUTE_E0, (i1 - ROUTER_EXPERT_LANE0).astype(F32),
                      jnp.where(lane == ROUTE_E1, (i2 - ROUTER_EXPERT_LANE0).astype(F32),
                                jnp.where(lane == ROUTE_W0, w1, jnp.where(lane == ROUTE_W1, w2, 0.0))))
    route_ref[...] = route


def _mix(y, o, u, h, p, tm):
    t = h.shape[0]
    assert t % tm == 0
    row = lambda i: (i, 0)
    const = lambda i: (0, 0)
    return pl.pallas_call(
        _mix_body,
        out_shape=(jax.ShapeDtypeStruct((t, D_MODEL), F32), jax.ShapeDtypeStruct((t, D_MODEL), F32),
                   jax.ShapeDtypeStruct((t, LANES), F32)),
        grid=(t // tm,),
        in_specs=[
            pl.BlockSpec((tm, D_INNER), row),
            pl.BlockSpec((tm, ATTN_DIM), row),
            pl.BlockSpec((tm, D_MODEL), lambda i: (i, U_GS // D_MODEL)),
            pl.BlockSpec((tm, D_MODEL), lambda i: (i, U_GA // D_MODEL)),
            pl.BlockSpec((tm, D_MODEL), row),
            pl.BlockSpec((D_INNER, D_MODEL), const),
            pl.BlockSpec((ATTN_DIM, D_MODEL), const),
            pl.BlockSpec((D_MODEL, D_MODEL), const),
            pl.BlockSpec((1, D_MODEL), const),
            pl.BlockSpec((D_MODEL, LANES), const),
            pl.BlockSpec((1, LANES), const),
        ],
        out_specs=(pl.BlockSpec((tm, D_MODEL), row), pl.BlockSpec((tm, D_MODEL), row),
                   pl.BlockSpec((tm, LANES), row)),
        compiler_params=_cparams(("parallel",), 56),
        name="mix_route",
    )(y, o, u, u, h, p["w_ssd_br"], p["w_attn_br"], p["w_out"], p["norm2_g"], p["w_router"], p["b_router"])


def _mix_params(w_ssd_br, w_attn_br, w_out, norm2_g, w_rg, b_rg, w_re, b_re):
    n_r = N_EXPERT_GROUPS + N_EXPERTS
    w_router = jnp.concatenate([w_rg, w_re, jnp.zeros((D_MODEL, LANES - n_r), F32)], axis=1)
    b_router = jnp.concatenate([b_rg, b_re, jnp.zeros((LANES - n_r,), F32)]).reshape(1, LANES)
    return dict(w_ssd_br=w_ssd_br.astype(BF16), w_attn_br=w_attn_br.astype(BF16), w_out=w_out.astype(BF16),
                norm2_g=norm2_g.reshape(1, D_MODEL), w_router=w_router, b_router=b_router)


def _moe_rank_body(route_ref, pos_ref, cnt_ref, carry_ref, *, tm):
    @pl.when(pl.program_id(0) == 0)
    def _():
        carry_ref[...] = jnp.zeros(carry_ref.shape, F32)

    route = route_ref[...]
    lane = lax.broadcasted_iota(jnp.int32, (tm, LANES), 1).astype(F32)
    hit0 = lane == route[:, ROUTE_E0:ROUTE_E0 + 1]
    hit1 = lane == route[:, ROUTE_E1:ROUTE_E1 + 1]
    onehot = hit0.astype(F32) + hit1.astype(F32)
    before = (lax.broadcasted_iota(jnp.int32, (tm, tm), 1) < lax.broadcasted_iota(jnp.int32, (tm, tm), 0))
    c = jnp.dot(before.astype(BF16), onehot.astype(BF16), preferred_element_type=F32) + carry_ref[...]
    pos0 = jnp.sum(jnp.where(hit0, c, 0.0), axis=-1, keepdims=True)
    pos1 = jnp.sum(jnp.where(hit1, c, 0.0), axis=-1, keepdims=True)
    pos_ref[...] = jnp.where(lane == 0.0, pos0, jnp.where(lane == 1.0, pos1, 0.0))
    total = carry_ref[...] + jnp.sum(onehot, axis=0, keepdims=True)
    carry_ref[...] = total
    cnt_ref[...] = total


def _moe_rank(route, tm):
    t = route.shape[0]
    assert t % tm == 0
    return pl.pallas_call(
        functools.partial(_moe_rank_body, tm=tm),
        out_shape=(jax.ShapeDtypeStruct((t, LANES), F32), jax.ShapeDtypeStruct((1, LANES), F32)),
        grid=(t // tm,),
        in_specs=[pl.BlockSpec((tm, LANES), lambda i: (i, 0))],
        out_specs=(pl.BlockSpec((tm, LANES), lambda i: (i, 0)), pl.BlockSpec((1, LANES), lambda i: (0, 0))),
        scratch_shapes=[pltpu.VMEM((1, LANES), F32)],
        compiler_params=_cparams(("arbitrary",)),
        name="moe_rank",
    )(route)


def _experts_body(be_ref, nu_ref, x_ref, wg_ref, wu_ref, wd_ref, y_ref, wg16_ref, wu16_ref, wd16_ref):
    i = pl.program_id(0)
    prev = be_ref[jnp.maximum(i - 1, 0)]

    @pl.when((i == 0) | (be_ref[i] != prev))
    def _():
        wg16_ref[...] = wg_ref[...].astype(BF16)
        wu16_ref[...] = wu_ref[...].astype(BF16)
        wd16_ref[...] = wd_ref[...].astype(BF16)

    @pl.when(i < nu_ref[0])
    def _():
        x16 = x_ref[...].astype(BF16)
        hb = _silu(jnp.dot(x16, wg16_ref[...], preferred_element_type=F32)) * jnp.dot(
            x16, wu16_ref[...], preferred_element_type=F32)
        y_ref[...] = jnp.dot(hb.astype(BF16), wd16_ref[...], preferred_element_type=F32)

    @pl.when(i >= nu_ref[0])
    def _():
        y_ref[...] = jnp.zeros(y_ref.shape, F32)


def _experts(xs, block_e, n_used, w_gate, w_up, w_down, blk):
    rows = xs.shape[0]
    n_blocks = rows // blk
    grid_spec = pltpu.PrefetchScalarGridSpec(
        num_scalar_prefetch=2,
        grid=(n_blocks,),
        in_specs=[
            pl.BlockSpec((blk, D_MODEL), lambda i, be, nu: (i, 0)),
            pl.BlockSpec((None, D_MODEL, D_EXPERT), lambda i, be, nu: (be[i], 0, 0)),
            pl.BlockSpec((None, D_MODEL, D_EXPERT), lambda i, be, nu: (be[i], 0, 0)),
            pl.BlockSpec((None, D_EXPERT, D_MODEL), lambda i, be, nu: (be[i], 0, 0)),
        ],
        out_specs=pl.BlockSpec((blk, D_MODEL), lambda i, be, nu: (i, 0)),
        scratch_shapes=[pltpu.VMEM((D_MODEL, D_EXPERT), BF16), pltpu.VMEM((D_MODEL, D_EXPERT), BF16),
                        pltpu.VMEM((D_EXPERT, D_MODEL), BF16)],
    )
    return pl.pallas_call(
        _experts_body,
        out_shape=jax.ShapeDtypeStruct((rows, D_MODEL), F32),
        grid_spec=grid_spec,
        compiler_params=_cparams(("arbitrary",), 48),
        name="moe_experts",
    )(block_e, n_used, xs, w_gate, w_up, w_down)


def _combine_body(h1_ref, y0_ref, y1_ref, route_ref, g_ref, o_ref):
    route = route_ref[...]
    h = h1_ref[...] + route[:, ROUTE_W0:ROUTE_W0 + 1] * y0_ref[...] + route[:, ROUTE_W1:ROUTE_W1 + 1] * y1_ref[...]
    ms = jnp.mean(h * h, axis=-1, keepdims=True)
    o_ref[...] = (h * lax.rsqrt(ms + RMS_EPS)) * g_ref[...]


def _combine(h1, y0, y1, route, g, tm):
    t = h1.shape[0]
    row = lambda i: (i, 0)
    return pl.pallas_call(
        _combine_body,
        out_shape=jax.ShapeDtypeStruct((t, D_MODEL), F32),
        grid=(t // tm,),
        in_specs=[pl.BlockSpec((tm, D_MODEL), row), pl.BlockSpec((tm, D_MODEL), row), pl.BlockSpec((tm, D_MODEL), row),
                  pl.BlockSpec((tm, LANES), row), pl.BlockSpec((1, D_MODEL), lambda i: (0, 0))],
        out_specs=pl.BlockSpec((tm, D_MODEL), row),
        compiler_params=_cparams(("parallel",)),
        name="moe_combine_norm",
    )(h1, y0, y1, route, g.reshape(1, D_MODEL))


MOE_ROWS_PER_BLOCK = 256


def _moe(hn, h1, route, w_gate, w_up, w_down, final_g):
    t = hn.shape[0]
    blk = MOE_ROWS_PER_BLOCK
    pos, cnt = _moe_rank(route, 256)
    counts = cnt[0, :N_EXPERTS].astype(jnp.int32)
    padded = (counts + blk - 1) // blk * blk
    ends = jnp.cumsum(padded)
    starts = ends - padded
    e0 = route[:, ROUTE_E0].astype(jnp.int32)
    e1 = route[:, ROUTE_E1].astype(jnp.int32)
    d0 = starts[e0] + pos[:, 0].astype(jnp.int32)
    d1 = starts[e1] + pos[:, 1].astype(jnp.int32)
    n_blocks = (2 * t + N_EXPERTS * (blk - 1) + blk - 1) // blk
    rows = n_blocks * blk
    block_e = jnp.minimum(jnp.searchsorted(ends, jnp.arange(n_blocks, dtype=jnp.int32) * blk, side="right"),
                          N_EXPERTS - 1).astype(jnp.int32)
    n_used = (ends[-1] // blk).astype(jnp.int32).reshape(1)
    tok = jnp.arange(t, dtype=jnp.int32)
    row_tok = jnp.zeros((rows,), jnp.int32).at[d0].set(tok).at[d1].set(tok)
    xs = hn[row_tok]
    ys = _experts(xs, block_e, n_used, w_gate, w_up, w_down, blk)
    return _combine(h1, ys[d0], ys[d1], route, final_g, 256)


def _conv_history(rows):
    n = rows.shape[0]
    return jnp.concatenate([jnp.zeros((n, CONV_PAD - (CONV_W - 1), CONV_DIM), F32), rows], axis=1)


def _ucols(u, start, width):
    return u[:, start:start + width]


def kernel(x_prompt, x_sample, cache_k, cache_v, cache_logf, state_ssm, state_conv, page_table, meta_tokens, norm1_g, w_in, conv_w, conv_b, dt_bias, a_log, d_skip, ssd_norm_g, b_forget, w_ssd_br, w_attn_br, w_out, norm2_g, w_router_group, b_router_group, w_router_expert, b_router_expert, w_exp_gate, w_exp_up, w_exp_down, final_norm_g):
    nb, sl, _ = x_prompt.shape
    sb, ss, _ = x_sample.shape
    assert w_in.shape[0] == 1, "single-layer step"
    ly = 0
    xp = x_prompt.reshape(nb * sl, D_MODEL)
    xs = x_sample.reshape(sb * ss, D_MODEL)

    wp = _pack_w_in(w_in[ly])
    u_p = _inproj(xp, norm1_g[ly], wp)
    u_s = _inproj(xs, norm1_g[ly], wp)
    u_m = _inproj(meta_tokens, norm1_g[ly], wp)

    sp = _ssd_params(conv_w[ly], conv_b[ly], dt_bias[ly], a_log[ly], d_skip[ly], ssd_norm_g[ly])
    zero_hist = jnp.zeros((1, CONV_PAD, CONV_DIM), F32)
    zero_state = jnp.zeros((1, D_INNER, D_STATE), F32)
    _, h_meta = _ssd(u_m, 1, N_META, N_META, zero_hist, zero_state, sp, BF16)
    hist_meta = _conv_history(_ucols(u_m, U_XBC, CONV_DIM)[None, N_META - (CONV_W - 1):])
    y_p, ssm_p = _ssd(u_p, nb, sl, SSD_CHUNK, hist_meta, h_meta, sp, BF16)
    y_s, ssm_s = _ssd(u_s, sb, ss, ss, _conv_history(state_conv[ly]),
                      state_ssm[ly].reshape(sb, D_INNER, D_STATE), sp, F32)

    slab_m, _ = _logf(u_m, U_DTF // LANES, b_forget[ly], N_META, N_META)
    c_meta = slab_m[:, :ATTN_HEADS]
    bias_meta_t = (c_meta - c_meta[N_META - 1:N_META]).T
    slab_p, ckt_p = _logf(u_p, U_DTF // LANES, b_forget[ly], nb * sl, sl)
    k_m = _ucols(u_m, U_K, KV_DIM)
    v_m = _ucols(u_m, U_V, KV_DIM)
    o_p = _fox_prompt(u_p, slab_p, ckt_p, k_m, v_m, bias_meta_t, nb, sl, 256)
    o_s, slab_s = _fox_sample(u_s, cache_k[ly], cache_v[ly], cache_logf[ly], page_table, b_forget[ly], sb, ss, 8)

    mp = _mix_params(w_ssd_br[ly], w_attn_br[ly], w_out[ly], norm2_g[ly], w_router_group[ly], b_router_group[ly],
                     w_router_expert[ly], b_router_expert[ly])
    h1_p, hn_p, route_p = _mix(y_p, o_p, u_p, xp, mp, 256)
    h1_s, hn_s, route_s = _mix(y_s, o_s, u_s, xs, mp, 256)
    out_p = _moe(hn_p, h1_p, route_p, w_exp_gate[ly], w_exp_up[ly], w_exp_down[ly], final_norm_g)
    out_s = _moe(hn_s, h1_s, route_s, w_exp_gate[ly], w_exp_up[ly], w_exp_down[ly], final_norm_g)

    def with_meta(meta_rows, rows, width):
        m = jnp.broadcast_to(meta_rows[None], (nb, N_META, width))
        return jnp.concatenate([m, rows.reshape(nb, sl, width)], axis=1)[None]

    lf_m = slab_m[:, SSD_HEADS:SSD_HEADS + ATTN_HEADS]
    lf_p = slab_p[:, SSD_HEADS:SSD_HEADS + ATTN_HEADS]
    lf_s = slab_s[:, SSD_HEADS:SSD_HEADS + ATTN_HEADS]
    kv_shape_p = (1, nb, sl + N_META, ATTN_KV_HEADS, HEAD_DIM)
    kv_shape_s = (1, sb, ss, ATTN_KV_HEADS, HEAD_DIM)
    state_shape = (SSD_HEADS, SSD_HEADDIM, D_STATE)
    tail = CONV_W - 1
    return (
        out_p.reshape(nb, sl, D_MODEL),
        out_s.reshape(sb, ss, D_MODEL),
        with_meta(k_m, _ucols(u_p, U_K, KV_DIM), KV_DIM).reshape(kv_shape_p),
        with_meta(v_m, _ucols(u_p, U_V, KV_DIM), KV_DIM).reshape(kv_shape_p),
        with_meta(lf_m, lf_p, ATTN_HEADS),
        ssm_p.reshape((1, nb) + state_shape),
        u_p.reshape(nb, sl, U_COLS)[None, :, sl - tail:, U_XBC:U_XBC + CONV_DIM],
        _ucols(u_s, U_K, KV_DIM).reshape(kv_shape_s),
        _ucols(u_s, U_V, KV_DIM).reshape(kv_shape_s),
        lf_s.reshape(1, sb, ss, ATTN_HEADS),
        ssm_s.reshape((1, sb) + state_shape),
        u_s.reshape(sb, ss, U_COLS)[None, :, ss - tail:, U_XBC:U_XBC + CONV_DIM],
    )
```

```python
import functools

import numpy as np
import jax
import jax.numpy as jnp
from jax import lax
from jax.experimental import pallas as pl
from jax.experimental.pallas import tpu as pltpu

F32 = jnp.float32
BF16 = jnp.bfloat16
HI = lax.Precision.HIGHEST

D_MODEL = 1024
N_META = 16
RMS_EPS = 1e-6
D_INNER = 2048
SSD_HEADDIM = 64
SSD_HEADS = 32
SSD_GROUPS = 8
HEADS_PER_GROUP = SSD_HEADS // SSD_GROUPS
GROUP_W = HEADS_PER_GROUP * SSD_HEADDIM
D_STATE = 128
CONV_W = 4
CONV_DIM = 4096
SSD_CHUNK = 128
ATTN_HEADS = 16
ATTN_KV_HEADS = 4
HEAD_DIM = 64
ATTN_REP = 4
ATTN_DIM = 1024
KV_DIM = 256
ATTN_SCALE = HEAD_DIM ** -0.5
PAGE_SIZE = 128
N_EXPERT_GROUPS = 4
EXPERTS_PER_GROUP = 8
N_EXPERTS = 32
D_EXPERT = 512
LANES = 128

U_XBC = 0
U_Z = 4096
U_Q = 6144
U_GS = 7168
U_GA = 8192
U_K = 9216
U_V = 9472
U_DTF = 9728
U_COLS = 9856
U_COLS_PADDED = 10240
INPROJ_TN = 1024


def _cparams(sem, vmem_mb=None):
    kw = dict(dimension_semantics=sem)
    if vmem_mb is not None:
        kw["vmem_limit_bytes"] = vmem_mb * 1024 * 1024
    return pltpu.CompilerParams(**kw)


def _inproj_body(x_ref, g_ref, w_ref, o_ref, xn_ref):
    @pl.when(pl.program_id(1) == 0)
    def _():
        x = x_ref[...]
        ms = jnp.mean(x * x, axis=-1, keepdims=True)
        xn_ref[...] = ((x * lax.rsqrt(ms + RMS_EPS)) * g_ref[...]).astype(BF16)

    o_ref[...] = jnp.dot(xn_ref[...], w_ref[...], preferred_element_type=F32)


def _inproj(x, g, wp):
    t = x.shape[0]
    tm = min(t, 1024)
    assert t % tm == 0
    return pl.pallas_call(
        _inproj_body,
        out_shape=jax.ShapeDtypeStruct((t, U_COLS), F32),
        grid=(t // tm, U_COLS_PADDED // INPROJ_TN),
        in_specs=[
            pl.BlockSpec((tm, D_MODEL), lambda i, j: (i, 0)),
            pl.BlockSpec((1, D_MODEL), lambda i, j: (0, 0)),
            pl.BlockSpec((D_MODEL, INPROJ_TN), lambda i, j: (0, j)),
        ],
        out_specs=pl.BlockSpec((tm, INPROJ_TN), lambda i, j: (i, j)),
        scratch_shapes=[pltpu.VMEM((tm, D_MODEL), BF16)],
        compiler_params=_cparams(("parallel", "arbitrary"), 48),
        name="inproj",
    )(x, g.reshape(1, D_MODEL), wp)


def _pack_w_in(w_in):
    o = np.cumsum([0, D_INNER, CONV_DIM, SSD_HEADS, ATTN_DIM, KV_DIM, KV_DIM, ATTN_HEADS, D_MODEL, D_MODEL])
    z, xbc, dt, q, k, v, f, gs, ga = [w_in[:, o[i]:o[i + 1]] for i in range(9)]
    pad_dtf = jnp.zeros((D_MODEL, LANES - SSD_HEADS - ATTN_HEADS), w_in.dtype)
    pad = jnp.zeros((D_MODEL, U_COLS_PADDED - U_COLS), w_in.dtype)
    return jnp.concatenate([xbc, z, q, gs, ga, k, v, dt, f, pad_dtf, pad], axis=1).astype(BF16)


def _log_sigmoid(x):
    return jnp.minimum(x, 0.0) - jnp.log1p(jnp.exp(-jnp.abs(x)))


LOG2E = 1.4426950408889634
QK_ONES_LANE = HEAD_DIM
QK_CK_LANE = HEAD_DIM + 3
V_ONES_ROW = HEAD_DIM
N_SPLIT = 3
ATTN_TQ = 256


def _placement_matrices():
    eq = np.zeros((LANES, ATTN_HEADS * LANES), np.float32)
    ek = np.zeros((LANES, ATTN_KV_HEADS * LANES), np.float32)
    for h in range(ATTN_HEADS):
        g, r = divmod(h, ATTN_REP)
        for s in range(N_SPLIT):
            eq[s * ATTN_HEADS + h, h * LANES + QK_ONES_LANE + s] = 1.0
            ek[s * ATTN_HEADS + h, g * LANES + QK_CK_LANE + s * ATTN_REP + r] = 1.0
    cq = np.zeros((1, ATTN_HEADS * LANES), np.float32)
    for h in range(ATTN_HEADS):
        r = h % ATTN_REP
        for s in range(N_SPLIT):
            cq[0, h * LANES + QK_CK_LANE + s * ATTN_REP + r] = -1.0
    return jnp.asarray(eq, BF16), jnp.asarray(ek, BF16), jnp.asarray(cq)


def _transpose_rows(x, q):
    if q < LANES:
        x = jnp.concatenate([x, jnp.zeros((LANES - q, x.shape[1]), x.dtype)], axis=0)
    return x.T[:, :q]


def _split3(x):
    hi = x.astype(BF16)
    r1 = x - hi.astype(F32)
    mid = r1.astype(BF16)
    lo = (r1 - mid.astype(F32)).astype(BF16)
    return hi, mid, lo


def _split3_stacked(x, axis):
    return jnp.concatenate([t.astype(F32) for t in _split3(x)], axis=axis).astype(BF16)


def _attn_prep_body(q_ref, k_ref, v_ref, dtf_ref, bf_ref, eq_ref, ek_ref, cq_ref, slab_ref, qp_ref, kp_ref, vpt_ref,
                    kt_ref, vt_ref, carry_ref, *, tiles_per_seq, tl, rel_to_last):
    i = pl.program_id(0)

    @pl.when(i % tiles_per_seq == 0)
    def _():
        carry_ref[...] = jnp.zeros_like(carry_ref)

    lf = _log_sigmoid(dtf_ref[...] + bf_ref[...])
    row = lax.broadcasted_iota(jnp.int32, (tl, tl), 0)
    col = lax.broadcasted_iota(jnp.int32, (tl, tl), 1)
    c = jnp.dot((col <= row).astype(F32), lf, precision=HI, preferred_element_type=F32) + carry_ref[...]
    carry_ref[...] = c[tl - 1:tl, :]
    lane = lax.broadcasted_iota(jnp.int32, (tl, LANES), 1)
    c16 = jnp.where(lane < ATTN_HEADS, pltpu.roll(c, LANES - SSD_HEADS, 1), 0.0)
    slab_ref[...] = jnp.where((lane >= SSD_HEADS) & (lane < SSD_HEADS + ATTN_HEADS), lf, c16)
    if rel_to_last:
        c16 = c16 - c16[tl - 1:tl, :]
    hi, mid, lo = _split3(c16 * LOG2E)
    x = jnp.where(lane < ATTN_HEADS, hi.astype(F32),
                  jnp.where(lane < 2 * ATTN_HEADS, pltpu.roll(mid.astype(F32), ATTN_HEADS, 1),
                            pltpu.roll(lo.astype(F32), 2 * ATTN_HEADS, 1))).astype(BF16)
    q_extra = jnp.dot(x, eq_ref[...], preferred_element_type=F32) + cq_ref[...]
    k_extra = jnp.dot(x, ek_ref[...], preferred_element_type=F32)
    low = lane < HEAD_DIM
    for h in range(ATTN_HEADS):
        qx = q_ref[:, (h // 2) * LANES:(h // 2 + 1) * LANES]
        if h % 2:
            qx = pltpu.roll(qx, HEAD_DIM, 1)
        tile = jnp.where(low, qx * (ATTN_SCALE * LOG2E), q_extra[:, h * LANES:(h + 1) * LANES])
        qp_ref[:, h * LANES:(h + 1) * LANES] = tile.astype(BF16)
    ones_k = ((lane >= QK_ONES_LANE) & (lane < QK_CK_LANE)).astype(F32)
    ones_v = (lane == V_ONES_ROW).astype(F32)
    for g in range(ATTN_KV_HEADS):
        kx = k_ref[:, (g // 2) * LANES:(g // 2 + 1) * LANES]
        vx = v_ref[:, (g // 2) * LANES:(g // 2 + 1) * LANES]
        if g % 2:
            kx = pltpu.roll(kx, HEAD_DIM, 1)
            vx = pltpu.roll(vx, HEAD_DIM, 1)
        kp_ref[:, g * LANES:(g + 1) * LANES] = jnp.where(low, kx, k_extra[:, g * LANES:(g + 1) * LANES] + ones_k
                                                         ).astype(BF16)
        vpt_ref[g * LANES:(g + 1) * LANES, :] = _transpose_rows(jnp.where(low, vx, ones_v), tl).astype(BF16)
    kt_ref[...] = _transpose_rows(k_ref[...], tl)
    vt_ref[...] = _transpose_rows(v_ref[...], tl)


def _attn_prep(u, b_forget, n_tokens, seq_len, rel_to_last=False):
    tl = min(n_tokens, 256)
    assert n_tokens % tl == 0 and seq_len % tl == 0
    bf = jnp.zeros((1, LANES), F32).at[0, SSD_HEADS:SSD_HEADS + ATTN_HEADS].set(b_forget)
    eq, ek, cq = _placement_matrices()
    const = lambda i: (0, 0)
    return pl.pallas_call(
        functools.partial(_attn_prep_body, tiles_per_seq=seq_len // tl, tl=tl, rel_to_last=rel_to_last),
        out_shape=(jax.ShapeDtypeStruct((n_tokens, LANES), F32),
                   jax.ShapeDtypeStruct((n_tokens, ATTN_HEADS * LANES), BF16),
                   jax.ShapeDtypeStruct((n_tokens, ATTN_KV_HEADS * LANES), BF16),
                   jax.ShapeDtypeStruct((ATTN_KV_HEADS * LANES, n_tokens), BF16),
                   jax.ShapeDtypeStruct((KV_DIM, n_tokens), F32),
                   jax.ShapeDtypeStruct((KV_DIM, n_tokens), F32)),
        grid=(n_tokens // tl,),
        in_specs=[
            pl.BlockSpec((tl, ATTN_DIM), lambda i: (i, U_Q // ATTN_DIM)),
            pl.BlockSpec((tl, KV_DIM), lambda i: (i, U_K // KV_DIM)),
            pl.BlockSpec((tl, KV_DIM), lambda i: (i, U_V // KV_DIM)),
            pl.BlockSpec((tl, LANES), lambda i: (i, U_DTF // LANES)),
            pl.BlockSpec((1, LANES), const),
            pl.BlockSpec((LANES, ATTN_HEADS * LANES), const),
            pl.BlockSpec((LANES, ATTN_KV_HEADS * LANES), const),
            pl.BlockSpec((1, ATTN_HEADS * LANES), const),
        ],
        out_specs=(pl.BlockSpec((tl, LANES), lambda i: (i, 0)),
                   pl.BlockSpec((tl, ATTN_HEADS * LANES), lambda i: (i, 0)),
                   pl.BlockSpec((tl, ATTN_KV_HEADS * LANES), lambda i: (i, 0)),
                   pl.BlockSpec((ATTN_KV_HEADS * LANES, tl), lambda i: (0, i)),
                   pl.BlockSpec((KV_DIM, tl), lambda i: (0, i)),
                   pl.BlockSpec((KV_DIM, tl), lambda i: (0, i))),
        scratch_shapes=[pltpu.VMEM((1, LANES), F32)],
        compiler_params=_cparams(("arbitrary",), 40),
        name="attn_prep",
    )(u, u, u, u, bf, eq, ek, cq)


CONV_PAD = 8


def _silu(x):
    return x * (1.0 / (1.0 + jnp.exp(-x)))


def _softplus(x):
    return jnp.maximum(x, 0.0) + jnp.log1p(jnp.exp(-jnp.abs(x)))


def _ssd_body(xbc_ref, z_ref, dtf_ref, conv0_ref, h0_ref, cw_ref, cb_ref, dtb_ref, alog_ref, dskip_ref,
              gn_ref, e_ref, y_ref, hout_ref, xconv_ref, ht_ref, *, q, n_chunks):
    c = pl.program_id(1)

    @pl.when(c == 0)
    def _():
        xconv_ref[0:CONV_PAD, :] = conv0_ref[0]
        for g in range(SSD_GROUPS):
            ht_ref[g] = h0_ref[0, g * GROUP_W:(g + 1) * GROUP_W, :].T

    xconv_ref[CONV_PAD:CONV_PAD + q, :] = xbc_ref[...]
    acc = cb_ref[...]
    for k in range(CONV_W):
        off = CONV_PAD - (CONV_W - 1) + k
        acc = acc + xconv_ref[off:off + q, :] * cw_ref[k:k + 1, :]
    xc = _silu(acc)
    xconv_ref[CONV_PAD - (CONV_W - 1):CONV_PAD, :] = xconv_ref[CONV_PAD + q - (CONV_W - 1):CONV_PAD + q, :]

    dt = _softplus(dtf_ref[...] + dtb_ref[...])
    a = -jnp.exp(alog_ref[...])
    row = lax.broadcasted_iota(jnp.int32, (q, q), 0)
    col = lax.broadcasted_iota(jnp.int32, (q, q), 1)
    causal = col <= row
    tri = causal.astype(BF16)
    acum = jnp.dot(jnp.concatenate([tri] * N_SPLIT, axis=1), _split3_stacked(dt * a, 0),
                   preferred_element_type=F32)
    acum_t = _transpose_rows(acum, q)
    a_last = acum[q - 1:q, :]
    fac = jnp.concatenate([jnp.exp(acum), jnp.exp(a_last - acum) * dt, dt], axis=0)
    fac = jnp.dot(_split3_stacked(fac, 1), e_ref[...], preferred_element_type=F32)
    ea_full, wst_full, dt_full = fac[0:q], fac[q:2 * q], fac[2 * q:3 * q]

    def group_bc(g):
        bg = xc[:, D_INNER + g * D_STATE:D_INNER + (g + 1) * D_STATE]
        cg = xc[:, D_INNER + SSD_GROUPS * D_STATE + g * D_STATE:D_INNER + SSD_GROUPS * D_STATE + (g + 1) * D_STATE]
        cg16 = cg.astype(BF16)
        cbm = lax.dot_general(cg16, bg.astype(BF16), (((1,), (1,)), ((), ())), preferred_element_type=F32)
        return bg, cg16, cbm

    nxt = group_bc(0)
    for g in range(SSD_GROUPS):
        gs = slice(g * GROUP_W, (g + 1) * GROUP_W)
        xg = xc[:, gs]
        bg, cg16, cbm = nxt
        if g + 1 < SSD_GROUPS:
            nxt = group_bc(g + 1)
        htg = ht_ref[g]
        yoff = jnp.dot(cg16, htg.astype(BF16), preferred_element_type=F32) * ea_full[:, gs]
        xw = (xg * wst_full[:, gs]).astype(BF16)
        bgt = _transpose_rows(bg, q).astype(BF16)
        st = jnp.dot(bgt, xw, preferred_element_type=F32)
        ht_ref[g] = ea_full[q - 1:q, gs] * htg + st
        xdt = (xg * dt_full[:, gs]).astype(BF16)
        yd = []
        for r in range(HEADS_PER_GROUP):
            h = g * HEADS_PER_GROUP + r
            seg = acum[:, h:h + 1] - acum_t[h:h + 1, :]
            m = cbm * jnp.exp(jnp.where(causal, seg, -jnp.inf))
            yd.append(jnp.dot(m.astype(BF16), xdt[:, r * SSD_HEADDIM:(r + 1) * SSD_HEADDIM],
                              preferred_element_type=F32))
        yd = jnp.concatenate(yd, axis=1)
        yg = yd + yoff + dskip_ref[:, gs] * xg
        yz = yg * _silu(z_ref[:, gs])
        ms = jnp.mean(yz * yz, axis=-1, keepdims=True)
        y_ref[:, gs] = (yz * lax.rsqrt(ms + RMS_EPS) * gn_ref[:, gs]).astype(y_ref.dtype)

    @pl.when(c == n_chunks - 1)
    def _():
        for g in range(SSD_GROUPS):
            hout_ref[0, g * GROUP_W:(g + 1) * GROUP_W, :] = ht_ref[g].T


def _head_expand_matrix():
    e = np.zeros((LANES, D_INNER), np.float32)
    for h in range(SSD_HEADS):
        e[h, h * SSD_HEADDIM:(h + 1) * SSD_HEADDIM] = 1.0
    return jnp.asarray(np.tile(e, (N_SPLIT, 1)), BF16)


def _ssd(u, n_seq, seq_len, q, conv0, h0, p, y_dtype):
    n_chunks = seq_len // q
    assert seq_len % q == 0 and q % 8 == 0
    conv_ix = (lambda b, c: (b, 0, 0)) if conv0.shape[0] == n_seq and n_seq > 1 else (lambda b, c: (0, 0, 0))
    h_ix = (lambda b, c: (b, 0, 0)) if h0.shape[0] == n_seq and n_seq > 1 else (lambda b, c: (0, 0, 0))
    const2 = lambda b, c: (0, 0)
    return pl.pallas_call(
        functools.partial(_ssd_body, q=q, n_chunks=n_chunks),
        out_shape=(jax.ShapeDtypeStruct((n_seq * seq_len, D_INNER), y_dtype),
                   jax.ShapeDtypeStruct((n_seq, D_INNER, D_STATE), F32)),
        grid=(n_seq, n_chunks),
        in_specs=[
            pl.BlockSpec((q, CONV_DIM), lambda b, c: (b * n_chunks + c, U_XBC // CONV_DIM)),
            pl.BlockSpec((q, D_INNER), lambda b, c: (b * n_chunks + c, U_Z // D_INNER)),
            pl.BlockSpec((q, LANES), lambda b, c: (b * n_chunks + c, U_DTF // LANES)),
            pl.BlockSpec((1, CONV_PAD, CONV_DIM), conv_ix),
            pl.BlockSpec((1, D_INNER, D_STATE), h_ix),
            pl.BlockSpec((CONV_W, CONV_DIM), const2),
            pl.BlockSpec((1, CONV_DIM), const2),
            pl.BlockSpec((1, LANES), const2),
            pl.BlockSpec((1, LANES), const2),
            pl.BlockSpec((1, D_INNER), const2),
            pl.BlockSpec((1, D_INNER), const2),
            pl.BlockSpec((N_SPLIT * LANES, D_INNER), const2),
        ],
        out_specs=(pl.BlockSpec((q, D_INNER), lambda b, c: (b * n_chunks + c, 0)),
                   pl.BlockSpec((1, D_INNER, D_STATE), lambda b, c: (b, 0, 0))),
        scratch_shapes=[pltpu.VMEM((CONV_PAD + q, CONV_DIM), F32),
                        pltpu.VMEM((SSD_GROUPS, D_STATE, GROUP_W), F32)],
        compiler_params=_cparams(("parallel", "arbitrary"), 48),
        name="ssd_q%d" % q,
    )(u, u, u, conv0, h0, p["conv_w"], p["conv_b"], p["dt_bias"], p["a_log"], p["d_skip"], p["ssd_norm_g"],
      p["head_expand"])


def _ssd_params(conv_w, conv_b, dt_bias, a_log, d_skip, ssd_norm_g):
    pad32 = lambda v: jnp.zeros((1, LANES), F32).at[0, :SSD_HEADS].set(v)
    return dict(conv_w=conv_w, conv_b=conv_b.reshape(1, CONV_DIM), dt_bias=pad32(dt_bias), a_log=pad32(a_log),
                d_skip=jnp.repeat(d_skip, SSD_HEADDIM).reshape(1, D_INNER),
                ssd_norm_g=ssd_norm_g.reshape(1, D_INNER), head_expand=_head_expand_matrix())


def _fox_prompt_body(qi_ref, kj_ref, qp_ref, kp_ref, vpt_ref, kpm_ref, vptm_ref, o_ref, m_ref, acc_ref, *, tq, tk):
    step = pl.program_id(1)
    i = qi_ref[step]
    j = kj_ref[step]
    nt = (((1,), (1,)), ((), ()))

    def scores(h, kp):
        g = h // ATTN_REP
        return lax.dot_general(kp[:, g * LANES:(g + 1) * LANES], qp_ref[:, h * LANES:(h + 1) * LANES], nt,
                               preferred_element_type=F32)

    def attend_all(kp, vpt, mask):
        s_next = scores(0, kp)
        for h in range(ATTN_HEADS):
            g = h // ATTN_REP
            s = s_next
            if h + 1 < ATTN_HEADS:
                s_next = scores(h + 1, kp)
            if mask is not None:
                s = jnp.where(mask, s, -jnp.inf)
            m_prev = m_ref[h, 0:1, :]
            m_new = jnp.maximum(m_prev, jnp.max(s, axis=0, keepdims=True))
            alpha = jnp.exp2(m_prev - m_new)
            p = jnp.exp2(s - m_new).astype(BF16)
            m_ref[h, 0:1, :] = m_new
            pv = jnp.dot(vpt[g * LANES:(g + 1) * LANES, :], p, preferred_element_type=F32)
            acc_ref[h] = alpha * acc_ref[h] + pv

    @pl.when(j == 0)
    def _():
        m_ref[...] = jnp.full(m_ref.shape, -jnp.inf, F32)
        acc_ref[...] = jnp.zeros(acc_ref.shape, F32)
        attend_all(kpm_ref[...], vptm_ref[...], None)

    @pl.when(j < i)
    def _():
        attend_all(kp_ref[...], vpt_ref[...], None)

    @pl.when(j == i)
    def _():
        key = lax.broadcasted_iota(jnp.int32, (tk, tq), 0)
        qry = lax.broadcasted_iota(jnp.int32, (tk, tq), 1)
        attend_all(kp_ref[...], vpt_ref[...], key <= qry)
        for h in range(ATTN_HEADS):
            acc = acc_ref[h]
            o_t = acc[0:HEAD_DIM, :] * (1.0 / acc[V_ONES_ROW:V_ONES_ROW + 1, :])
            o_ref[:, h * HEAD_DIM:(h + 1) * HEAD_DIM] = o_t.T.astype(o_ref.dtype)


def _fox_prompt(qp, kp, vpt, kp_meta, vpt_meta, n_seq, seq_len, tq):
    nq = seq_len // tq
    assert seq_len % tq == 0
    qi = np.array([i for i in range(nq) for _ in range(i + 1)], np.int32)
    kj = np.array([j for i in range(nq) for j in range(i + 1)], np.int32)
    n_meta = kp_meta.shape[0]
    qw = ATTN_HEADS * LANES
    kw = ATTN_KV_HEADS * LANES
    grid_spec = pltpu.PrefetchScalarGridSpec(
        num_scalar_prefetch=2,
        grid=(n_seq, len(qi)),
        in_specs=[
            pl.BlockSpec((tq, qw), lambda b, s, qi, kj: (b * nq + qi[s], 0)),
            pl.BlockSpec((tq, kw), lambda b, s, qi, kj: (b * nq + kj[s], 0)),
            pl.BlockSpec((kw, tq), lambda b, s, qi, kj: (0, b * nq + kj[s])),
            pl.BlockSpec((n_meta, kw), lambda b, s, qi, kj: (0, 0)),
            pl.BlockSpec((kw, n_meta), lambda b, s, qi, kj: (0, 0)),
        ],
        out_specs=pl.BlockSpec((tq, ATTN_DIM), lambda b, s, qi, kj: (b * nq + qi[s], 0)),
        scratch_shapes=[pltpu.VMEM((ATTN_HEADS, 8, tq), F32),
                        pltpu.VMEM((ATTN_HEADS, LANES, tq), F32)],
    )
    return pl.pallas_call(
        functools.partial(_fox_prompt_body, tq=tq, tk=tq),
        out_shape=jax.ShapeDtypeStruct((n_seq * seq_len, ATTN_DIM), BF16),
        grid_spec=grid_spec,
        compiler_params=_cparams(("parallel", "arbitrary"), 48),
        name="fox_prompt",
    )(jnp.asarray(qi), jnp.asarray(kj), qp, kp, vpt, kp_meta, vpt_meta)


def _fox_sample_body(pt_ref, qaug_ref, kn_ref, vn_ref, dtf_ref, bf_ref, et_ref, suf_ref, *rest, npp, n_steps, s_new):
    k_refs = rest[0:npp]
    v_refs = rest[npp:2 * npp]
    lf_refs = rest[2 * npp:3 * npp]
    o_ref, lfo_ref, m_ref, l_ref, acc_ref, carry_ref, cnrow_ref = rest[3 * npp:]
    step = pl.program_id(1)
    rows = ATTN_HEADS * s_new
    qaug = qaug_ref[...]
    nt = (((1,), (1,)), ((), ()))

    @pl.when(step == 0)
    def _():
        lane = lax.broadcasted_iota(jnp.int32, (s_new, LANES), 1)
        lfn = _log_sigmoid(dtf_ref[...] + bf_ref[...])
        lfo_ref[...] = jnp.where((lane >= SSD_HEADS) & (lane < SSD_HEADS + ATTN_HEADS), lfn, 0.0)
        lf16 = lfn[:, SSD_HEADS:SSD_HEADS + ATTN_HEADS]
        tri = (lax.broadcasted_iota(jnp.int32, (s_new, s_new), 1)
               <= lax.broadcasted_iota(jnp.int32, (s_new, s_new), 0)).astype(F32)
        cn = jnp.dot(tri, lf16, precision=HI, preferred_element_type=F32) * LOG2E
        cne = lax.dot_general(et_ref[...], cn, nt, precision=HI, preferred_element_type=F32)
        trow = lax.broadcasted_iota(jnp.int32, (rows, s_new), 0) % s_new
        tcol = lax.broadcasted_iota(jnp.int32, (rows, s_new), 1)
        cn_row = jnp.sum(jnp.where(tcol == trow, cne, 0.0), axis=-1, keepdims=True)
        cnrow_ref[...] = jnp.broadcast_to(cn_row, cnrow_ref.shape)
        ss = lax.dot_general(qaug[:, 0:KV_DIM], kn_ref[...].astype(BF16), nt, preferred_element_type=F32)
        ss = jnp.where(tcol <= trow, ss + cn_row - cne, -jnp.inf)
        m = jnp.max(ss, axis=-1, keepdims=True)
        p = jnp.exp2(ss - m)
        m_ref[...] = jnp.broadcast_to(m, m_ref.shape)
        l_ref[...] = jnp.broadcast_to(jnp.sum(p, axis=-1, keepdims=True), l_ref.shape)
        acc_ref[...] = jnp.dot(p.astype(BF16), vn_ref[...].astype(BF16), preferred_element_type=F32)
        carry_ref[...] = jnp.zeros(carry_ref.shape, F32)

    lf_all = jnp.concatenate([lf_refs[i][...] for i in range(npp)], axis=0)
    in_page = jnp.dot(jnp.concatenate(_split3(lf_all), axis=1), suf_ref[...], preferred_element_type=F32)
    page_total = jnp.sum(lf_all, axis=1, keepdims=True)
    carry = carry_ref[:, 0:1]
    scores = [None] * npp
    for i in reversed(range(npp)):
        hs = slice(i * ATTN_HEADS, (i + 1) * ATTN_HEADS)
        r_hi, r_mid, r_lo = _split3((in_page[hs] + carry) * LOG2E)
        carry = carry + page_total[hs]
        kt = k_refs[i][...].reshape(KV_DIM, PAGE_SIZE).astype(BF16)
        k_aug = jnp.concatenate([kt, r_hi, r_mid, r_lo], axis=0)
        scores[i] = jnp.dot(qaug, k_aug, preferred_element_type=F32)
    carry_ref[...] = jnp.broadcast_to(carry, carry_ref.shape)
    s_all = jnp.concatenate(scores, axis=1) + cnrow_ref[:, 0:1]
    vt_all = jnp.concatenate([v_refs[i][...].reshape(KV_DIM, PAGE_SIZE).astype(BF16) for i in range(npp)], axis=1)
    m_prev = m_ref[...]
    m_new = jnp.maximum(m_prev, jnp.max(s_all, axis=-1, keepdims=True))
    alpha = jnp.exp2(m_prev - m_new)
    p = jnp.exp2(s_all - m_new[:, 0:1])
    l_ref[...] = alpha * l_ref[...] + jnp.sum(p, axis=-1, keepdims=True)
    m_ref[...] = m_new
    pv = lax.dot_general(p.astype(BF16), vt_all, nt, preferred_element_type=F32)
    acc_ref[...] = jnp.concatenate([alpha, alpha], axis=1) * acc_ref[...] + pv

    @pl.when(step == n_steps - 1)
    def _():
        inv = 1.0 / l_ref[...]
        o_ref[...] = acc_ref[...] * jnp.concatenate([inv, inv], axis=1)


SAMPLE_PAGES_PER_STEP = 16


def _fox_sample(u_s, cache_k, cache_v, cache_logf, page_table, b_forget, n_seq, s_new, npp):
    n_pages = page_table.shape[1]
    assert n_pages % npp == 0 and ATTN_HEADS * s_new == LANES
    n_steps = n_pages // npp
    n_pool = cache_k.shape[0]
    rows = ATTN_HEADS * s_new
    ck = jnp.transpose(cache_k, (0, 2, 3, 1))
    cv = jnp.transpose(cache_v, (0, 2, 3, 1))
    clf = jnp.transpose(cache_logf, (0, 2, 1))
    q = u_s[:, U_Q:U_Q + ATTN_DIM].reshape(n_seq, s_new, ATTN_KV_HEADS, ATTN_REP, HEAD_DIM)
    qbd = jnp.einsum("btgrd,gh->bgrthd", q, jnp.eye(ATTN_KV_HEADS, dtype=F32)).reshape(n_seq, rows, KV_DIM)
    et = np.zeros((rows, ATTN_HEADS), np.float32)
    et[np.arange(rows), np.arange(rows) // s_new] = 1.0
    et_b = jnp.broadcast_to(jnp.asarray(et), (n_seq, rows, ATTN_HEADS))
    qaug = jnp.concatenate([qbd * (ATTN_SCALE * LOG2E)] + [et_b] * N_SPLIT, axis=-1).astype(BF16)
    aug_w = KV_DIM + N_SPLIT * ATTN_HEADS
    suf = np.tile(np.triu(np.ones((PAGE_SIZE, PAGE_SIZE), np.float32), 1).T, (N_SPLIT, 1))
    bf = jnp.zeros((1, LANES), F32).at[0, SSD_HEADS:SSD_HEADS + ATTN_HEADS].set(b_forget)

    def page_ix(i):
        return lambda b, s, pt: (pt[b * n_pages + n_pages - (s + 1) * npp + i], 0, 0)

    const2 = lambda b, s, pt: (0, 0)
    in_specs = [
        pl.BlockSpec((None, rows, aug_w), lambda b, s, pt: (b, 0, 0)),
        pl.BlockSpec((s_new, KV_DIM), lambda b, s, pt: (b, U_K // KV_DIM)),
        pl.BlockSpec((s_new, KV_DIM), lambda b, s, pt: (b, U_V // KV_DIM)),
        pl.BlockSpec((s_new, LANES), lambda b, s, pt: (b, U_DTF // LANES)),
        pl.BlockSpec((1, LANES), const2),
        pl.BlockSpec((rows, ATTN_HEADS), const2),
        pl.BlockSpec((N_SPLIT * PAGE_SIZE, PAGE_SIZE), const2),
    ]
    page_ix4 = lambda i: (lambda b, s, pt: page_ix(i)(b, s, pt) + (0,))
    in_specs += [pl.BlockSpec((None, ATTN_KV_HEADS, HEAD_DIM, PAGE_SIZE), page_ix4(i)) for i in range(npp)]
    in_specs += [pl.BlockSpec((None, ATTN_KV_HEADS, HEAD_DIM, PAGE_SIZE), page_ix4(i)) for i in range(npp)]
    in_specs += [pl.BlockSpec((None, ATTN_HEADS, PAGE_SIZE), page_ix(i)) for i in range(npp)]
    grid_spec = pltpu.PrefetchScalarGridSpec(
        num_scalar_prefetch=1,
        grid=(n_seq, n_steps),
        in_specs=in_specs,
        out_specs=(pl.BlockSpec((None, rows, KV_DIM), lambda b, s, pt: (b, 0, 0)),
                   pl.BlockSpec((s_new, LANES), lambda b, s, pt: (b, 0))),
        scratch_shapes=[pltpu.VMEM((rows, LANES), F32), pltpu.VMEM((rows, LANES), F32),
                        pltpu.VMEM((rows, KV_DIM), F32), pltpu.VMEM((ATTN_HEADS, LANES), F32),
                        pltpu.VMEM((rows, LANES), F32)],
    )
    o_raw, lf_slab = pl.pallas_call(
        functools.partial(_fox_sample_body, npp=npp, n_steps=n_steps, s_new=s_new),
        out_shape=(jax.ShapeDtypeStruct((n_seq, rows, KV_DIM), F32),
                   jax.ShapeDtypeStruct((n_seq * s_new, LANES), F32)),
        grid_spec=grid_spec,
        compiler_params=_cparams(("parallel", "arbitrary"), 48),
        name="fox_sample",
    )(page_table.reshape(-1), qaug, u_s, u_s, u_s, bf, jnp.asarray(et), jnp.asarray(suf, BF16),
      *([ck] * npp), *([cv] * npp), *([clf] * npp))
    o = o_raw.reshape(n_seq, ATTN_KV_HEADS, ATTN_REP, s_new, ATTN_KV_HEADS, HEAD_DIM)
    o = jnp.einsum("bgrtgd->btgrd", o).reshape(n_seq * s_new, ATTN_DIM)
    return o, lf_slab


ROUTE_E0, ROUTE_E1, ROUTE_W0, ROUTE_W1 = 0, 1, 2, 3
ROUTER_EXPERT_LANE0 = N_EXPERT_GROUPS
_BIG_LANE = 4 * LANES


def _sigmoid(x):
    return 1.0 / (1.0 + jnp.exp(-x))


def _mix_body(y_ref, o_ref, gs_ref, ga_ref, h_ref, wssd_ref, wattn_ref, wout_ref, g2_ref, wr_ref, br_ref,
              h1_ref, hn_ref, route_ref):
    ys = jnp.dot(y_ref[...].astype(BF16), wssd_ref[...], preferred_element_type=F32)
    oa = jnp.dot(o_ref[...].astype(BF16), wattn_ref[...], preferred_element_type=F32)
    mix = _sigmoid(gs_ref[...]) * ys + _sigmoid(ga_ref[...]) * oa
    h1 = h_ref[...] + jnp.dot(mix.astype(BF16), wout_ref[...], preferred_element_type=F32)
    h1_ref[...] = h1
    ms = jnp.mean(h1 * h1, axis=-1, keepdims=True)
    hn = (h1 * lax.rsqrt(ms + RMS_EPS)) * g2_ref[...]
    hn_ref[...] = hn
    hn_hi = hn.astype(BF16)
    hn_lo = (hn - hn_hi.astype(F32)).astype(BF16)
    logits = jnp.dot(jnp.concatenate([hn_hi, hn_hi, hn_lo], axis=1), wr_ref[...],
                     preferred_element_type=F32) + br_ref[...]
    lane = lax.broadcasted_iota(jnp.int32, logits.shape, 1)
    gl = jnp.where(lane < N_EXPERT_GROUPS, logits, -jnp.inf)
    gmax = jnp.max(gl, axis=-1, keepdims=True)
    gsel = jnp.min(jnp.where(gl == gmax, lane, _BIG_LANE), axis=-1, keepdims=True)
    wgrp = 1.0 / jnp.sum(jnp.exp(gl - gmax), axis=-1, keepdims=True)
    elane = lane - ROUTER_EXPERT_LANE0
    in_group = (elane >= gsel * EXPERTS_PER_GROUP) & (elane < (gsel + 1) * EXPERTS_PER_GROUP)
    el = jnp.where(in_group, logits, -jnp.inf)
    t1 = jnp.max(el, axis=-1, keepdims=True)
    i1 = jnp.min(jnp.where(el == t1, lane, _BIG_LANE), axis=-1, keepdims=True)
    el2 = jnp.where(lane == i1, -jnp.inf, el)
    t2 = jnp.max(el2, axis=-1, keepdims=True)
    i2 = jnp.min(jnp.where(el2 == t2, lane, _BIG_LANE), axis=-1, keepdims=True)
    e21 = jnp.exp(t2 - t1)
    w1 = wgrp / (1.0 + e21)
    w2 = w1 * e21
    route = jnp.where(lane == ROUTE_E0, (i1 - ROUTER_EXPERT_LANE0).astype(F32),
                      jnp.where(lane == ROUTE_E1, (i2 - ROUTER_EXPERT_LANE0).astype(F32),
                                jnp.where(lane == ROUTE_W0, w1, jnp.where(lane == ROUTE_W1, w2, 0.0))))
    route_ref[...] = route


def _mix(y, o, u, h, p, tm):
    t = h.shape[0]
    assert t % tm == 0
    row = lambda i: (i, 0)
    const = lambda i: (0, 0)
    return pl.pallas_call(
        _mix_body,
        out_shape=(jax.ShapeDtypeStruct((t, D_MODEL), F32), jax.ShapeDtypeStruct((t, D_MODEL), F32),
                   jax.ShapeDtypeStruct((t, LANES), F32)),
        grid=(t // tm,),
        in_specs=[
            pl.BlockSpec((tm, D_INNER), row),
            pl.BlockSpec((tm, ATTN_DIM), row),
            pl.BlockSpec((tm, D_MODEL), lambda i: (i, U_GS // D_MODEL)),
            pl.BlockSpec((tm, D_MODEL), lambda i: (i, U_GA // D_MODEL)),
            pl.BlockSpec((tm, D_MODEL), row),
            pl.BlockSpec((D_INNER, D_MODEL), const),
            pl.BlockSpec((ATTN_DIM, D_MODEL), const),
            pl.BlockSpec((D_MODEL, D_MODEL), const),
            pl.BlockSpec((1, D_MODEL), const),
            pl.BlockSpec((3 * D_MODEL, LANES), const),
            pl.BlockSpec((1, LANES), const),
        ],
        out_specs=(pl.BlockSpec((tm, D_MODEL), row), pl.BlockSpec((tm, D_MODEL), row),
                   pl.BlockSpec((tm, LANES), row)),
        compiler_params=_cparams(("parallel",), 56),
        name="mix_route",
    )(y, o, u, u, h, p["w_ssd_br"], p["w_attn_br"], p["w_out"], p["norm2_g"], p["w_router"], p["b_router"])


def _mix_params(w_ssd_br, w_attn_br, w_out, norm2_g, w_rg, b_rg, w_re, b_re):
    n_r = N_EXPERT_GROUPS + N_EXPERTS
    w_router = jnp.concatenate([w_rg, w_re, jnp.zeros((D_MODEL, LANES - n_r), F32)], axis=1)
    w_hi = w_router.astype(BF16)
    w_lo = (w_router - w_hi.astype(F32)).astype(BF16)
    w_router = jnp.concatenate([w_hi, w_lo, w_hi], axis=0)
    b_router = jnp.concatenate([b_rg, b_re, jnp.zeros((LANES - n_r,), F32)]).reshape(1, LANES)
    return dict(w_ssd_br=w_ssd_br.astype(BF16), w_attn_br=w_attn_br.astype(BF16), w_out=w_out.astype(BF16),
                norm2_g=norm2_g.reshape(1, D_MODEL), w_router=w_router, b_router=b_router)


def _moe_rank_body(route_ref, pos_ref, cnt_ref, carry_ref, *, tm):
    @pl.when(pl.program_id(0) == 0)
    def _():
        carry_ref[...] = jnp.zeros(carry_ref.shape, F32)

    route = route_ref[...]
    lane = lax.broadcasted_iota(jnp.int32, (tm, LANES), 1).astype(F32)
    hit0 = lane == route[:, ROUTE_E0:ROUTE_E0 + 1]
    hit1 = lane == route[:, ROUTE_E1:ROUTE_E1 + 1]
    onehot = hit0.astype(F32) + hit1.astype(F32)
    before = (lax.broadcasted_iota(jnp.int32, (tm, tm), 1) < lax.broadcasted_iota(jnp.int32, (tm, tm), 0))
    c = jnp.dot(before.astype(BF16), onehot.astype(BF16), preferred_element_type=F32) + carry_ref[...]
    pos0 = jnp.sum(jnp.where(hit0, c, 0.0), axis=-1, keepdims=True)
    pos1 = jnp.sum(jnp.where(hit1, c, 0.0), axis=-1, keepdims=True)
    pos_ref[...] = jnp.where(lane == 0.0, pos0, jnp.where(lane == 1.0, pos1, 0.0))
    total = carry_ref[...] + jnp.sum(onehot, axis=0, keepdims=True)
    carry_ref[...] = total
    cnt_ref[...] = total


def _moe_rank(route, tm):
    t = route.shape[0]
    assert t % tm == 0
    return pl.pallas_call(
        functools.partial(_moe_rank_body, tm=tm),
        out_shape=(jax.ShapeDtypeStruct((t, LANES), F32), jax.ShapeDtypeStruct((1, LANES), F32)),
        grid=(t // tm,),
        in_specs=[pl.BlockSpec((tm, LANES), lambda i: (i, 0))],
        out_specs=(pl.BlockSpec((tm, LANES), lambda i: (i, 0)), pl.BlockSpec((1, LANES), lambda i: (0, 0))),
        scratch_shapes=[pltpu.VMEM((1, LANES), F32)],
        compiler_params=_cparams(("arbitrary",)),
        name="moe_rank",
    )(route)


def _experts_body(be_ref, nu_ref, x_ref, wg_ref, wu_ref, wd_ref, y_ref, wg16_ref, wu16_ref, wd16_ref):
    i = pl.program_id(0)
    prev = be_ref[jnp.maximum(i - 1, 0)]

    @pl.when((i == 0) | (be_ref[i] != prev))
    def _():
        wg16_ref[...] = wg_ref[...].astype(BF16)
        wu16_ref[...] = wu_ref[...].astype(BF16)
        wd16_ref[...] = wd_ref[...].astype(BF16)

    @pl.when(i < nu_ref[0])
    def _():
        x16 = x_ref[...].astype(BF16)
        hb = _silu(jnp.dot(x16, wg16_ref[...], preferred_element_type=F32)) * jnp.dot(
            x16, wu16_ref[...], preferred_element_type=F32)
        y_ref[...] = jnp.dot(hb.astype(BF16), wd16_ref[...], preferred_element_type=F32)

    @pl.when(i >= nu_ref[0])
    def _():
        y_ref[...] = jnp.zeros(y_ref.shape, F32)


def _experts(xs, block_e, n_used, w_gate, w_up, w_down, blk):
    rows = xs.shape[0]
    n_blocks = rows // blk
    grid_spec = pltpu.PrefetchScalarGridSpec(
        num_scalar_prefetch=2,
        grid=(n_blocks,),
        in_specs=[
            pl.BlockSpec((blk, D_MODEL), lambda i, be, nu: (jnp.minimum(i, nu[0] - 1), 0)),
            pl.BlockSpec((None, D_MODEL, D_EXPERT), lambda i, be, nu: (be[i], 0, 0)),
            pl.BlockSpec((None, D_MODEL, D_EXPERT), lambda i, be, nu: (be[i], 0, 0)),
            pl.BlockSpec((None, D_EXPERT, D_MODEL), lambda i, be, nu: (be[i], 0, 0)),
        ],
        out_specs=pl.BlockSpec((blk, D_MODEL), lambda i, be, nu: (i, 0)),
        scratch_shapes=[pltpu.VMEM((D_MODEL, D_EXPERT), BF16), pltpu.VMEM((D_MODEL, D_EXPERT), BF16),
                        pltpu.VMEM((D_EXPERT, D_MODEL), BF16)],
    )
    return pl.pallas_call(
        _experts_body,
        out_shape=jax.ShapeDtypeStruct((rows, D_MODEL), F32),
        grid_spec=grid_spec,
        compiler_params=_cparams(("arbitrary",), 48),
        name="moe_experts",
    )(block_e, n_used, xs, w_gate, w_up, w_down)


MOE_ROWS_PER_BLOCK = 256
MOE_TOKEN_TILE = 256
DMA_ISSUE_UNROLL = 8


def _row_copy(src, src_row, dst, dst_row, sem):
    return pltpu.make_async_copy(src.at[pl.ds(src_row, 1)], dst.at[pl.ds(dst_row, 1)], sem)


def _dispatch_body(zb_ref, nu_ref, d0_ref, d1_ref, hn_ref, xs_ref, zero_ref, sem, zsem, *, tm, blk, n_blocks):
    @pl.when(pl.program_id(0) == 0)
    def _():
        zero_ref[...] = jnp.zeros(zero_ref.shape, F32)

        def zero_block(b):
            return pltpu.make_async_copy(zero_ref, xs_ref.at[pl.ds(b * blk, blk)], zsem)

        for e in range(N_EXPERTS):
            zero_block(zb_ref[e]).start()

        def start_tail(b, c):
            zero_block(b).start()
            return c

        def wait_tail(b, c):
            zero_block(b).wait()
            return c

        lax.fori_loop(nu_ref[0], n_blocks, start_tail, 0)
        for e in range(N_EXPERTS):
            zero_block(0).wait()
        lax.fori_loop(nu_ref[0], n_blocks, wait_tail, 0)

    def issue(r, c):
        _row_copy(hn_ref, r, xs_ref, d0_ref[r], sem).start()
        _row_copy(hn_ref, r, xs_ref, d1_ref[r], sem).start()
        return c

    lax.fori_loop(0, tm, issue, 0, unroll=DMA_ISSUE_UNROLL)

    def drain(r, c):
        _row_copy(hn_ref, 0, xs_ref, 0, sem).wait()
        _row_copy(hn_ref, 0, xs_ref, 0, sem).wait()
        return c

    lax.fori_loop(0, tm, drain, 0, unroll=DMA_ISSUE_UNROLL)


def _dispatch(hn, d0, d1, zero_blocks, n_used, n_blocks, blk, tm):
    t = hn.shape[0]
    assert t % tm == 0
    smem_tile = lambda: pl.BlockSpec((tm,), lambda i, zb, nu: (i,), memory_space=pltpu.SMEM)
    grid_spec = pltpu.PrefetchScalarGridSpec(
        num_scalar_prefetch=2,
        grid=(t // tm,),
        in_specs=[smem_tile(), smem_tile(), pl.BlockSpec((tm, D_MODEL), lambda i, zb, nu: (i, 0))],
        out_specs=pl.BlockSpec(memory_space=pl.ANY),
        scratch_shapes=[pltpu.VMEM((blk, D_MODEL), F32), pltpu.SemaphoreType.DMA, pltpu.SemaphoreType.DMA],
    )
    return pl.pallas_call(
        functools.partial(_dispatch_body, tm=tm, blk=blk, n_blocks=n_blocks),
        out_shape=jax.ShapeDtypeStruct((n_blocks * blk, D_MODEL), F32),
        grid_spec=grid_spec,
        compiler_params=_cparams(("arbitrary",)),
        name="moe_dispatch",
    )(zero_blocks, n_used, d0, d1, hn)


def _combine_body(d0_ref, d1_ref, d0n_ref, d1n_ref, h1_ref, route_ref, g_ref, ys_ref, o_ref, buf, sem, *, tm, n_tiles):
    i = pl.program_id(0)
    slot = i % 2

    def gather(da_ref, db_ref, s):
        def issue(r, c):
            _row_copy(ys_ref, da_ref[r], buf.at[s, 0], r, sem.at[s]).start()
            _row_copy(ys_ref, db_ref[r], buf.at[s, 1], r, sem.at[s]).start()
            return c
        lax.fori_loop(0, tm, issue, 0, unroll=DMA_ISSUE_UNROLL)

    @pl.when(i == 0)
    def _():
        gather(d0_ref, d1_ref, 0)

    @pl.when(i + 1 < n_tiles)
    def _():
        gather(d0n_ref, d1n_ref, 1 - slot)

    def drain(r, c):
        _row_copy(ys_ref, 0, buf.at[slot, 0], 0, sem.at[slot]).wait()
        _row_copy(ys_ref, 0, buf.at[slot, 1], 0, sem.at[slot]).wait()
        return c

    lax.fori_loop(0, tm, drain, 0, unroll=DMA_ISSUE_UNROLL)
    route = route_ref[...]
    h = (h1_ref[...] + route[:, ROUTE_W0:ROUTE_W0 + 1] * buf[slot, 0]
         + route[:, ROUTE_W1:ROUTE_W1 + 1] * buf[slot, 1])
    ms = jnp.mean(h * h, axis=-1, keepdims=True)
    o_ref[...] = (h * lax.rsqrt(ms + RMS_EPS)) * g_ref[...]


def _combine(h1, ys, d0, d1, route, g, tm):
    t = h1.shape[0]
    n_tiles = t // tm
    row = lambda i: (i, 0)
    cur = lambda: pl.BlockSpec((tm,), lambda i: (i,), memory_space=pltpu.SMEM)
    nxt = lambda: pl.BlockSpec((tm,), lambda i: (jnp.minimum(i + 1, n_tiles - 1),), memory_space=pltpu.SMEM)
    return pl.pallas_call(
        functools.partial(_combine_body, tm=tm, n_tiles=n_tiles),
        out_shape=jax.ShapeDtypeStruct((t, D_MODEL), F32),
        grid=(n_tiles,),
        in_specs=[cur(), cur(), nxt(), nxt(), pl.BlockSpec((tm, D_MODEL), row), pl.BlockSpec((tm, LANES), row),
                  pl.BlockSpec((1, D_MODEL), lambda i: (0, 0)), pl.BlockSpec(memory_space=pl.ANY)],
        out_specs=pl.BlockSpec((tm, D_MODEL), row),
        scratch_shapes=[pltpu.VMEM((2, 2, tm, D_MODEL), F32), pltpu.SemaphoreType.DMA((2,))],
        compiler_params=_cparams(("arbitrary",)),
        name="moe_combine_norm",
    )(d0, d1, d0, d1, h1, route, g.reshape(1, D_MODEL), ys)


def _moe(hn, h1, route, w_gate, w_up, w_down, final_g):
    t = hn.shape[0]
    blk = MOE_ROWS_PER_BLOCK
    pos, cnt = _moe_rank(route, MOE_TOKEN_TILE)
    counts = cnt[0, :N_EXPERTS].astype(jnp.int32)
    padded = (counts + blk - 1) // blk * blk
    ends = jnp.cumsum(padded)
    starts = ends - padded
    expert_ids = jnp.arange(N_EXPERTS, dtype=jnp.int32)

    def dest(e_lane, p_lane):
        e = route[:, e_lane].astype(jnp.int32)
        start = jnp.sum(jnp.where(e[:, None] == expert_ids[None, :], starts[None, :], 0), axis=1)
        return start + pos[:, p_lane].astype(jnp.int32)

    d0 = dest(ROUTE_E0, 0)
    d1 = dest(ROUTE_E1, 1)
    n_blocks = (2 * t + N_EXPERTS * (blk - 1) + blk - 1) // blk
    first_row = jnp.arange(n_blocks, dtype=jnp.int32) * blk
    block_e = jnp.minimum(jnp.sum((ends[None, :] <= first_row[:, None]).astype(jnp.int32), axis=1), N_EXPERTS - 1)
    n_used = (ends[-1] // blk).astype(jnp.int32).reshape(1)
    zero_blocks = jnp.clip((ends - 1) // blk, 0, n_blocks - 1).astype(jnp.int32)
    xs = _dispatch(hn, d0, d1, zero_blocks, n_used, n_blocks, blk, MOE_TOKEN_TILE)
    ys = _experts(xs, block_e, n_used, w_gate, w_up, w_down, blk)
    return _combine(h1, ys, d0, d1, route, final_g, MOE_TOKEN_TILE)


def _conv_history(rows):
    n = rows.shape[0]
    return jnp.concatenate([jnp.zeros((n, CONV_PAD - (CONV_W - 1), CONV_DIM), F32), rows], axis=1)


def _ucols(u, start, width):
    return u[:, start:start + width]


def kernel(x_prompt, x_sample, cache_k, cache_v, cache_logf, state_ssm, state_conv, page_table, meta_tokens, norm1_g, w_in, conv_w, conv_b, dt_bias, a_log, d_skip, ssd_norm_g, b_forget, w_ssd_br, w_attn_br, w_out, norm2_g, w_router_group, b_router_group, w_router_expert, b_router_expert, w_exp_gate, w_exp_up, w_exp_down, final_norm_g):
    nb, sl, _ = x_prompt.shape
    sb, ss, _ = x_sample.shape
    assert w_in.shape[0] == 1, "single-layer step"
    ly = 0
    xp = x_prompt.reshape(nb * sl, D_MODEL)
    xs = x_sample.reshape(sb * ss, D_MODEL)

    wp = _pack_w_in(w_in[ly])
    u_p = _inproj(xp, norm1_g[ly], wp)
    u_s = _inproj(xs, norm1_g[ly], wp)
    u_m = _inproj(meta_tokens, norm1_g[ly], wp)

    sp = _ssd_params(conv_w[ly], conv_b[ly], dt_bias[ly], a_log[ly], d_skip[ly], ssd_norm_g[ly])
    zero_hist = jnp.zeros((1, CONV_PAD, CONV_DIM), F32)
    zero_state = jnp.zeros((1, D_INNER, D_STATE), F32)
    _, h_meta = _ssd(u_m, 1, N_META, N_META, zero_hist, zero_state, sp, BF16)
    hist_meta = _conv_history(_ucols(u_m, U_XBC, CONV_DIM)[None, N_META - (CONV_W - 1):])
    y_p, ssm_p = _ssd(u_p, nb, sl, SSD_CHUNK, hist_meta, h_meta, sp, BF16)
    y_s, ssm_s = _ssd(u_s, sb, ss, ss, _conv_history(state_conv[ly]),
                      state_ssm[ly].reshape(sb, D_INNER, D_STATE), sp, F32)

    slab_m, _, kp_m, vpt_m, kt_m, vt_m = _attn_prep(u_m, b_forget[ly], N_META, N_META, rel_to_last=True)
    slab_p, qp_p, kp_p, vpt_p, kt_p, vt_p = _attn_prep(u_p, b_forget[ly], nb * sl, sl)
    o_p = _fox_prompt(qp_p, kp_p, vpt_p, kp_m, vpt_m, nb, sl, ATTN_TQ)
    o_s, slab_s = _fox_sample(u_s, cache_k[ly], cache_v[ly], cache_logf[ly], page_table, b_forget[ly], sb, ss,
                              SAMPLE_PAGES_PER_STEP)

    mp = _mix_params(w_ssd_br[ly], w_attn_br[ly], w_out[ly], norm2_g[ly], w_router_group[ly], b_router_group[ly],
                     w_router_expert[ly], b_router_expert[ly])
    h1_p, hn_p, route_p = _mix(y_p, o_p, u_p, xp, mp, 256)
    h1_s, hn_s, route_s = _mix(y_s, o_s, u_s, xs, mp, 256)
    out_p = _moe(hn_p, h1_p, route_p, w_exp_gate[ly], w_exp_up[ly], w_exp_down[ly], final_norm_g)
    out_s = _moe(hn_s, h1_s, route_s, w_exp_gate[ly], w_exp_up[ly], w_exp_down[ly], final_norm_g)

    def with_meta(meta_rows, rows, width):
        m = jnp.broadcast_to(meta_rows[None], (nb, N_META, width))
        return jnp.concatenate([m, rows.reshape(nb, sl, width)], axis=1)[None]

    def kv_with_meta(t_meta, t_rows):
        m = jnp.broadcast_to(t_meta.reshape(ATTN_KV_HEADS, HEAD_DIM, 1, N_META), (ATTN_KV_HEADS, HEAD_DIM, nb, N_META))
        full = jnp.concatenate([m, t_rows.reshape(ATTN_KV_HEADS, HEAD_DIM, nb, sl)], axis=3)
        return jnp.transpose(full, (2, 3, 0, 1))[None]

    lf_m = slab_m[:, SSD_HEADS:SSD_HEADS + ATTN_HEADS]
    lf_p = slab_p[:, SSD_HEADS:SSD_HEADS + ATTN_HEADS]
    lf_s = slab_s[:, SSD_HEADS:SSD_HEADS + ATTN_HEADS]
    kv_shape_s = (1, sb, ss, ATTN_KV_HEADS, HEAD_DIM)
    state_shape = (SSD_HEADS, SSD_HEADDIM, D_STATE)
    tail = CONV_W - 1
    return (
        out_p.reshape(nb, sl, D_MODEL),
        out_s.reshape(sb, ss, D_MODEL),
        kv_with_meta(kt_m, kt_p),
        kv_with_meta(vt_m, vt_p),
        with_meta(lf_m, lf_p, ATTN_HEADS),
        ssm_p.reshape((1, nb) + state_shape),
        u_p.reshape(nb, sl, U_COLS)[None, :, sl - tail:, U_XBC:U_XBC + CONV_DIM],
        _ucols(u_s, U_K, KV_DIM).reshape(kv_shape_s),
        _ucols(u_s, U_V, KV_DIM).reshape(kv_shape_s),
        lf_s.reshape(1, sb, ss, ATTN_HEADS),
        ssm_s.reshape((1, sb) + state_shape),
        u_s.reshape(sb, ss, U_COLS)[None, :, ss - tail:, U_XBC:U_XBC + CONV_DIM],
    )
```

```python
import functools

import numpy as np
import jax
import jax.numpy as jnp
from jax import lax
from jax.experimental import pallas as pl
from jax.experimental.pallas import tpu as pltpu

F32 = jnp.float32
BF16 = jnp.bfloat16
HI = lax.Precision.HIGHEST

D_MODEL = 1024
N_META = 16
RMS_EPS = 1e-6
D_INNER = 2048
SSD_HEADDIM = 64
SSD_HEADS = 32
SSD_GROUPS = 8
HEADS_PER_GROUP = SSD_HEADS // SSD_GROUPS
GROUP_W = HEADS_PER_GROUP * SSD_HEADDIM
D_STATE = 128
CONV_W = 4
CONV_DIM = 4096
SSD_CHUNK = 128
ATTN_HEADS = 16
ATTN_KV_HEADS = 4
HEAD_DIM = 64
ATTN_REP = 4
ATTN_DIM = 1024
KV_DIM = 256
ATTN_SCALE = HEAD_DIM ** -0.5
PAGE_SIZE = 128
N_EXPERT_GROUPS = 4
EXPERTS_PER_GROUP = 8
N_EXPERTS = 32
D_EXPERT = 512
LANES = 128

U_XBC = 0
U_Z = 4096
U_Q = 6144
U_GS = 7168
U_GA = 8192
U_K = 9216
U_V = 9472
U_DTF = 9728
U_COLS = 9856
U_COLS_PADDED = 10240
INPROJ_TN = 1024


def _cparams(sem, vmem_mb=None):
    kw = dict(dimension_semantics=sem)
    if vmem_mb is not None:
        kw["vmem_limit_bytes"] = vmem_mb * 1024 * 1024
    return pltpu.CompilerParams(**kw)


def _inproj_body(x_ref, g_ref, w_ref, o_ref, xn_ref):
    @pl.when(pl.program_id(1) == 0)
    def _():
        x = x_ref[...]
        ms = jnp.mean(x * x, axis=-1, keepdims=True)
        xn_ref[...] = ((x * lax.rsqrt(ms + RMS_EPS)) * g_ref[...]).astype(BF16)

    o_ref[...] = jnp.dot(xn_ref[...], w_ref[...], preferred_element_type=F32)


def _inproj(x, g, wp):
    t = x.shape[0]
    tm = min(t, 1024)
    assert t % tm == 0
    return pl.pallas_call(
        _inproj_body,
        out_shape=jax.ShapeDtypeStruct((t, U_COLS), F32),
        grid=(t // tm, U_COLS_PADDED // INPROJ_TN),
        in_specs=[
            pl.BlockSpec((tm, D_MODEL), lambda i, j: (i, 0)),
            pl.BlockSpec((1, D_MODEL), lambda i, j: (0, 0)),
            pl.BlockSpec((D_MODEL, INPROJ_TN), lambda i, j: (0, j)),
        ],
        out_specs=pl.BlockSpec((tm, INPROJ_TN), lambda i, j: (i, j)),
        scratch_shapes=[pltpu.VMEM((tm, D_MODEL), BF16)],
        compiler_params=_cparams(("parallel", "arbitrary"), 48),
        name="inproj",
    )(x, g.reshape(1, D_MODEL), wp)


def _pack_w_in(w_in):
    o = np.cumsum([0, D_INNER, CONV_DIM, SSD_HEADS, ATTN_DIM, KV_DIM, KV_DIM, ATTN_HEADS, D_MODEL, D_MODEL])
    z, xbc, dt, q, k, v, f, gs, ga = [w_in[:, o[i]:o[i + 1]] for i in range(9)]
    pad_dtf = jnp.zeros((D_MODEL, LANES - SSD_HEADS - ATTN_HEADS), w_in.dtype)
    pad = jnp.zeros((D_MODEL, U_COLS_PADDED - U_COLS), w_in.dtype)
    return jnp.concatenate([xbc, z, q, gs, ga, k, v, dt, f, pad_dtf, pad], axis=1).astype(BF16)


def _log_sigmoid(x):
    return jnp.minimum(x, 0.0) - jnp.log1p(jnp.exp(-jnp.abs(x)))


LOG2E = 1.4426950408889634
QK_ONES_LANE = HEAD_DIM
QK_CK_LANE = HEAD_DIM + 3
V_ONES_ROW = HEAD_DIM
N_SPLIT = 3
ATTN_TQ = 256
QK_AHEAD = 4


def _placement_matrices():
    eq = np.zeros((LANES, ATTN_HEADS * LANES), np.float32)
    ek = np.zeros((LANES, ATTN_KV_HEADS * LANES), np.float32)
    for h in range(ATTN_HEADS):
        g, r = divmod(h, ATTN_REP)
        for s in range(N_SPLIT):
            eq[s * ATTN_HEADS + h, h * LANES + QK_ONES_LANE + s] = 1.0
            ek[s * ATTN_HEADS + h, g * LANES + QK_CK_LANE + s * ATTN_REP + r] = 1.0
    cq = np.zeros((1, ATTN_HEADS * LANES), np.float32)
    for h in range(ATTN_HEADS):
        r = h % ATTN_REP
        for s in range(N_SPLIT):
            cq[0, h * LANES + QK_CK_LANE + s * ATTN_REP + r] = -1.0
    return jnp.asarray(eq, BF16), jnp.asarray(ek, BF16), jnp.asarray(cq)


def _transpose_rows(x, q):
    if q < LANES:
        x = jnp.concatenate([x, jnp.zeros((LANES - q, x.shape[1]), x.dtype)], axis=0)
    return x.T[:, :q]


def _split3(x):
    hi = x.astype(BF16)
    r1 = x - hi.astype(F32)
    mid = r1.astype(BF16)
    lo = (r1 - mid.astype(F32)).astype(BF16)
    return hi, mid, lo


def _split3_stacked(x, axis):
    return jnp.concatenate([t.astype(F32) for t in _split3(x)], axis=axis).astype(BF16)


def _attn_prep_body(q_ref, k_ref, v_ref, dtf_ref, bf_ref, eq_ref, ek_ref, cq_ref, slab_ref, qp_ref, kp_ref, vpt_ref,
                    kt_ref, vt_ref, carry_ref, *, tiles_per_seq, tl, rel_to_last):
    i = pl.program_id(0)

    @pl.when(i % tiles_per_seq == 0)
    def _():
        carry_ref[...] = jnp.zeros_like(carry_ref)

    lf = _log_sigmoid(dtf_ref[...] + bf_ref[...])
    row = lax.broadcasted_iota(jnp.int32, (tl, tl), 0)
    col = lax.broadcasted_iota(jnp.int32, (tl, tl), 1)
    c = jnp.dot((col <= row).astype(F32), lf, precision=HI, preferred_element_type=F32) + carry_ref[...]
    carry_ref[...] = c[tl - 1:tl, :]
    lane = lax.broadcasted_iota(jnp.int32, (tl, LANES), 1)
    c16 = jnp.where(lane < ATTN_HEADS, pltpu.roll(c, LANES - SSD_HEADS, 1), 0.0)
    slab_ref[...] = jnp.where((lane >= SSD_HEADS) & (lane < SSD_HEADS + ATTN_HEADS), lf, c16)
    if rel_to_last:
        c16 = c16 - c16[tl - 1:tl, :]
    hi, mid, lo = _split3(c16 * LOG2E)
    x = jnp.where(lane < ATTN_HEADS, hi.astype(F32),
                  jnp.where(lane < 2 * ATTN_HEADS, pltpu.roll(mid.astype(F32), ATTN_HEADS, 1),
                            pltpu.roll(lo.astype(F32), 2 * ATTN_HEADS, 1))).astype(BF16)
    q_extra = jnp.dot(x, eq_ref[...], preferred_element_type=F32) + cq_ref[...]
    k_extra = jnp.dot(x, ek_ref[...], preferred_element_type=F32)
    low = lane < HEAD_DIM
    for h in range(ATTN_HEADS):
        qx = q_ref[:, (h // 2) * LANES:(h // 2 + 1) * LANES]
        if h % 2:
            qx = pltpu.roll(qx, HEAD_DIM, 1)
        tile = jnp.where(low, qx * (ATTN_SCALE * LOG2E), q_extra[:, h * LANES:(h + 1) * LANES])
        qp_ref[:, h * LANES:(h + 1) * LANES] = tile.astype(BF16)
    ones_k = ((lane >= QK_ONES_LANE) & (lane < QK_CK_LANE)).astype(F32)
    ones_v = (lane == V_ONES_ROW).astype(F32)
    for g in range(ATTN_KV_HEADS):
        kx = k_ref[:, (g // 2) * LANES:(g // 2 + 1) * LANES]
        vx = v_ref[:, (g // 2) * LANES:(g // 2 + 1) * LANES]
        if g % 2:
            kx = pltpu.roll(kx, HEAD_DIM, 1)
            vx = pltpu.roll(vx, HEAD_DIM, 1)
        kp_ref[:, g * LANES:(g + 1) * LANES] = jnp.where(low, kx, k_extra[:, g * LANES:(g + 1) * LANES] + ones_k
                                                         ).astype(BF16)
        vpt_ref[g * LANES:(g + 1) * LANES, :] = _transpose_rows(jnp.where(low, vx, ones_v), tl).astype(BF16)
    kt_ref[...] = _transpose_rows(k_ref[...], tl)
    vt_ref[...] = _transpose_rows(v_ref[...], tl)


def _attn_prep(u, b_forget, n_tokens, seq_len, rel_to_last=False):
    tl = min(n_tokens, 256)
    assert n_tokens % tl == 0 and seq_len % tl == 0
    bf = jnp.zeros((1, LANES), F32).at[0, SSD_HEADS:SSD_HEADS + ATTN_HEADS].set(b_forget)
    eq, ek, cq = _placement_matrices()
    const = lambda i: (0, 0)
    return pl.pallas_call(
        functools.partial(_attn_prep_body, tiles_per_seq=seq_len // tl, tl=tl, rel_to_last=rel_to_last),
        out_shape=(jax.ShapeDtypeStruct((n_tokens, LANES), F32),
                   jax.ShapeDtypeStruct((n_tokens, ATTN_HEADS * LANES), BF16),
                   jax.ShapeDtypeStruct((n_tokens, ATTN_KV_HEADS * LANES), BF16),
                   jax.ShapeDtypeStruct((ATTN_KV_HEADS * LANES, n_tokens), BF16),
                   jax.ShapeDtypeStruct((KV_DIM, n_tokens), F32),
                   jax.ShapeDtypeStruct((KV_DIM, n_tokens), F32)),
        grid=(n_tokens // tl,),
        in_specs=[
            pl.BlockSpec((tl, ATTN_DIM), lambda i: (i, U_Q // ATTN_DIM)),
            pl.BlockSpec((tl, KV_DIM), lambda i: (i, U_K // KV_DIM)),
            pl.BlockSpec((tl, KV_DIM), lambda i: (i, U_V // KV_DIM)),
            pl.BlockSpec((tl, LANES), lambda i: (i, U_DTF // LANES)),
            pl.BlockSpec((1, LANES), const),
            pl.BlockSpec((LANES, ATTN_HEADS * LANES), const),
            pl.BlockSpec((LANES, ATTN_KV_HEADS * LANES), const),
            pl.BlockSpec((1, ATTN_HEADS * LANES), const),
        ],
        out_specs=(pl.BlockSpec((tl, LANES), lambda i: (i, 0)),
                   pl.BlockSpec((tl, ATTN_HEADS * LANES), lambda i: (i, 0)),
                   pl.BlockSpec((tl, ATTN_KV_HEADS * LANES), lambda i: (i, 0)),
                   pl.BlockSpec((ATTN_KV_HEADS * LANES, tl), lambda i: (0, i)),
                   pl.BlockSpec((KV_DIM, tl), lambda i: (0, i)),
                   pl.BlockSpec((KV_DIM, tl), lambda i: (0, i))),
        scratch_shapes=[pltpu.VMEM((1, LANES), F32)],
        compiler_params=_cparams(("arbitrary",), 40),
        name="attn_prep",
    )(u, u, u, u, bf, eq, ek, cq)


CONV_PAD = 8


def _silu(x):
    return x * (1.0 / (1.0 + jnp.exp(-x)))


def _softplus(x):
    return jnp.maximum(x, 0.0) + jnp.log1p(jnp.exp(-jnp.abs(x)))


def _ssd_body(xbc_ref, z_ref, dtf_ref, conv0_ref, h0_ref, cw_ref, cb_ref, dtb_ref, alog_ref, dskip_ref,
              gn_ref, e_ref, y_ref, hout_ref, xconv_ref, ht_ref, *, q, n_chunks):
    c = pl.program_id(1)

    @pl.when(c == 0)
    def _():
        xconv_ref[0:CONV_PAD, :] = conv0_ref[0]
        for g in range(SSD_GROUPS):
            ht_ref[g] = h0_ref[0, g * GROUP_W:(g + 1) * GROUP_W, :].T

    xconv_ref[CONV_PAD:CONV_PAD + q, :] = xbc_ref[...]
    acc = cb_ref[...] + xbc_ref[...] * cw_ref[CONV_W - 1:CONV_W, :]
    staged = xconv_ref[...]
    for k in range(CONV_W - 1):
        off = CONV_PAD - (CONV_W - 1) + k
        acc = acc + pltpu.roll(staged, CONV_PAD + q - off, 0)[0:q, :] * cw_ref[k:k + 1, :]
    xc = _silu(acc)
    xconv_ref[CONV_PAD - (CONV_W - 1):CONV_PAD, :] = xconv_ref[CONV_PAD + q - (CONV_W - 1):CONV_PAD + q, :]

    dt = _softplus(dtf_ref[...] + dtb_ref[...])
    a = -jnp.exp(alog_ref[...])
    row = lax.broadcasted_iota(jnp.int32, (q, q), 0)
    col = lax.broadcasted_iota(jnp.int32, (q, q), 1)
    causal = col <= row
    tri = causal.astype(BF16)
    acum = jnp.dot(jnp.concatenate([tri] * N_SPLIT, axis=1), _split3_stacked(dt * a, 0),
                   preferred_element_type=F32)
    acum_t = _transpose_rows(acum, q)
    a_last = acum[q - 1:q, :]
    fac = jnp.concatenate([jnp.exp(acum), jnp.exp(a_last - acum) * dt, dt], axis=0)
    fac = jnp.dot(_split3_stacked(fac, 1), e_ref[...], preferred_element_type=F32)
    ea_full, wst_full, dt_full = fac[0:q], fac[q:2 * q], fac[2 * q:3 * q]

    def group_bc(g):
        bg = xc[:, D_INNER + g * D_STATE:D_INNER + (g + 1) * D_STATE]
        cg = xc[:, D_INNER + SSD_GROUPS * D_STATE + g * D_STATE:D_INNER + SSD_GROUPS * D_STATE + (g + 1) * D_STATE]
        cg16 = cg.astype(BF16)
        cbm = lax.dot_general(cg16, bg.astype(BF16), (((1,), (1,)), ((), ())), preferred_element_type=F32)
        return bg, cg16, cbm

    nxt = group_bc(0)
    for g in range(SSD_GROUPS):
        gs = slice(g * GROUP_W, (g + 1) * GROUP_W)
        xg = xc[:, gs]
        bg, cg16, cbm = nxt
        if g + 1 < SSD_GROUPS:
            nxt = group_bc(g + 1)
        htg = ht_ref[g]
        yoff = jnp.dot(cg16, htg.astype(BF16), preferred_element_type=F32) * ea_full[:, gs]
        xw = (xg * wst_full[:, gs]).astype(BF16)
        bgt = _transpose_rows(bg, q).astype(BF16)
        st = jnp.dot(bgt, xw, preferred_element_type=F32)
        ht_ref[g] = ea_full[q - 1:q, gs] * htg + st
        xdt = (xg * dt_full[:, gs]).astype(BF16)
        yd = []
        for r in range(HEADS_PER_GROUP):
            h = g * HEADS_PER_GROUP + r
            seg = acum[:, h:h + 1] - acum_t[h:h + 1, :]
            m = cbm * jnp.exp(jnp.where(causal, seg, -jnp.inf))
            yd.append(jnp.dot(m.astype(BF16), xdt[:, r * SSD_HEADDIM:(r + 1) * SSD_HEADDIM],
                              preferred_element_type=F32))
        yd = jnp.concatenate(yd, axis=1)
        yg = yd + yoff + dskip_ref[:, gs] * xg
        yz = yg * _silu(z_ref[:, gs])
        ms = jnp.mean(yz * yz, axis=-1, keepdims=True)
        y_ref[:, gs] = (yz * lax.rsqrt(ms + RMS_EPS) * gn_ref[:, gs]).astype(y_ref.dtype)

    @pl.when(c == n_chunks - 1)
    def _():
        for g in range(SSD_GROUPS):
            hout_ref[0, g * GROUP_W:(g + 1) * GROUP_W, :] = ht_ref[g].T


def _head_expand_matrix():
    e = np.zeros((LANES, D_INNER), np.float32)
    for h in range(SSD_HEADS):
        e[h, h * SSD_HEADDIM:(h + 1) * SSD_HEADDIM] = 1.0
    return jnp.asarray(np.tile(e, (N_SPLIT, 1)), BF16)


def _ssd(u, n_seq, seq_len, q, conv0, h0, p, y_dtype):
    n_chunks = seq_len // q
    assert seq_len % q == 0 and q % 8 == 0
    conv_ix = (lambda b, c: (b, 0, 0)) if conv0.shape[0] == n_seq and n_seq > 1 else (lambda b, c: (0, 0, 0))
    h_ix = (lambda b, c: (b, 0, 0)) if h0.shape[0] == n_seq and n_seq > 1 else (lambda b, c: (0, 0, 0))
    const2 = lambda b, c: (0, 0)
    return pl.pallas_call(
        functools.partial(_ssd_body, q=q, n_chunks=n_chunks),
        out_shape=(jax.ShapeDtypeStruct((n_seq * seq_len, D_INNER), y_dtype),
                   jax.ShapeDtypeStruct((n_seq, D_INNER, D_STATE), F32)),
        grid=(n_seq, n_chunks),
        in_specs=[
            pl.BlockSpec((q, CONV_DIM), lambda b, c: (b * n_chunks + c, U_XBC // CONV_DIM)),
            pl.BlockSpec((q, D_INNER), lambda b, c: (b * n_chunks + c, U_Z // D_INNER)),
            pl.BlockSpec((q, LANES), lambda b, c: (b * n_chunks + c, U_DTF // LANES)),
            pl.BlockSpec((1, CONV_PAD, CONV_DIM), conv_ix),
            pl.BlockSpec((1, D_INNER, D_STATE), h_ix),
            pl.BlockSpec((CONV_W, CONV_DIM), const2),
            pl.BlockSpec((1, CONV_DIM), const2),
            pl.BlockSpec((1, LANES), const2),
            pl.BlockSpec((1, LANES), const2),
            pl.BlockSpec((1, D_INNER), const2),
            pl.BlockSpec((1, D_INNER), const2),
            pl.BlockSpec((N_SPLIT * LANES, D_INNER), const2),
        ],
        out_specs=(pl.BlockSpec((q, D_INNER), lambda b, c: (b * n_chunks + c, 0)),
                   pl.BlockSpec((1, D_INNER, D_STATE), lambda b, c: (b, 0, 0))),
        scratch_shapes=[pltpu.VMEM((CONV_PAD + q, CONV_DIM), F32),
                        pltpu.VMEM((SSD_GROUPS, D_STATE, GROUP_W), F32)],
        compiler_params=_cparams(("parallel", "arbitrary"), 48),
        name="ssd_q%d" % q,
    )(u, u, u, conv0, h0, p["conv_w"], p["conv_b"], p["dt_bias"], p["a_log"], p["d_skip"], p["ssd_norm_g"],
      p["head_expand"])


def _ssd_params(conv_w, conv_b, dt_bias, a_log, d_skip, ssd_norm_g):
    pad32 = lambda v: jnp.zeros((1, LANES), F32).at[0, :SSD_HEADS].set(v)
    return dict(conv_w=conv_w, conv_b=conv_b.reshape(1, CONV_DIM), dt_bias=pad32(dt_bias), a_log=pad32(a_log),
                d_skip=jnp.repeat(d_skip, SSD_HEADDIM).reshape(1, D_INNER),
                ssd_norm_g=ssd_norm_g.reshape(1, D_INNER), head_expand=_head_expand_matrix())


def _fox_prompt_body(qi_ref, kj_ref, qp_ref, kp_ref, vpt_ref, kpm_ref, vptm_ref, o_ref, m_ref, acc_ref, *, tq, tk):
    step = pl.program_id(1)
    i = qi_ref[step]
    j = kj_ref[step]
    nt = (((1,), (1,)), ((), ()))

    def scores(h, kp):
        g = h // ATTN_REP
        return lax.dot_general(kp[:, g * LANES:(g + 1) * LANES], qp_ref[:, h * LANES:(h + 1) * LANES], nt,
                               preferred_element_type=F32)

    def attend_all(kp, vpt, mask, ahead=QK_AHEAD):
        pending = [scores(h, kp) for h in range(ahead)]
        for h in range(ATTN_HEADS):
            g = h // ATTN_REP
            s = pending.pop(0)
            if h + ahead < ATTN_HEADS:
                pending.append(scores(h + ahead, kp))
            if mask is not None:
                s = jnp.where(mask, s, -jnp.inf)
            m_prev = m_ref[h, 0:1, :]
            m_new = jnp.maximum(m_prev, jnp.max(s, axis=0, keepdims=True))
            alpha = jnp.exp2(m_prev - m_new)
            p = jnp.exp2(s - m_new).astype(BF16)
            m_ref[h, 0:1, :] = m_new
            pv = jnp.dot(vpt[g * LANES:(g + 1) * LANES, :], p, preferred_element_type=F32)
            acc_ref[h] = alpha * acc_ref[h] + pv

    @pl.when(j == 0)
    def _():
        m_ref[...] = jnp.full(m_ref.shape, -jnp.inf, F32)
        acc_ref[...] = jnp.zeros(acc_ref.shape, F32)
        attend_all(kpm_ref[...], vptm_ref[...], None, ahead=1)

    @pl.when(j < i)
    def _():
        attend_all(kp_ref[...], vpt_ref[...], None)

    @pl.when(j == i)
    def _():
        key = lax.broadcasted_iota(jnp.int32, (tk, tq), 0)
        qry = lax.broadcasted_iota(jnp.int32, (tk, tq), 1)
        attend_all(kp_ref[...], vpt_ref[...], key <= qry)
        for h in range(ATTN_HEADS):
            acc = acc_ref[h]
            o_t = acc[0:HEAD_DIM, :] * (1.0 / acc[V_ONES_ROW:V_ONES_ROW + 1, :])
            o_ref[:, h * HEAD_DIM:(h + 1) * HEAD_DIM] = o_t.T.astype(o_ref.dtype)


def _fox_prompt(qp, kp, vpt, kp_meta, vpt_meta, n_seq, seq_len, tq):
    nq = seq_len // tq
    assert seq_len % tq == 0
    qi = np.array([i for i in range(nq) for _ in range(i + 1)], np.int32)
    kj = np.array([j for i in range(nq) for j in range(i + 1)], np.int32)
    n_meta = kp_meta.shape[0]
    qw = ATTN_HEADS * LANES
    kw = ATTN_KV_HEADS * LANES
    grid_spec = pltpu.PrefetchScalarGridSpec(
        num_scalar_prefetch=2,
        grid=(n_seq, len(qi)),
        in_specs=[
            pl.BlockSpec((tq, qw), lambda b, s, qi, kj: (b * nq + qi[s], 0)),
            pl.BlockSpec((tq, kw), lambda b, s, qi, kj: (b * nq + kj[s], 0)),
            pl.BlockSpec((kw, tq), lambda b, s, qi, kj: (0, b * nq + kj[s])),
            pl.BlockSpec((n_meta, kw), lambda b, s, qi, kj: (0, 0)),
            pl.BlockSpec((kw, n_meta), lambda b, s, qi, kj: (0, 0)),
        ],
        out_specs=pl.BlockSpec((tq, ATTN_DIM), lambda b, s, qi, kj: (b * nq + qi[s], 0)),
        scratch_shapes=[pltpu.VMEM((ATTN_HEADS, 8, tq), F32),
                        pltpu.VMEM((ATTN_HEADS, LANES, tq), F32)],
    )
    return pl.pallas_call(
        functools.partial(_fox_prompt_body, tq=tq, tk=tq),
        out_shape=jax.ShapeDtypeStruct((n_seq * seq_len, ATTN_DIM), BF16),
        grid_spec=grid_spec,
        compiler_params=_cparams(("parallel", "arbitrary"), 48),
        name="fox_prompt",
    )(jnp.asarray(qi), jnp.asarray(kj), qp, kp, vpt, kp_meta, vpt_meta)


def _fox_sample_body(pt_ref, qaug_ref, kn_ref, vn_ref, dtf_ref, bf_ref, et_ref, suf_ref, ck_hbm, cv_hbm, clf_hbm,
                     o_ref, lfo_ref, kbuf, vbuf, lfbuf, sem, m_ref, l_ref, acc_ref, carry_ref, cnrow_ref,
                     *, npp, n_steps, n_pages, n_seq, s_new):
    seq = pl.program_id(0)
    step = pl.program_id(1)
    t = seq * n_steps + step
    slot = t % 2
    rows = ATTN_HEADS * s_new
    qaug = qaug_ref[...]
    nt = (((1,), (1,)), ((), ()))

    def page_copies(seq_i, step_i, s):
        first = seq_i * n_pages + n_pages - (step_i + 1) * npp
        out = []
        for i in range(npp):
            page = pt_ref[first + i]
            out.append(pltpu.make_async_copy(ck_hbm.at[page], kbuf.at[s, i], sem.at[s]))
            out.append(pltpu.make_async_copy(cv_hbm.at[page], vbuf.at[s, i], sem.at[s]))
            out.append(pltpu.make_async_copy(clf_hbm.at[page], lfbuf.at[s, i], sem.at[s]))
        return out

    @pl.when(t == 0)
    def _():
        for c in page_copies(0, 0, 0):
            c.start()

    final = t + 1 == n_seq * n_steps
    wrap = step == n_steps - 1
    nxt_seq = jnp.where(final, seq, jnp.where(wrap, seq + 1, seq))
    nxt_step = jnp.where(final, step, jnp.where(wrap, 0, step + 1))
    for c in page_copies(nxt_seq, nxt_step, 1 - slot):
        c.start()
    for c in page_copies(seq, step, slot):
        c.wait()
    k_pages = [kbuf[slot, i] for i in range(npp)]
    v_pages = [vbuf[slot, i] for i in range(npp)]
    lf_pages = [lfbuf[slot, i] for i in range(npp)]

    @pl.when(step == 0)
    def _():
        lane = lax.broadcasted_iota(jnp.int32, (s_new, LANES), 1)
        lfn = _log_sigmoid(dtf_ref[...] + bf_ref[...])
        lfo_ref[...] = jnp.where((lane >= SSD_HEADS) & (lane < SSD_HEADS + ATTN_HEADS), lfn, 0.0)
        lf16 = lfn[:, SSD_HEADS:SSD_HEADS + ATTN_HEADS]
        tri = (lax.broadcasted_iota(jnp.int32, (s_new, s_new), 1)
               <= lax.broadcasted_iota(jnp.int32, (s_new, s_new), 0)).astype(F32)
        cn = jnp.dot(tri, lf16, precision=HI, preferred_element_type=F32) * LOG2E
        cne = lax.dot_general(et_ref[...], cn, nt, precision=HI, preferred_element_type=F32)
        trow = lax.broadcasted_iota(jnp.int32, (rows, s_new), 0) % s_new
        tcol = lax.broadcasted_iota(jnp.int32, (rows, s_new), 1)
        cn_row = jnp.sum(jnp.where(tcol == trow, cne, 0.0), axis=-1, keepdims=True)
        cnrow_ref[...] = jnp.broadcast_to(cn_row, cnrow_ref.shape)
        ss = lax.dot_general(qaug[:, 0:KV_DIM], kn_ref[...].astype(BF16), nt, preferred_element_type=F32)
        ss = jnp.where(tcol <= trow, ss + cn_row - cne, -jnp.inf)
        m = jnp.max(ss, axis=-1, keepdims=True)
        p = jnp.exp2(ss - m)
        m_ref[...] = jnp.broadcast_to(m, m_ref.shape)
        l_ref[...] = jnp.broadcast_to(jnp.sum(p, axis=-1, keepdims=True), l_ref.shape)
        acc_ref[...] = jnp.dot(p.astype(BF16), vn_ref[...].astype(BF16), preferred_element_type=F32)
        carry_ref[...] = jnp.zeros(carry_ref.shape, F32)

    lf_all = jnp.concatenate(lf_pages, axis=0)
    in_page = jnp.dot(jnp.concatenate(_split3(lf_all), axis=1), suf_ref[...], preferred_element_type=F32)
    page_total = jnp.sum(lf_all, axis=1, keepdims=True)
    carry = carry_ref[:, 0:1]
    scores = [None] * npp
    for i in reversed(range(npp)):
        hs = slice(i * ATTN_HEADS, (i + 1) * ATTN_HEADS)
        r_hi, r_mid, r_lo = _split3((in_page[hs] + carry) * LOG2E)
        carry = carry + page_total[hs]
        kt = k_pages[i].reshape(KV_DIM, PAGE_SIZE).astype(BF16)
        k_aug = jnp.concatenate([kt, r_hi, r_mid, r_lo], axis=0)
        scores[i] = jnp.dot(qaug, k_aug, preferred_element_type=F32)
    carry_ref[...] = jnp.broadcast_to(carry, carry_ref.shape)
    s_all = jnp.concatenate(scores, axis=1) + cnrow_ref[:, 0:1]
    vt_all = jnp.concatenate([v.reshape(KV_DIM, PAGE_SIZE).astype(BF16) for v in v_pages], axis=1)
    m_prev = m_ref[...]
    m_new = jnp.maximum(m_prev, jnp.max(s_all, axis=-1, keepdims=True))
    alpha = jnp.exp2(m_prev - m_new)
    p = jnp.exp2(s_all - m_new[:, 0:1])
    l_ref[...] = alpha * l_ref[...] + jnp.sum(p, axis=-1, keepdims=True)
    m_ref[...] = m_new
    pv = lax.dot_general(p.astype(BF16), vt_all, nt, preferred_element_type=F32)
    acc_ref[...] = jnp.concatenate([alpha, alpha], axis=1) * acc_ref[...] + pv

    @pl.when(step == n_steps - 1)
    def _():
        inv = 1.0 / l_ref[...]
        o_ref[...] = acc_ref[...] * jnp.concatenate([inv, inv], axis=1)

    @pl.when(final)
    def _():
        for c in page_copies(seq, step, 1 - slot):
            c.wait()


SAMPLE_PAGES_PER_STEP = 16


def _fox_sample(u_s, cache_k, cache_v, cache_logf, page_table, b_forget, n_seq, s_new, npp):
    n_pages = page_table.shape[1]
    assert n_pages % npp == 0 and ATTN_HEADS * s_new == LANES
    n_steps = n_pages // npp
    n_pool = cache_k.shape[0]
    rows = ATTN_HEADS * s_new
    ck = jnp.transpose(cache_k, (0, 2, 3, 1))
    cv = jnp.transpose(cache_v, (0, 2, 3, 1))
    clf = jnp.transpose(cache_logf, (0, 2, 1))
    q = u_s[:, U_Q:U_Q + ATTN_DIM].reshape(n_seq, s_new, ATTN_KV_HEADS, ATTN_REP, HEAD_DIM)
    qbd = jnp.einsum("btgrd,gh->bgrthd", q, jnp.eye(ATTN_KV_HEADS, dtype=F32)).reshape(n_seq, rows, KV_DIM)
    et = np.zeros((rows, ATTN_HEADS), np.float32)
    et[np.arange(rows), np.arange(rows) // s_new] = 1.0
    et_b = jnp.broadcast_to(jnp.asarray(et), (n_seq, rows, ATTN_HEADS))
    qaug = jnp.concatenate([qbd * (ATTN_SCALE * LOG2E)] + [et_b] * N_SPLIT, axis=-1).astype(BF16)
    aug_w = KV_DIM + N_SPLIT * ATTN_HEADS
    suf = np.tile(np.triu(np.ones((PAGE_SIZE, PAGE_SIZE), np.float32), 1).T, (N_SPLIT, 1))
    bf = jnp.zeros((1, LANES), F32).at[0, SSD_HEADS:SSD_HEADS + ATTN_HEADS].set(b_forget)

    const2 = lambda b, s, pt: (0, 0)
    hbm = pl.BlockSpec(memory_space=pl.ANY)
    in_specs = [
        pl.BlockSpec((None, rows, aug_w), lambda b, s, pt: (b, 0, 0)),
        pl.BlockSpec((s_new, KV_DIM), lambda b, s, pt: (b, U_K // KV_DIM)),
        pl.BlockSpec((s_new, KV_DIM), lambda b, s, pt: (b, U_V // KV_DIM)),
        pl.BlockSpec((s_new, LANES), lambda b, s, pt: (b, U_DTF // LANES)),
        pl.BlockSpec((1, LANES), const2),
        pl.BlockSpec((rows, ATTN_HEADS), const2),
        pl.BlockSpec((N_SPLIT * PAGE_SIZE, PAGE_SIZE), const2),
        hbm, hbm, hbm,
    ]
    grid_spec = pltpu.PrefetchScalarGridSpec(
        num_scalar_prefetch=1,
        grid=(n_seq, n_steps),
        in_specs=in_specs,
        out_specs=(pl.BlockSpec((None, rows, KV_DIM), lambda b, s, pt: (b, 0, 0)),
                   pl.BlockSpec((s_new, LANES), lambda b, s, pt: (b, 0))),
        scratch_shapes=[pltpu.VMEM((2, npp, ATTN_KV_HEADS, HEAD_DIM, PAGE_SIZE), F32),
                        pltpu.VMEM((2, npp, ATTN_KV_HEADS, HEAD_DIM, PAGE_SIZE), F32),
                        pltpu.VMEM((2, npp, ATTN_HEADS, PAGE_SIZE), F32),
                        pltpu.SemaphoreType.DMA((2,)),
                        pltpu.VMEM((rows, LANES), F32), pltpu.VMEM((rows, LANES), F32),
                        pltpu.VMEM((rows, KV_DIM), F32), pltpu.VMEM((ATTN_HEADS, LANES), F32),
                        pltpu.VMEM((rows, LANES), F32)],
    )
    o_raw, lf_slab = pl.pallas_call(
        functools.partial(_fox_sample_body, npp=npp, n_steps=n_steps, n_pages=n_pages, n_seq=n_seq, s_new=s_new),
        out_shape=(jax.ShapeDtypeStruct((n_seq, rows, KV_DIM), F32),
                   jax.ShapeDtypeStruct((n_seq * s_new, LANES), F32)),
        grid_spec=grid_spec,
        compiler_params=_cparams(("arbitrary", "arbitrary"), 48),
        name="fox_sample",
    )(page_table.reshape(-1), qaug, u_s, u_s, u_s, bf, jnp.asarray(et), jnp.asarray(suf, BF16), ck, cv, clf)
    o = o_raw.reshape(n_seq, ATTN_KV_HEADS, ATTN_REP, s_new, ATTN_KV_HEADS, HEAD_DIM)
    o = jnp.einsum("bgrtgd->btgrd", o).reshape(n_seq * s_new, ATTN_DIM)
    return o, lf_slab


ROUTE_E0, ROUTE_E1, ROUTE_W0, ROUTE_W1 = 0, 1, 2, 3
ROUTER_EXPERT_LANE0 = N_EXPERT_GROUPS
_BIG_LANE = 4 * LANES


def _sigmoid(x):
    return 1.0 / (1.0 + jnp.exp(-x))


def _mix_body(y_ref, o_ref, gs_ref, ga_ref, h_ref, wssd_ref, wattn_ref, wout_ref, g2_ref, wr_ref, br_ref,
              h1_ref, hn_ref, route_ref):
    ys = jnp.dot(y_ref[...].astype(BF16), wssd_ref[...], preferred_element_type=F32)
    oa = jnp.dot(o_ref[...].astype(BF16), wattn_ref[...], preferred_element_type=F32)
    mix = _sigmoid(gs_ref[...]) * ys + _sigmoid(ga_ref[...]) * oa
    h1 = h_ref[...] + jnp.dot(mix.astype(BF16), wout_ref[...], preferred_element_type=F32)
    h1_ref[...] = h1
    ms = jnp.mean(h1 * h1, axis=-1, keepdims=True)
    hn = (h1 * lax.rsqrt(ms + RMS_EPS)) * g2_ref[...]
    hn_ref[...] = hn
    hn_hi = hn.astype(BF16)
    hn_lo = (hn - hn_hi.astype(F32)).astype(BF16)
    logits = jnp.dot(jnp.concatenate([hn_hi, hn_hi, hn_lo], axis=1), wr_ref[...],
                     preferred_element_type=F32) + br_ref[...]
    lane = lax.broadcasted_iota(jnp.int32, logits.shape, 1)
    gl = jnp.where(lane < N_EXPERT_GROUPS, logits, -jnp.inf)
    gmax = jnp.max(gl, axis=-1, keepdims=True)
    gsel = jnp.min(jnp.where(gl == gmax, lane, _BIG_LANE), axis=-1, keepdims=True)
    wgrp = 1.0 / jnp.sum(jnp.exp(gl - gmax), axis=-1, keepdims=True)
    elane = lane - ROUTER_EXPERT_LANE0
    in_group = (elane >= gsel * EXPERTS_PER_GROUP) & (elane < (gsel + 1) * EXPERTS_PER_GROUP)
    el = jnp.where(in_group, logits, -jnp.inf)
    t1 = jnp.max(el, axis=-1, keepdims=True)
    i1 = jnp.min(jnp.where(el == t1, lane, _BIG_LANE), axis=-1, keepdims=True)
    el2 = jnp.where(lane == i1, -jnp.inf, el)
    t2 = jnp.max(el2, axis=-1, keepdims=True)
    i2 = jnp.min(jnp.where(el2 == t2, lane, _BIG_LANE), axis=-1, keepdims=True)
    e21 = jnp.exp(t2 - t1)
    w1 = wgrp / (1.0 + e21)
    w2 = w1 * e21
    route = jnp.where(lane == ROUTE_E0, (i1 - ROUTER_EXPERT_LANE0).astype(F32),
                      jnp.where(lane == ROUTE_E1, (i2 - ROUTER_EXPERT_LANE0).astype(F32),
                                jnp.where(lane == ROUTE_W0, w1, jnp.where(lane == ROUTE_W1, w2, 0.0))))
    route_ref[...] = route


def _mix(y, o, u, h, p, tm):
    t = h.shape[0]
    assert t % tm == 0
    row = lambda i: (i, 0)
    const = lambda i: (0, 0)
    return pl.pallas_call(
        _mix_body,
        out_shape=(jax.ShapeDtypeStruct((t, D_MODEL), F32), jax.ShapeDtypeStruct((t, D_MODEL), F32),
                   jax.ShapeDtypeStruct((t, LANES), F32)),
        grid=(t // tm,),
        in_specs=[
            pl.BlockSpec((tm, D_INNER), row),
            pl.BlockSpec((tm, ATTN_DIM), row),
            pl.BlockSpec((tm, D_MODEL), lambda i: (i, U_GS // D_MODEL)),
            pl.BlockSpec((tm, D_MODEL), lambda i: (i, U_GA // D_MODEL)),
            pl.BlockSpec((tm, D_MODEL), row),
            pl.BlockSpec((D_INNER, D_MODEL), const),
            pl.BlockSpec((ATTN_DIM, D_MODEL), const),
            pl.BlockSpec((D_MODEL, D_MODEL), const),
            pl.BlockSpec((1, D_MODEL), const),
            pl.BlockSpec((3 * D_MODEL, LANES), const),
            pl.BlockSpec((1, LANES), const),
        ],
        out_specs=(pl.BlockSpec((tm, D_MODEL), row), pl.BlockSpec((tm, D_MODEL), row),
                   pl.BlockSpec((tm, LANES), row)),
        compiler_params=_cparams(("parallel",), 56),
        name="mix_route",
    )(y, o, u, u, h, p["w_ssd_br"], p["w_attn_br"], p["w_out"], p["norm2_g"], p["w_router"], p["b_router"])


def _mix_params(w_ssd_br, w_attn_br, w_out, norm2_g, w_rg, b_rg, w_re, b_re):
    n_r = N_EXPERT_GROUPS + N_EXPERTS
    w_router = jnp.concatenate([w_rg, w_re, jnp.zeros((D_MODEL, LANES - n_r), F32)], axis=1)
    w_hi = w_router.astype(BF16)
    w_lo = (w_router - w_hi.astype(F32)).astype(BF16)
    w_router = jnp.concatenate([w_hi, w_lo, w_hi], axis=0)
    b_router = jnp.concatenate([b_rg, b_re, jnp.zeros((LANES - n_r,), F32)]).reshape(1, LANES)
    return dict(w_ssd_br=w_ssd_br.astype(BF16), w_attn_br=w_attn_br.astype(BF16), w_out=w_out.astype(BF16),
                norm2_g=norm2_g.reshape(1, D_MODEL), w_router=w_router, b_router=b_router)


def _moe_rank_body(route_ref, pos_ref, cnt_ref, carry_ref, *, tm):
    @pl.when(pl.program_id(0) == 0)
    def _():
        carry_ref[...] = jnp.zeros(carry_ref.shape, F32)

    route = route_ref[...]
    lane = lax.broadcasted_iota(jnp.int32, (tm, LANES), 1).astype(F32)
    hit0 = lane == route[:, ROUTE_E0:ROUTE_E0 + 1]
    hit1 = lane == route[:, ROUTE_E1:ROUTE_E1 + 1]
    onehot = hit0.astype(F32) + hit1.astype(F32)
    before = (lax.broadcasted_iota(jnp.int32, (tm, tm), 1) < lax.broadcasted_iota(jnp.int32, (tm, tm), 0))
    c = jnp.dot(before.astype(BF16), onehot.astype(BF16), preferred_element_type=F32) + carry_ref[...]
    pos0 = jnp.sum(jnp.where(hit0, c, 0.0), axis=-1, keepdims=True)
    pos1 = jnp.sum(jnp.where(hit1, c, 0.0), axis=-1, keepdims=True)
    pos_ref[...] = jnp.where(lane == 0.0, pos0, jnp.where(lane == 1.0, pos1, 0.0))
    total = carry_ref[...] + jnp.sum(onehot, axis=0, keepdims=True)
    carry_ref[...] = total
    cnt_ref[...] = total


def _moe_rank(route, tm):
    t = route.shape[0]
    assert t % tm == 0
    return pl.pallas_call(
        functools.partial(_moe_rank_body, tm=tm),
        out_shape=(jax.ShapeDtypeStruct((t, LANES), F32), jax.ShapeDtypeStruct((1, LANES), F32)),
        grid=(t // tm,),
        in_specs=[pl.BlockSpec((tm, LANES), lambda i: (i, 0))],
        out_specs=(pl.BlockSpec((tm, LANES), lambda i: (i, 0)), pl.BlockSpec((1, LANES), lambda i: (0, 0))),
        scratch_shapes=[pltpu.VMEM((1, LANES), F32)],
        compiler_params=_cparams(("arbitrary",)),
        name="moe_rank",
    )(route)


def _experts_body(be_ref, nu_ref, x_ref, wg_ref, wu_ref, wd_ref, y_ref, wg16_ref, wu16_ref, wd16_ref):
    i = pl.program_id(0)
    prev = be_ref[jnp.maximum(i - 1, 0)]

    @pl.when((i == 0) | (be_ref[i] != prev))
    def _():
        wg16_ref[...] = wg_ref[...].astype(BF16)
        wu16_ref[...] = wu_ref[...].astype(BF16)
        wd16_ref[...] = wd_ref[...].astype(BF16)

    @pl.when(i < nu_ref[0])
    def _():
        x16 = x_ref[...].astype(BF16)
        hb = _silu(jnp.dot(x16, wg16_ref[...], preferred_element_type=F32)) * jnp.dot(
            x16, wu16_ref[...], preferred_element_type=F32)
        y_ref[...] = jnp.dot(hb.astype(BF16), wd16_ref[...], preferred_element_type=F32)

    @pl.when(i >= nu_ref[0])
    def _():
        y_ref[...] = jnp.zeros(y_ref.shape, F32)


def _experts(xs, block_e, n_used, w_gate, w_up, w_down, blk):
    rows = xs.shape[0]
    n_blocks = rows // blk
    grid_spec = pltpu.PrefetchScalarGridSpec(
        num_scalar_prefetch=2,
        grid=(n_blocks,),
        in_specs=[
            pl.BlockSpec((blk, D_MODEL), lambda i, be, nu: (jnp.minimum(i, nu[0] - 1), 0)),
            pl.BlockSpec((None, D_MODEL, D_EXPERT), lambda i, be, nu: (be[i], 0, 0)),
            pl.BlockSpec((None, D_MODEL, D_EXPERT), lambda i, be, nu: (be[i], 0, 0)),
            pl.BlockSpec((None, D_EXPERT, D_MODEL), lambda i, be, nu: (be[i], 0, 0)),
        ],
        out_specs=pl.BlockSpec((blk, D_MODEL), lambda i, be, nu: (i, 0)),
        scratch_shapes=[pltpu.VMEM((D_MODEL, D_EXPERT), BF16), pltpu.VMEM((D_MODEL, D_EXPERT), BF16),
                        pltpu.VMEM((D_EXPERT, D_MODEL), BF16)],
    )
    return pl.pallas_call(
        _experts_body,
        out_shape=jax.ShapeDtypeStruct((rows, D_MODEL), F32),
        grid_spec=grid_spec,
        compiler_params=_cparams(("arbitrary",), 48),
        name="moe_experts",
    )(block_e, n_used, xs, w_gate, w_up, w_down)


MOE_ROWS_PER_BLOCK = 256
MOE_TOKEN_TILE = 256
DMA_ISSUE_UNROLL = 8


def _row_copy(src, src_row, dst, dst_row, sem):
    return pltpu.make_async_copy(src.at[pl.ds(src_row, 1)], dst.at[pl.ds(dst_row, 1)], sem)


def _dispatch_body(zb_ref, nu_ref, d0_ref, d1_ref, hn_ref, xs_ref, zero_ref, sem, zsem, *, tm, blk, n_blocks):
    @pl.when(pl.program_id(0) == 0)
    def _():
        zero_ref[...] = jnp.zeros(zero_ref.shape, F32)

        def zero_block(b):
            return pltpu.make_async_copy(zero_ref, xs_ref.at[pl.ds(b * blk, blk)], zsem)

        for e in range(N_EXPERTS):
            zero_block(zb_ref[e]).start()

        def start_tail(b, c):
            zero_block(b).start()
            return c

        def wait_tail(b, c):
            zero_block(b).wait()
            return c

        lax.fori_loop(nu_ref[0], n_blocks, start_tail, 0)
        for e in range(N_EXPERTS):
            zero_block(0).wait()
        lax.fori_loop(nu_ref[0], n_blocks, wait_tail, 0)

    def issue(r, c):
        _row_copy(hn_ref, r, xs_ref, d0_ref[r], sem).start()
        _row_copy(hn_ref, r, xs_ref, d1_ref[r], sem).start()
        return c

    lax.fori_loop(0, tm, issue, 0, unroll=DMA_ISSUE_UNROLL)

    def drain(r, c):
        _row_copy(hn_ref, 0, xs_ref, 0, sem).wait()
        _row_copy(hn_ref, 0, xs_ref, 0, sem).wait()
        return c

    lax.fori_loop(0, tm, drain, 0, unroll=DMA_ISSUE_UNROLL)


def _dispatch(hn, d0, d1, zero_blocks, n_used, n_blocks, blk, tm):
    t = hn.shape[0]
    assert t % tm == 0
    smem_tile = lambda: pl.BlockSpec((tm,), lambda i, zb, nu: (i,), memory_space=pltpu.SMEM)
    grid_spec = pltpu.PrefetchScalarGridSpec(
        num_scalar_prefetch=2,
        grid=(t // tm,),
        in_specs=[smem_tile(), smem_tile(), pl.BlockSpec((tm, D_MODEL), lambda i, zb, nu: (i, 0))],
        out_specs=pl.BlockSpec(memory_space=pl.ANY),
        scratch_shapes=[pltpu.VMEM((blk, D_MODEL), F32), pltpu.SemaphoreType.DMA, pltpu.SemaphoreType.DMA],
    )
    return pl.pallas_call(
        functools.partial(_dispatch_body, tm=tm, blk=blk, n_blocks=n_blocks),
        out_shape=jax.ShapeDtypeStruct((n_blocks * blk, D_MODEL), F32),
        grid_spec=grid_spec,
        compiler_params=_cparams(("arbitrary",)),
        name="moe_dispatch",
    )(zero_blocks, n_used, d0, d1, hn)


def _combine_body(d0_ref, d1_ref, d0n_ref, d1n_ref, h1_ref, route_ref, g_ref, ys_ref, o_ref, buf, sem, *, tm, n_tiles):
    i = pl.program_id(0)
    slot = i % 2

    def gather(da_ref, db_ref, s):
        def issue(r, c):
            _row_copy(ys_ref, da_ref[r], buf.at[s, 0], r, sem.at[s]).start()
            _row_copy(ys_ref, db_ref[r], buf.at[s, 1], r, sem.at[s]).start()
            return c
        lax.fori_loop(0, tm, issue, 0, unroll=DMA_ISSUE_UNROLL)

    @pl.when(i == 0)
    def _():
        gather(d0_ref, d1_ref, 0)

    @pl.when(i + 1 < n_tiles)
    def _():
        gather(d0n_ref, d1n_ref, 1 - slot)

    def drain(r, c):
        _row_copy(ys_ref, 0, buf.at[slot, 0], 0, sem.at[slot]).wait()
        _row_copy(ys_ref, 0, buf.at[slot, 1], 0, sem.at[slot]).wait()
        return c

    lax.fori_loop(0, tm, drain, 0, unroll=DMA_ISSUE_UNROLL)
    route = route_ref[...]
    h = (h1_ref[...] + route[:, ROUTE_W0:ROUTE_W0 + 1] * buf[slot, 0]
         + route[:, ROUTE_W1:ROUTE_W1 + 1] * buf[slot, 1])
    ms = jnp.mean(h * h, axis=-1, keepdims=True)
    o_ref[...] = (h * lax.rsqrt(ms + RMS_EPS)) * g_ref[...]


def _combine(h1, ys, d0, d1, route, g, tm):
    t = h1.shape[0]
    n_tiles = t // tm
    row = lambda i: (i, 0)
    cur = lambda: pl.BlockSpec((tm,), lambda i: (i,), memory_space=pltpu.SMEM)
    nxt = lambda: pl.BlockSpec((tm,), lambda i: (jnp.minimum(i + 1, n_tiles - 1),), memory_space=pltpu.SMEM)
    return pl.pallas_call(
        functools.partial(_combine_body, tm=tm, n_tiles=n_tiles),
        out_shape=jax.ShapeDtypeStruct((t, D_MODEL), F32),
        grid=(n_tiles,),
        in_specs=[cur(), cur(), nxt(), nxt(), pl.BlockSpec((tm, D_MODEL), row), pl.BlockSpec((tm, LANES), row),
                  pl.BlockSpec((1, D_MODEL), lambda i: (0, 0)), pl.BlockSpec(memory_space=pl.ANY)],
        out_specs=pl.BlockSpec((tm, D_MODEL), row),
        scratch_shapes=[pltpu.VMEM((2, 2, tm, D_MODEL), F32), pltpu.SemaphoreType.DMA((2,))],
        compiler_params=_cparams(("arbitrary",)),
        name="moe_combine_norm",
    )(d0, d1, d0, d1, h1, route, g.reshape(1, D_MODEL), ys)


def _moe(hn, h1, route, w_gate, w_up, w_down, final_g):
    t = hn.shape[0]
    blk = MOE_ROWS_PER_BLOCK
    pos, cnt = _moe_rank(route, MOE_TOKEN_TILE)
    counts = cnt[0, :N_EXPERTS].astype(jnp.int32)
    padded = (counts + blk - 1) // blk * blk
    ends = jnp.cumsum(padded)
    starts = ends - padded
    expert_ids = jnp.arange(N_EXPERTS, dtype=jnp.int32)

    def dest(e_lane, p_lane):
        e = route[:, e_lane].astype(jnp.int32)
        start = jnp.sum(jnp.where(e[:, None] == expert_ids[None, :], starts[None, :], 0), axis=1)
        return start + pos[:, p_lane].astype(jnp.int32)

    d0 = dest(ROUTE_E0, 0)
    d1 = dest(ROUTE_E1, 1)
    n_blocks = (2 * t + N_EXPERTS * (blk - 1) + blk - 1) // blk
    first_row = jnp.arange(n_blocks, dtype=jnp.int32) * blk
    block_e = jnp.minimum(jnp.sum((ends[None, :] <= first_row[:, None]).astype(jnp.int32), axis=1), N_EXPERTS - 1)
    n_used = (ends[-1] // blk).astype(jnp.int32).reshape(1)
    zero_blocks = jnp.clip((ends - 1) // blk, 0, n_blocks - 1).astype(jnp.int32)
    xs = _dispatch(hn, d0, d1, zero_blocks, n_used, n_blocks, blk, MOE_TOKEN_TILE)
    ys = _experts(xs, block_e, n_used, w_gate, w_up, w_down, blk)
    return _combine(h1, ys, d0, d1, route, final_g, MOE_TOKEN_TILE)


def _conv_history(rows):
    n = rows.shape[0]
    return jnp.concatenate([jnp.zeros((n, CONV_PAD - (CONV_W - 1), CONV_DIM), F32), rows], axis=1)


def _ucols(u, start, width):
    return u[:, start:start + width]


def kernel(x_prompt, x_sample, cache_k, cache_v, cache_logf, state_ssm, state_conv, page_table, meta_tokens, norm1_g, w_in, conv_w, conv_b, dt_bias, a_log, d_skip, ssd_norm_g, b_forget, w_ssd_br, w_attn_br, w_out, norm2_g, w_router_group, b_router_group, w_router_expert, b_router_expert, w_exp_gate, w_exp_up, w_exp_down, final_norm_g):
    nb, sl, _ = x_prompt.shape
    sb, ss, _ = x_sample.shape
    assert w_in.shape[0] == 1, "single-layer step"
    ly = 0
    xp = x_prompt.reshape(nb * sl, D_MODEL)
    xs = x_sample.reshape(sb * ss, D_MODEL)

    wp = _pack_w_in(w_in[ly])
    u_p = _inproj(xp, norm1_g[ly], wp)
    u_s = _inproj(xs, norm1_g[ly], wp)
    u_m = _inproj(meta_tokens, norm1_g[ly], wp)

    sp = _ssd_params(conv_w[ly], conv_b[ly], dt_bias[ly], a_log[ly], d_skip[ly], ssd_norm_g[ly])
    zero_hist = jnp.zeros((1, CONV_PAD, CONV_DIM), F32)
    zero_state = jnp.zeros((1, D_INNER, D_STATE), F32)
    _, h_meta = _ssd(u_m, 1, N_META, N_META, zero_hist, zero_state, sp, BF16)
    hist_meta = _conv_history(_ucols(u_m, U_XBC, CONV_DIM)[None, N_META - (CONV_W - 1):])
    y_p, ssm_p = _ssd(u_p, nb, sl, SSD_CHUNK, hist_meta, h_meta, sp, BF16)
    y_s, ssm_s = _ssd(u_s, sb, ss, ss, _conv_history(state_conv[ly]),
                      state_ssm[ly].reshape(sb, D_INNER, D_STATE), sp, F32)

    slab_m, _, kp_m, vpt_m, kt_m, vt_m = _attn_prep(u_m, b_forget[ly], N_META, N_META, rel_to_last=True)
    slab_p, qp_p, kp_p, vpt_p, kt_p, vt_p = _attn_prep(u_p, b_forget[ly], nb * sl, sl)
    o_p = _fox_prompt(qp_p, kp_p, vpt_p, kp_m, vpt_m, nb, sl, ATTN_TQ)
    o_s, slab_s = _fox_sample(u_s, cache_k[ly], cache_v[ly], cache_logf[ly], page_table, b_forget[ly], sb, ss,
                              SAMPLE_PAGES_PER_STEP)

    mp = _mix_params(w_ssd_br[ly], w_attn_br[ly], w_out[ly], norm2_g[ly], w_router_group[ly], b_router_group[ly],
                     w_router_expert[ly], b_router_expert[ly])
    h1_p, hn_p, route_p = _mix(y_p, o_p, u_p, xp, mp, 256)
    h1_s, hn_s, route_s = _mix(y_s, o_s, u_s, xs, mp, 256)
    out_p = _moe(hn_p, h1_p, route_p, w_exp_gate[ly], w_exp_up[ly], w_exp_down[ly], final_norm_g)
    out_s = _moe(hn_s, h1_s, route_s, w_exp_gate[ly], w_exp_up[ly], w_exp_down[ly], final_norm_g)

    def with_meta(meta_rows, rows, width):
        m = jnp.broadcast_to(meta_rows[None], (nb, N_META, width))
        return jnp.concatenate([m, rows.reshape(nb, sl, width)], axis=1)[None]

    def kv_with_meta(t_meta, t_rows):
        m = jnp.broadcast_to(t_meta.reshape(ATTN_KV_HEADS, HEAD_DIM, 1, N_META), (ATTN_KV_HEADS, HEAD_DIM, nb, N_META))
        full = jnp.concatenate([m, t_rows.reshape(ATTN_KV_HEADS, HEAD_DIM, nb, sl)], axis=3)
        return jnp.transpose(full, (2, 3, 0, 1))[None]

    lf_m = slab_m[:, SSD_HEADS:SSD_HEADS + ATTN_HEADS]
    lf_p = slab_p[:, SSD_HEADS:SSD_HEADS + ATTN_HEADS]
    lf_s = slab_s[:, SSD_HEADS:SSD_HEADS + ATTN_HEADS]
    kv_shape_s = (1, sb, ss, ATTN_KV_HEADS, HEAD_DIM)
    state_shape = (SSD_HEADS, SSD_HEADDIM, D_STATE)
    tail = CONV_W - 1
    return (
        out_p.reshape(nb, sl, D_MODEL),
        out_s.reshape(sb, ss, D_MODEL),
        kv_with_meta(kt_m, kt_p),
        kv_with_meta(vt_m, vt_p),
        with_meta(lf_m, lf_p, ATTN_HEADS),
        ssm_p.reshape((1, nb) + state_shape),
        u_p.reshape(nb, sl, U_COLS)[None, :, sl - tail:, U_XBC:U_XBC + CONV_DIM],
        _ucols(u_s, U_K, KV_DIM).reshape(kv_shape_s),
        _ucols(u_s, U_V, KV_DIM).reshape(kv_shape_s),
        lf_s.reshape(1, sb, ss, ATTN_HEADS),
        ssm_s.reshape((1, sb) + state_shape),
        u_s.reshape(sb, ss, U_COLS)[None, :, ss - tail:, U_XBC:U_XBC + CONV_DIM],
    )
```

```python
import functools

import numpy as np
import jax
import jax.numpy as jnp
from jax import lax
from jax.experimental import pallas as pl
from jax.experimental.pallas import tpu as pltpu

F32 = jnp.float32
BF16 = jnp.bfloat16
HI = lax.Precision.HIGHEST

D_MODEL = 1024
N_META = 16
RMS_EPS = 1e-6
D_INNER = 2048
SSD_HEADDIM = 64
SSD_HEADS = 32
SSD_GROUPS = 8
HEADS_PER_GROUP = SSD_HEADS // SSD_GROUPS
GROUP_W = HEADS_PER_GROUP * SSD_HEADDIM
D_STATE = 128
CONV_W = 4
CONV_DIM = 4096
SSD_CHUNK = 128
ATTN_HEADS = 16
ATTN_KV_HEADS = 4
HEAD_DIM = 64
ATTN_REP = 4
ATTN_DIM = 1024
KV_DIM = 256
ATTN_SCALE = HEAD_DIM ** -0.5
PAGE_SIZE = 128
N_EXPERT_GROUPS = 4
EXPERTS_PER_GROUP = 8
N_EXPERTS = 32
D_EXPERT = 512
LANES = 128

U_XBC = 0
U_Z = 4096
U_Q = 6144
U_GS = 7168
U_GA = 8192
U_K = 9216
U_V = 9472
U_DTF = 9728
U_COLS = 9856
U_COLS_PADDED = 10240
INPROJ_TN = 1024
INPROJ_TM = 2048


def _cparams(sem, vmem_mb=None):
    kw = dict(dimension_semantics=sem)
    if vmem_mb is not None:
        kw["vmem_limit_bytes"] = vmem_mb * 1024 * 1024
    return pltpu.CompilerParams(**kw)


def _inproj_body(x_ref, g_ref, w_ref, o_ref, xn_ref):
    @pl.when(pl.program_id(1) == 0)
    def _():
        x = x_ref[...]
        ms = jnp.mean(x * x, axis=-1, keepdims=True)
        xn_ref[...] = ((x * lax.rsqrt(ms + RMS_EPS)) * g_ref[...]).astype(BF16)

    o_ref[...] = jnp.dot(xn_ref[...], w_ref[...], preferred_element_type=F32)


def _inproj(x, g, wp):
    t = x.shape[0]
    tm = min(t, INPROJ_TM)
    assert t % tm == 0
    return pl.pallas_call(
        _inproj_body,
        out_shape=jax.ShapeDtypeStruct((t, U_COLS), F32),
        grid=(t // tm, U_COLS_PADDED // INPROJ_TN),
        in_specs=[
            pl.BlockSpec((tm, D_MODEL), lambda i, j: (i, 0)),
            pl.BlockSpec((1, D_MODEL), lambda i, j: (0, 0)),
            pl.BlockSpec((D_MODEL, INPROJ_TN), lambda i, j: (0, j)),
        ],
        out_specs=pl.BlockSpec((tm, INPROJ_TN), lambda i, j: (i, j)),
        scratch_shapes=[pltpu.VMEM((tm, D_MODEL), BF16)],
        compiler_params=_cparams(("parallel", "arbitrary"), 56),
        name="inproj",
    )(x, g.reshape(1, D_MODEL), wp)


def _pack_w_in(w_in):
    o = np.cumsum([0, D_INNER, CONV_DIM, SSD_HEADS, ATTN_DIM, KV_DIM, KV_DIM, ATTN_HEADS, D_MODEL, D_MODEL])
    z, xbc, dt, q, k, v, f, gs, ga = [w_in[:, o[i]:o[i + 1]] for i in range(9)]
    pad_dtf = jnp.zeros((D_MODEL, LANES - SSD_HEADS - ATTN_HEADS), w_in.dtype)
    pad = jnp.zeros((D_MODEL, U_COLS_PADDED - U_COLS), w_in.dtype)
    return jnp.concatenate([xbc, z, q, gs, ga, k, v, dt, f, pad_dtf, pad], axis=1).astype(BF16)


def _log_sigmoid(x):
    return jnp.minimum(x, 0.0) - jnp.log1p(jnp.exp(-jnp.abs(x)))


LOG2E = 1.4426950408889634
QK_ONES_LANE = HEAD_DIM
QK_CK_LANE = HEAD_DIM + 3
V_ONES_ROW = HEAD_DIM
N_SPLIT = 3
ATTN_TQ = 256
QK_AHEAD = 4


def _placement_matrices():
    eq = np.zeros((LANES, ATTN_HEADS * LANES), np.float32)
    ek = np.zeros((LANES, ATTN_KV_HEADS * LANES), np.float32)
    for h in range(ATTN_HEADS):
        g, r = divmod(h, ATTN_REP)
        for s in range(N_SPLIT):
            eq[s * ATTN_HEADS + h, h * LANES + QK_ONES_LANE + s] = 1.0
            ek[s * ATTN_HEADS + h, g * LANES + QK_CK_LANE + s * ATTN_REP + r] = 1.0
    cq = np.zeros((1, ATTN_HEADS * LANES), np.float32)
    for h in range(ATTN_HEADS):
        r = h % ATTN_REP
        for s in range(N_SPLIT):
            cq[0, h * LANES + QK_CK_LANE + s * ATTN_REP + r] = -1.0
    return jnp.asarray(eq, BF16), jnp.asarray(ek, BF16), jnp.asarray(cq)


def _transpose_rows(x, q):
    if q < LANES:
        x = jnp.concatenate([x, jnp.zeros((LANES - q, x.shape[1]), x.dtype)], axis=0)
    return x.T[:, :q]


def _split3(x):
    hi = x.astype(BF16)
    r1 = x - hi.astype(F32)
    mid = r1.astype(BF16)
    lo = (r1 - mid.astype(F32)).astype(BF16)
    return hi, mid, lo


def _split3_stacked(x, axis):
    return jnp.concatenate([t.astype(F32) for t in _split3(x)], axis=axis).astype(BF16)


def _attn_prep_body(q_ref, k_ref, v_ref, dtf_ref, bf_ref, eq_ref, ek_ref, cq_ref, slab_ref, qp_ref, kp_ref, vpt_ref,
                    kt_ref, vt_ref, carry_ref, *, tiles_per_seq, tl, rel_to_last):
    i = pl.program_id(0)

    @pl.when(i % tiles_per_seq == 0)
    def _():
        carry_ref[...] = jnp.zeros_like(carry_ref)

    lf = _log_sigmoid(dtf_ref[...] + bf_ref[...])
    row = lax.broadcasted_iota(jnp.int32, (tl, tl), 0)
    col = lax.broadcasted_iota(jnp.int32, (tl, tl), 1)
    c = jnp.dot((col <= row).astype(F32), lf, precision=HI, preferred_element_type=F32) + carry_ref[...]
    carry_ref[...] = c[tl - 1:tl, :]
    lane = lax.broadcasted_iota(jnp.int32, (tl, LANES), 1)
    c16 = jnp.where(lane < ATTN_HEADS, pltpu.roll(c, LANES - SSD_HEADS, 1), 0.0)
    slab_ref[...] = jnp.where((lane >= SSD_HEADS) & (lane < SSD_HEADS + ATTN_HEADS), lf, c16)
    if rel_to_last:
        c16 = c16 - c16[tl - 1:tl, :]
    hi, mid, lo = _split3(c16 * LOG2E)
    x = jnp.where(lane < ATTN_HEADS, hi.astype(F32),
                  jnp.where(lane < 2 * ATTN_HEADS, pltpu.roll(mid.astype(F32), ATTN_HEADS, 1),
                            pltpu.roll(lo.astype(F32), 2 * ATTN_HEADS, 1))).astype(BF16)
    q_extra = jnp.dot(x, eq_ref[...], preferred_element_type=F32) + cq_ref[...]
    k_extra = jnp.dot(x, ek_ref[...], preferred_element_type=F32)
    low = lane < HEAD_DIM
    for h in range(ATTN_HEADS):
        qx = q_ref[:, (h // 2) * LANES:(h // 2 + 1) * LANES]
        if h % 2:
            qx = pltpu.roll(qx, HEAD_DIM, 1)
        tile = jnp.where(low, qx * (ATTN_SCALE * LOG2E), q_extra[:, h * LANES:(h + 1) * LANES])
        qp_ref[:, h * LANES:(h + 1) * LANES] = tile.astype(BF16)
    ones_k = ((lane >= QK_ONES_LANE) & (lane < QK_CK_LANE)).astype(F32)
    ones_v = (lane == V_ONES_ROW).astype(F32)
    for g in range(ATTN_KV_HEADS):
        kx = k_ref[:, (g // 2) * LANES:(g // 2 + 1) * LANES]
        vx = v_ref[:, (g // 2) * LANES:(g // 2 + 1) * LANES]
        if g % 2:
            kx = pltpu.roll(kx, HEAD_DIM, 1)
            vx = pltpu.roll(vx, HEAD_DIM, 1)
        kp_ref[:, g * LANES:(g + 1) * LANES] = jnp.where(low, kx, k_extra[:, g * LANES:(g + 1) * LANES] + ones_k
                                                         ).astype(BF16)
        vpt_ref[g * LANES:(g + 1) * LANES, :] = _transpose_rows(jnp.where(low, vx, ones_v), tl).astype(BF16)
    kt_ref[...] = _transpose_rows(k_ref[...], tl)
    vt_ref[...] = _transpose_rows(v_ref[...], tl)


def _attn_prep(u, b_forget, n_tokens, seq_len, rel_to_last=False):
    tl = min(n_tokens, 256)
    assert n_tokens % tl == 0 and seq_len % tl == 0
    bf = jnp.zeros((1, LANES), F32).at[0, SSD_HEADS:SSD_HEADS + ATTN_HEADS].set(b_forget)
    eq, ek, cq = _placement_matrices()
    const = lambda i: (0, 0)
    return pl.pallas_call(
        functools.partial(_attn_prep_body, tiles_per_seq=seq_len // tl, tl=tl, rel_to_last=rel_to_last),
        out_shape=(jax.ShapeDtypeStruct((n_tokens, LANES), F32),
                   jax.ShapeDtypeStruct((n_tokens, ATTN_HEADS * LANES), BF16),
                   jax.ShapeDtypeStruct((n_tokens, ATTN_KV_HEADS * LANES), BF16),
                   jax.ShapeDtypeStruct((ATTN_KV_HEADS * LANES, n_tokens), BF16),
                   jax.ShapeDtypeStruct((KV_DIM, n_tokens), F32),
                   jax.ShapeDtypeStruct((KV_DIM, n_tokens), F32)),
        grid=(n_tokens // tl,),
        in_specs=[
            pl.BlockSpec((tl, ATTN_DIM), lambda i: (i, U_Q // ATTN_DIM)),
            pl.BlockSpec((tl, KV_DIM), lambda i: (i, U_K // KV_DIM)),
            pl.BlockSpec((tl, KV_DIM), lambda i: (i, U_V // KV_DIM)),
            pl.BlockSpec((tl, LANES), lambda i: (i, U_DTF // LANES)),
            pl.BlockSpec((1, LANES), const),
            pl.BlockSpec((LANES, ATTN_HEADS * LANES), const),
            pl.BlockSpec((LANES, ATTN_KV_HEADS * LANES), const),
            pl.BlockSpec((1, ATTN_HEADS * LANES), const),
        ],
        out_specs=(pl.BlockSpec((tl, LANES), lambda i: (i, 0)),
                   pl.BlockSpec((tl, ATTN_HEADS * LANES), lambda i: (i, 0)),
                   pl.BlockSpec((tl, ATTN_KV_HEADS * LANES), lambda i: (i, 0)),
                   pl.BlockSpec((ATTN_KV_HEADS * LANES, tl), lambda i: (0, i)),
                   pl.BlockSpec((KV_DIM, tl), lambda i: (0, i)),
                   pl.BlockSpec((KV_DIM, tl), lambda i: (0, i))),
        scratch_shapes=[pltpu.VMEM((1, LANES), F32)],
        compiler_params=_cparams(("arbitrary",), 40),
        name="attn_prep",
    )(u, u, u, u, bf, eq, ek, cq)


CONV_PAD = 8


def _silu(x):
    return x * (1.0 / (1.0 + jnp.exp(-x)))


def _softplus(x):
    return jnp.maximum(x, 0.0) + jnp.log1p(jnp.exp(-jnp.abs(x)))


def _ssd_body(xbc_ref, z_ref, dtf_ref, conv0_ref, h0_ref, cw_ref, cb_ref, dtb_ref, alog_ref, dskip_ref,
              gn_ref, e_ref, y_ref, hout_ref, xconv_ref, ht_ref, *, q, n_chunks):
    c = pl.program_id(1)

    @pl.when(c == 0)
    def _():
        xconv_ref[0:CONV_PAD, :] = conv0_ref[0]
        for g in range(SSD_GROUPS):
            ht_ref[g] = h0_ref[0, g * GROUP_W:(g + 1) * GROUP_W, :].T

    xconv_ref[CONV_PAD:CONV_PAD + q, :] = xbc_ref[...]
    acc = cb_ref[...] + xbc_ref[...] * cw_ref[CONV_W - 1:CONV_W, :]
    staged = xconv_ref[...]
    for k in range(CONV_W - 1):
        off = CONV_PAD - (CONV_W - 1) + k
        acc = acc + pltpu.roll(staged, CONV_PAD + q - off, 0)[0:q, :] * cw_ref[k:k + 1, :]
    xc = _silu(acc)
    xconv_ref[CONV_PAD - (CONV_W - 1):CONV_PAD, :] = xconv_ref[CONV_PAD + q - (CONV_W - 1):CONV_PAD + q, :]

    dt = _softplus(dtf_ref[...] + dtb_ref[...])
    a = -jnp.exp(alog_ref[...])
    row = lax.broadcasted_iota(jnp.int32, (q, q), 0)
    col = lax.broadcasted_iota(jnp.int32, (q, q), 1)
    causal = col <= row
    tri = causal.astype(BF16)
    acum = jnp.dot(jnp.concatenate([tri] * N_SPLIT, axis=1), _split3_stacked(dt * a, 0),
                   preferred_element_type=F32)
    acum_t = _transpose_rows(acum, q)
    a_last = acum[q - 1:q, :]
    fac = jnp.concatenate([jnp.exp(acum), jnp.exp(a_last - acum) * dt, dt], axis=0)
    fac = jnp.dot(_split3_stacked(fac, 1), e_ref[...], preferred_element_type=F32)
    ea_full, wst_full, dt_full = fac[0:q], fac[q:2 * q], fac[2 * q:3 * q]

    def group_bc(g):
        bg = xc[:, D_INNER + g * D_STATE:D_INNER + (g + 1) * D_STATE]
        cg = xc[:, D_INNER + SSD_GROUPS * D_STATE + g * D_STATE:D_INNER + SSD_GROUPS * D_STATE + (g + 1) * D_STATE]
        cg16 = cg.astype(BF16)
        cbm = lax.dot_general(cg16, bg.astype(BF16), (((1,), (1,)), ((), ())), preferred_element_type=F32)
        return bg, cg16, cbm

    nxt = group_bc(0)
    for g in range(SSD_GROUPS):
        gs = slice(g * GROUP_W, (g + 1) * GROUP_W)
        xg = xc[:, gs]
        bg, cg16, cbm = nxt
        if g + 1 < SSD_GROUPS:
            nxt = group_bc(g + 1)
        htg = ht_ref[g]
        yoff = jnp.dot(cg16, htg.astype(BF16), preferred_element_type=F32) * ea_full[:, gs]
        xw = (xg * wst_full[:, gs]).astype(BF16)
        bgt = _transpose_rows(bg, q).astype(BF16)
        st = jnp.dot(bgt, xw, preferred_element_type=F32)
        ht_ref[g] = ea_full[q - 1:q, gs] * htg + st
        xdt = (xg * dt_full[:, gs]).astype(BF16)
        yd = []
        for r in range(HEADS_PER_GROUP):
            h = g * HEADS_PER_GROUP + r
            seg = acum[:, h:h + 1] - acum_t[h:h + 1, :]
            m = cbm * jnp.exp(jnp.where(causal, seg, -jnp.inf))
            yd.append(jnp.dot(m.astype(BF16), xdt[:, r * SSD_HEADDIM:(r + 1) * SSD_HEADDIM],
                              preferred_element_type=F32))
        yd = jnp.concatenate(yd, axis=1)
        yg = yd + yoff + dskip_ref[:, gs] * xg
        yz = yg * _silu(z_ref[:, gs])
        ms = jnp.mean(yz * yz, axis=-1, keepdims=True)
        y_ref[:, gs] = (yz * lax.rsqrt(ms + RMS_EPS) * gn_ref[:, gs]).astype(y_ref.dtype)

    @pl.when(c == n_chunks - 1)
    def _():
        for g in range(SSD_GROUPS):
            hout_ref[0, g * GROUP_W:(g + 1) * GROUP_W, :] = ht_ref[g].T


def _head_expand_matrix():
    e = np.zeros((LANES, D_INNER), np.float32)
    for h in range(SSD_HEADS):
        e[h, h * SSD_HEADDIM:(h + 1) * SSD_HEADDIM] = 1.0
    return jnp.asarray(np.tile(e, (N_SPLIT, 1)), BF16)


def _ssd(u, n_seq, seq_len, q, conv0, h0, p, y_dtype):
    n_chunks = seq_len // q
    assert seq_len % q == 0 and q % 8 == 0
    conv_ix = (lambda b, c: (b, 0, 0)) if conv0.shape[0] == n_seq and n_seq > 1 else (lambda b, c: (0, 0, 0))
    h_ix = (lambda b, c: (b, 0, 0)) if h0.shape[0] == n_seq and n_seq > 1 else (lambda b, c: (0, 0, 0))
    const2 = lambda b, c: (0, 0)
    return pl.pallas_call(
        functools.partial(_ssd_body, q=q, n_chunks=n_chunks),
        out_shape=(jax.ShapeDtypeStruct((n_seq * seq_len, D_INNER), y_dtype),
                   jax.ShapeDtypeStruct((n_seq, D_INNER, D_STATE), F32)),
        grid=(n_seq, n_chunks),
        in_specs=[
            pl.BlockSpec((q, CONV_DIM), lambda b, c: (b * n_chunks + c, U_XBC // CONV_DIM)),
            pl.BlockSpec((q, D_INNER), lambda b, c: (b * n_chunks + c, U_Z // D_INNER)),
            pl.BlockSpec((q, LANES), lambda b, c: (b * n_chunks + c, U_DTF // LANES)),
            pl.BlockSpec((1, CONV_PAD, CONV_DIM), conv_ix),
            pl.BlockSpec((1, D_INNER, D_STATE), h_ix),
            pl.BlockSpec((CONV_W, CONV_DIM), const2),
            pl.BlockSpec((1, CONV_DIM), const2),
            pl.BlockSpec((1, LANES), const2),
            pl.BlockSpec((1, LANES), const2),
            pl.BlockSpec((1, D_INNER), const2),
            pl.BlockSpec((1, D_INNER), const2),
            pl.BlockSpec((N_SPLIT * LANES, D_INNER), const2),
        ],
        out_specs=(pl.BlockSpec((q, D_INNER), lambda b, c: (b * n_chunks + c, 0)),
                   pl.BlockSpec((1, D_INNER, D_STATE), lambda b, c: (b, 0, 0))),
        scratch_shapes=[pltpu.VMEM((CONV_PAD + q, CONV_DIM), F32),
                        pltpu.VMEM((SSD_GROUPS, D_STATE, GROUP_W), F32)],
        compiler_params=_cparams(("parallel", "arbitrary"), 48),
        name="ssd_q%d" % q,
    )(u, u, u, conv0, h0, p["conv_w"], p["conv_b"], p["dt_bias"], p["a_log"], p["d_skip"], p["ssd_norm_g"],
      p["head_expand"])


def _ssd_params(conv_w, conv_b, dt_bias, a_log, d_skip, ssd_norm_g):
    pad32 = lambda v: jnp.zeros((1, LANES), F32).at[0, :SSD_HEADS].set(v)
    return dict(conv_w=conv_w, conv_b=conv_b.reshape(1, CONV_DIM), dt_bias=pad32(dt_bias), a_log=pad32(a_log),
                d_skip=jnp.repeat(d_skip, SSD_HEADDIM).reshape(1, D_INNER),
                ssd_norm_g=ssd_norm_g.reshape(1, D_INNER), head_expand=_head_expand_matrix())


def _fox_prompt_body(qi_ref, kj_ref, qp_ref, kp_ref, vpt_ref, kpm_ref, vptm_ref, o_ref, m_ref, acc_ref, *, tq, tk):
    step = pl.program_id(1)
    i = qi_ref[step]
    j = kj_ref[step]
    nt = (((1,), (1,)), ((), ()))

    def scores(h, kp):
        g = h // ATTN_REP
        return lax.dot_general(kp[:, g * LANES:(g + 1) * LANES], qp_ref[:, h * LANES:(h + 1) * LANES], nt,
                               preferred_element_type=F32)

    def attend_all(kp, vpt, mask, ahead=QK_AHEAD):
        pending = [scores(h, kp) for h in range(ahead)]
        for h in range(ATTN_HEADS):
            g = h // ATTN_REP
            s = pending.pop(0)
            if h + ahead < ATTN_HEADS:
                pending.append(scores(h + ahead, kp))
            if mask is not None:
                s = jnp.where(mask, s, -jnp.inf)
            m_prev = m_ref[h, 0:1, :]
            m_new = jnp.maximum(m_prev, jnp.max(s, axis=0, keepdims=True))
            alpha = jnp.exp2(m_prev - m_new)
            p = jnp.exp2(s - m_new).astype(BF16)
            m_ref[h, 0:1, :] = m_new
            pv = jnp.dot(vpt[g * LANES:(g + 1) * LANES, :], p, preferred_element_type=F32)
            acc_ref[h] = alpha * acc_ref[h] + pv

    @pl.when(j == 0)
    def _():
        kpm = kpm_ref[...]
        vptm = vptm_ref[...]
        meta_scores = [scores(h, kpm) for h in range(ATTN_HEADS)]
        for h, s in enumerate(meta_scores):
            g = h // ATTN_REP
            m_new = jnp.max(s, axis=0, keepdims=True)
            m_ref[h, 0:1, :] = m_new
            acc_ref[h] = jnp.dot(vptm[g * LANES:(g + 1) * LANES, :], jnp.exp2(s - m_new).astype(BF16),
                                 preferred_element_type=F32)

    @pl.when(j < i)
    def _():
        attend_all(kp_ref[...], vpt_ref[...], None)

    @pl.when(j == i)
    def _():
        key = lax.broadcasted_iota(jnp.int32, (tk, tq), 0)
        qry = lax.broadcasted_iota(jnp.int32, (tk, tq), 1)
        attend_all(kp_ref[...], vpt_ref[...], key <= qry)
        for h in range(ATTN_HEADS):
            acc = acc_ref[h]
            o_t = acc[0:HEAD_DIM, :] * (1.0 / acc[V_ONES_ROW:V_ONES_ROW + 1, :])
            o_ref[:, h * HEAD_DIM:(h + 1) * HEAD_DIM] = o_t.T.astype(o_ref.dtype)


def _fox_prompt(qp, kp, vpt, kp_meta, vpt_meta, n_seq, seq_len, tq):
    nq = seq_len // tq
    assert seq_len % tq == 0
    qi = np.array([i for i in range(nq) for _ in range(i + 1)], np.int32)
    kj = np.array([j for i in range(nq) for j in range(i + 1)], np.int32)
    n_meta = kp_meta.shape[0]
    qw = ATTN_HEADS * LANES
    kw = ATTN_KV_HEADS * LANES
    grid_spec = pltpu.PrefetchScalarGridSpec(
        num_scalar_prefetch=2,
        grid=(n_seq, len(qi)),
        in_specs=[
            pl.BlockSpec((tq, qw), lambda b, s, qi, kj: (b * nq + qi[s], 0)),
            pl.BlockSpec((tq, kw), lambda b, s, qi, kj: (b * nq + kj[s], 0)),
            pl.BlockSpec((kw, tq), lambda b, s, qi, kj: (0, b * nq + kj[s])),
            pl.BlockSpec((n_meta, kw), lambda b, s, qi, kj: (0, 0)),
            pl.BlockSpec((kw, n_meta), lambda b, s, qi, kj: (0, 0)),
        ],
        out_specs=pl.BlockSpec((tq, ATTN_DIM), lambda b, s, qi, kj: (b * nq + qi[s], 0)),
        scratch_shapes=[pltpu.VMEM((ATTN_HEADS, 8, tq), F32),
                        pltpu.VMEM((ATTN_HEADS, LANES, tq), F32)],
    )
    return pl.pallas_call(
        functools.partial(_fox_prompt_body, tq=tq, tk=tq),
        out_shape=jax.ShapeDtypeStruct((n_seq * seq_len, ATTN_DIM), BF16),
        grid_spec=grid_spec,
        compiler_params=_cparams(("parallel", "arbitrary"), 48),
        name="fox_prompt",
    )(jnp.asarray(qi), jnp.asarray(kj), qp, kp, vpt, kp_meta, vpt_meta)


def _fox_sample_body(pt_ref, qaug_ref, kn_ref, vn_ref, dtf_ref, bf_ref, et_ref, suf_ref, ck_hbm, cv_hbm, clf_hbm,
                     o_ref, lfo_ref, kbuf, vbuf, lfbuf, sem, m_ref, l_ref, acc_ref, carry_ref, cnrow_ref,
                     *, npp, n_steps, n_pages, n_seq, s_new):
    seq = pl.program_id(0)
    step = pl.program_id(1)
    total = n_seq * n_steps
    t = seq * n_steps + step
    slot = t % SAMPLE_DMA_SLOTS
    rows = ATTN_HEADS * s_new
    qaug = qaug_ref[...]
    nt = (((1,), (1,)), ((), ()))

    def page_copies(t_i, s):
        seq_i = t_i // n_steps
        step_i = t_i - seq_i * n_steps
        first = seq_i * n_pages + n_pages - (step_i + 1) * npp
        out = []
        for i in range(npp):
            page = pt_ref[first + i]
            out.append(pltpu.make_async_copy(ck_hbm.at[page], kbuf.at[s, i], sem.at[s]))
            out.append(pltpu.make_async_copy(cv_hbm.at[page], vbuf.at[s, i], sem.at[s]))
            out.append(pltpu.make_async_copy(clf_hbm.at[page], lfbuf.at[s, i], sem.at[s]))
        return out

    @pl.when(t == 0)
    def _():
        for t_i in range(SAMPLE_DMA_SLOTS - 1):
            for c in page_copies(t_i, t_i):
                c.start()

    @pl.when(step == 0)
    def _():
        lane = lax.broadcasted_iota(jnp.int32, (s_new, LANES), 1)
        lfn = _log_sigmoid(dtf_ref[...] + bf_ref[...])
        lfo_ref[...] = jnp.where((lane >= SSD_HEADS) & (lane < SSD_HEADS + ATTN_HEADS), lfn, 0.0)
        lf16 = lfn[:, SSD_HEADS:SSD_HEADS + ATTN_HEADS]
        tri = (lax.broadcasted_iota(jnp.int32, (s_new, s_new), 1)
               <= lax.broadcasted_iota(jnp.int32, (s_new, s_new), 0)).astype(F32)
        cn = jnp.dot(tri, lf16, precision=HI, preferred_element_type=F32) * LOG2E
        cne = lax.dot_general(et_ref[...], cn, nt, precision=HI, preferred_element_type=F32)
        trow = lax.broadcasted_iota(jnp.int32, (rows, s_new), 0) % s_new
        tcol = lax.broadcasted_iota(jnp.int32, (rows, s_new), 1)
        cn_row = jnp.sum(jnp.where(tcol == trow, cne, 0.0), axis=-1, keepdims=True)
        cnrow_ref[...] = jnp.broadcast_to(cn_row, cnrow_ref.shape)
        ss = lax.dot_general(qaug[:, 0:KV_DIM], kn_ref[...].astype(BF16), nt, preferred_element_type=F32)
        ss = jnp.where(tcol <= trow, ss + cn_row - cne, -jnp.inf)
        m = jnp.max(ss, axis=-1, keepdims=True)
        p = jnp.exp2(ss - m)
        m_ref[...] = jnp.broadcast_to(m, m_ref.shape)
        l_ref[...] = jnp.broadcast_to(jnp.sum(p, axis=-1, keepdims=True), l_ref.shape)
        acc_ref[...] = jnp.dot(p.astype(BF16), vn_ref[...].astype(BF16), preferred_element_type=F32)
        carry_ref[...] = jnp.zeros(carry_ref.shape, F32)

    for c in page_copies(t, slot):
        c.wait()
    lf_all = jnp.concatenate([lfbuf[slot, i] for i in range(npp)], axis=0)
    in_page = jnp.dot(jnp.concatenate(_split3(lf_all), axis=1), suf_ref[...], preferred_element_type=F32)
    page_total = jnp.sum(lf_all, axis=1, keepdims=True)
    carry = carry_ref[:, 0:1]
    scores = [None] * npp
    for i in reversed(range(npp)):
        hs = slice(i * ATTN_HEADS, (i + 1) * ATTN_HEADS)
        r_hi, r_mid, r_lo = _split3((in_page[hs] + carry) * LOG2E)
        carry = carry + page_total[hs]
        kt = kbuf[slot, i].reshape(KV_DIM, PAGE_SIZE).astype(BF16)
        k_aug = jnp.concatenate([kt, r_hi, r_mid, r_lo], axis=0)
        scores[i] = jnp.dot(qaug, k_aug, preferred_element_type=F32)
    carry_ref[...] = jnp.broadcast_to(carry, carry_ref.shape)
    s_all = jnp.concatenate(scores, axis=1) + cnrow_ref[:, 0:1]
    vt_all = jnp.concatenate([vbuf[slot, i].reshape(KV_DIM, PAGE_SIZE).astype(BF16) for i in range(npp)], axis=1)
    m_prev = m_ref[...]
    m_new = jnp.maximum(m_prev, jnp.max(s_all, axis=-1, keepdims=True))
    alpha = jnp.exp2(m_prev - m_new)
    p = jnp.exp2(s_all - m_new[:, 0:1])
    l_ref[...] = alpha * l_ref[...] + jnp.sum(p, axis=-1, keepdims=True)
    m_ref[...] = m_new
    pv = lax.dot_general(p.astype(BF16), vt_all, nt, preferred_element_type=F32)
    acc_ref[...] = jnp.concatenate([alpha, alpha], axis=1) * acc_ref[...] + pv
    ahead = t + (SAMPLE_DMA_SLOTS - 1)
    for c in page_copies(jnp.where(ahead < total, ahead, t), ahead % SAMPLE_DMA_SLOTS):
        c.start()

    @pl.when(step == n_steps - 1)
    def _():
        inv = 1.0 / l_ref[...]
        o_ref[...] = acc_ref[...] * jnp.concatenate([inv, inv], axis=1)

    @pl.when(t == total - 1)
    def _():
        for back in range(SAMPLE_DMA_SLOTS - 1):
            for c in page_copies(t, (t + 1 + back) % SAMPLE_DMA_SLOTS):
                c.wait()


SAMPLE_PAGES_PER_STEP = 16
SAMPLE_DMA_SLOTS = 3


def _fox_sample(u_s, cache_k, cache_v, cache_logf, page_table, b_forget, n_seq, s_new, npp):
    n_pages = page_table.shape[1]
    assert n_pages % npp == 0 and ATTN_HEADS * s_new == LANES
    n_steps = n_pages // npp
    n_pool = cache_k.shape[0]
    rows = ATTN_HEADS * s_new
    ck = jnp.transpose(cache_k, (0, 2, 3, 1))
    cv = jnp.transpose(cache_v, (0, 2, 3, 1))
    clf = jnp.transpose(cache_logf, (0, 2, 1))
    q = u_s[:, U_Q:U_Q + ATTN_DIM].reshape(n_seq, s_new, ATTN_KV_HEADS, ATTN_REP, HEAD_DIM)
    qbd = jnp.einsum("btgrd,gh->bgrthd", q, jnp.eye(ATTN_KV_HEADS, dtype=F32)).reshape(n_seq, rows, KV_DIM)
    et = np.zeros((rows, ATTN_HEADS), np.float32)
    et[np.arange(rows), np.arange(rows) // s_new] = 1.0
    et_b = jnp.broadcast_to(jnp.asarray(et), (n_seq, rows, ATTN_HEADS))
    qaug = jnp.concatenate([qbd * (ATTN_SCALE * LOG2E)] + [et_b] * N_SPLIT, axis=-1).astype(BF16)
    aug_w = KV_DIM + N_SPLIT * ATTN_HEADS
    suf = np.tile(np.triu(np.ones((PAGE_SIZE, PAGE_SIZE), np.float32), 1).T, (N_SPLIT, 1))
    bf = jnp.zeros((1, LANES), F32).at[0, SSD_HEADS:SSD_HEADS + ATTN_HEADS].set(b_forget)

    const2 = lambda b, s, pt: (0, 0)
    hbm = pl.BlockSpec(memory_space=pl.ANY)
    in_specs = [
        pl.BlockSpec((None, rows, aug_w), lambda b, s, pt: (b, 0, 0)),
        pl.BlockSpec((s_new, KV_DIM), lambda b, s, pt: (b, U_K // KV_DIM)),
        pl.BlockSpec((s_new, KV_DIM), lambda b, s, pt: (b, U_V // KV_DIM)),
        pl.BlockSpec((s_new, LANES), lambda b, s, pt: (b, U_DTF // LANES)),
        pl.BlockSpec((1, LANES), const2),
        pl.BlockSpec((rows, ATTN_HEADS), const2),
        pl.BlockSpec((N_SPLIT * PAGE_SIZE, PAGE_SIZE), const2),
        hbm, hbm, hbm,
    ]
    grid_spec = pltpu.PrefetchScalarGridSpec(
        num_scalar_prefetch=1,
        grid=(n_seq, n_steps),
        in_specs=in_specs,
        out_specs=(pl.BlockSpec((None, rows, KV_DIM), lambda b, s, pt: (b, 0, 0)),
                   pl.BlockSpec((s_new, LANES), lambda b, s, pt: (b, 0))),
        scratch_shapes=[pltpu.VMEM((SAMPLE_DMA_SLOTS, npp, ATTN_KV_HEADS, HEAD_DIM, PAGE_SIZE), F32),
                        pltpu.VMEM((SAMPLE_DMA_SLOTS, npp, ATTN_KV_HEADS, HEAD_DIM, PAGE_SIZE), F32),
                        pltpu.VMEM((SAMPLE_DMA_SLOTS, npp, ATTN_HEADS, PAGE_SIZE), F32),
                        pltpu.SemaphoreType.DMA((SAMPLE_DMA_SLOTS,)),
                        pltpu.VMEM((rows, LANES), F32), pltpu.VMEM((rows, LANES), F32),
                        pltpu.VMEM((rows, KV_DIM), F32), pltpu.VMEM((ATTN_HEADS, LANES), F32),
                        pltpu.VMEM((rows, LANES), F32)],
    )
    o_raw, lf_slab = pl.pallas_call(
        functools.partial(_fox_sample_body, npp=npp, n_steps=n_steps, n_pages=n_pages, n_seq=n_seq, s_new=s_new),
        out_shape=(jax.ShapeDtypeStruct((n_seq, rows, KV_DIM), F32),
                   jax.ShapeDtypeStruct((n_seq * s_new, LANES), F32)),
        grid_spec=grid_spec,
        compiler_params=_cparams(("arbitrary", "arbitrary"), 48),
        name="fox_sample",
    )(page_table.reshape(-1), qaug, u_s, u_s, u_s, bf, jnp.asarray(et), jnp.asarray(suf, BF16), ck, cv, clf)
    o = o_raw.reshape(n_seq, ATTN_KV_HEADS, ATTN_REP, s_new, ATTN_KV_HEADS, HEAD_DIM)
    o = jnp.einsum("bgrtgd->btgrd", o).reshape(n_seq * s_new, ATTN_DIM)
    return o, lf_slab


ROUTE_E0, ROUTE_E1, ROUTE_W0, ROUTE_W1 = 0, 1, 2, 3
ROUTER_EXPERT_LANE0 = N_EXPERT_GROUPS
_BIG_LANE = 4 * LANES


def _sigmoid(x):
    return 1.0 / (1.0 + jnp.exp(-x))


def _mix_body(y_ref, o_ref, gs_ref, ga_ref, h_ref, wssd_ref, wattn_ref, wout_ref, g2_ref, wr_ref, br_ref,
              h1_ref, hn_ref, route_ref):
    ys = jnp.dot(y_ref[...].astype(BF16), wssd_ref[...], preferred_element_type=F32)
    oa = jnp.dot(o_ref[...].astype(BF16), wattn_ref[...], preferred_element_type=F32)
    mix = _sigmoid(gs_ref[...]) * ys + _sigmoid(ga_ref[...]) * oa
    h1 = h_ref[...] + jnp.dot(mix.astype(BF16), wout_ref[...], preferred_element_type=F32)
    h1_ref[...] = h1
    ms = jnp.mean(h1 * h1, axis=-1, keepdims=True)
    hn = (h1 * lax.rsqrt(ms + RMS_EPS)) * g2_ref[...]
    hn_ref[...] = hn
    hn_hi = hn.astype(BF16)
    hn_lo = (hn - hn_hi.astype(F32)).astype(BF16)
    logits = jnp.dot(jnp.concatenate([hn_hi, hn_hi, hn_lo], axis=1), wr_ref[...], preferred_element_type=F32)
    route_ref[...] = _route(logits + br_ref[...])


def _route(logits):
    lane = lax.broadcasted_iota(jnp.int32, logits.shape, 1)
    gl = jnp.where(lane < N_EXPERT_GROUPS, logits, -jnp.inf)
    gmax = jnp.max(gl, axis=-1, keepdims=True)
    gsel = jnp.min(jnp.where(gl == gmax, lane, _BIG_LANE), axis=-1, keepdims=True)
    wgrp = 1.0 / jnp.sum(jnp.exp(gl - gmax), axis=-1, keepdims=True)
    elane = lane - ROUTER_EXPERT_LANE0
    in_group = (elane >= gsel * EXPERTS_PER_GROUP) & (elane < (gsel + 1) * EXPERTS_PER_GROUP)
    el = jnp.where(in_group, logits, -jnp.inf)
    t1 = jnp.max(el, axis=-1, keepdims=True)
    i1 = jnp.min(jnp.where(el == t1, lane, _BIG_LANE), axis=-1, keepdims=True)
    el2 = jnp.where(lane == i1, -jnp.inf, el)
    t2 = jnp.max(el2, axis=-1, keepdims=True)
    i2 = jnp.min(jnp.where(el2 == t2, lane, _BIG_LANE), axis=-1, keepdims=True)
    e21 = jnp.exp(t2 - t1)
    w1 = wgrp / (1.0 + e21)
    w2 = w1 * e21
    return jnp.where(lane == ROUTE_E0, (i1 - ROUTER_EXPERT_LANE0).astype(F32),
                     jnp.where(lane == ROUTE_E1, (i2 - ROUTER_EXPERT_LANE0).astype(F32),
                               jnp.where(lane == ROUTE_W0, w1, jnp.where(lane == ROUTE_W1, w2, 0.0))))


def _mix(y, o, u, h, p, tm):
    t = h.shape[0]
    assert t % tm == 0
    row = lambda i: (i, 0)
    const = lambda i: (0, 0)
    return pl.pallas_call(
        _mix_body,
        out_shape=(jax.ShapeDtypeStruct((t, D_MODEL), F32), jax.ShapeDtypeStruct((t, D_MODEL), F32),
                   jax.ShapeDtypeStruct((t, LANES), F32)),
        grid=(t // tm,),
        in_specs=[
            pl.BlockSpec((tm, D_INNER), row),
            pl.BlockSpec((tm, ATTN_DIM), row),
            pl.BlockSpec((tm, D_MODEL), lambda i: (i, U_GS // D_MODEL)),
            pl.BlockSpec((tm, D_MODEL), lambda i: (i, U_GA // D_MODEL)),
            pl.BlockSpec((tm, D_MODEL), row),
            pl.BlockSpec((D_INNER, D_MODEL), const),
            pl.BlockSpec((ATTN_DIM, D_MODEL), const),
            pl.BlockSpec((D_MODEL, D_MODEL), const),
            pl.BlockSpec((1, D_MODEL), const),
            pl.BlockSpec((3 * D_MODEL, LANES), const),
            pl.BlockSpec((1, LANES), const),
        ],
        out_specs=(pl.BlockSpec((tm, D_MODEL), row), pl.BlockSpec((tm, D_MODEL), row),
                   pl.BlockSpec((tm, LANES), row)),
        compiler_params=_cparams(("parallel",), 56),
        name="mix_route",
    )(y, o, u, u, h, p["w_ssd_br"], p["w_attn_br"], p["w_out"], p["norm2_g"], p["w_router"], p["b_router"])


def _mix_params(w_ssd_br, w_attn_br, w_out, norm2_g, w_rg, b_rg, w_re, b_re):
    n_r = N_EXPERT_GROUPS + N_EXPERTS
    w_router = jnp.concatenate([w_rg, w_re, jnp.zeros((D_MODEL, LANES - n_r), F32)], axis=1)
    w_hi = w_router.astype(BF16)
    w_lo = (w_router - w_hi.astype(F32)).astype(BF16)
    w_router = jnp.concatenate([w_hi, w_lo, w_hi], axis=0)
    b_router = jnp.concatenate([b_rg, b_re, jnp.zeros((LANES - n_r,), F32)]).reshape(1, LANES)
    return dict(w_ssd_br=w_ssd_br.astype(BF16), w_attn_br=w_attn_br.astype(BF16), w_out=w_out.astype(BF16),
                norm2_g=norm2_g.reshape(1, D_MODEL), w_router=w_router, b_router=b_router)


def _moe_rank_body(route_ref, pos_ref, cnt_ref, carry_ref, *, tm):
    @pl.when(pl.program_id(0) == 0)
    def _():
        carry_ref[...] = jnp.zeros(carry_ref.shape, F32)

    route = route_ref[...]
    lane = lax.broadcasted_iota(jnp.int32, (tm, LANES), 1).astype(F32)
    hit0 = lane == route[:, ROUTE_E0:ROUTE_E0 + 1]
    hit1 = lane == route[:, ROUTE_E1:ROUTE_E1 + 1]
    onehot = hit0.astype(F32) + hit1.astype(F32)
    before = (lax.broadcasted_iota(jnp.int32, (tm, tm), 1) < lax.broadcasted_iota(jnp.int32, (tm, tm), 0))
    c = jnp.dot(before.astype(BF16), onehot.astype(BF16), preferred_element_type=F32) + carry_ref[...]
    pos0 = jnp.sum(jnp.where(hit0, c, 0.0), axis=-1, keepdims=True)
    pos1 = jnp.sum(jnp.where(hit1, c, 0.0), axis=-1, keepdims=True)
    pos_ref[...] = jnp.where(lane == 0.0, pos0, jnp.where(lane == 1.0, pos1, 0.0))
    total = carry_ref[...] + jnp.sum(onehot, axis=0, keepdims=True)
    carry_ref[...] = total
    cnt_ref[...] = total


def _moe_rank(route, tm):
    t = route.shape[0]
    assert t % tm == 0
    return pl.pallas_call(
        functools.partial(_moe_rank_body, tm=tm),
        out_shape=(jax.ShapeDtypeStruct((t, LANES), F32), jax.ShapeDtypeStruct((1, LANES), F32)),
        grid=(t // tm,),
        in_specs=[pl.BlockSpec((tm, LANES), lambda i: (i, 0))],
        out_specs=(pl.BlockSpec((tm, LANES), lambda i: (i, 0)), pl.BlockSpec((1, LANES), lambda i: (0, 0))),
        scratch_shapes=[pltpu.VMEM((1, LANES), F32)],
        compiler_params=_cparams(("arbitrary",)),
        name="moe_rank",
    )(route)


def _experts_body(be_ref, nu_ref, x_ref, wg_ref, wu_ref, wd_ref, y_ref, wg16_ref, wu16_ref, wd16_ref):
    i = pl.program_id(0)
    prev = be_ref[jnp.maximum(i - 1, 0)]

    @pl.when((i == 0) | (be_ref[i] != prev))
    def _():
        wg16_ref[...] = wg_ref[...].astype(BF16)
        wu16_ref[...] = wu_ref[...].astype(BF16)
        wd16_ref[...] = wd_ref[...].astype(BF16)

    @pl.when(i < nu_ref[0])
    def _():
        sub = x_ref.shape[0] // 2
        rows = [slice(0, sub), slice(sub, 2 * sub)]
        gate_up = []
        for r in rows:
            x16 = x_ref[r, :].astype(BF16)
            gate_up.append((jnp.dot(x16, wg16_ref[...], preferred_element_type=F32),
                            jnp.dot(x16, wu16_ref[...], preferred_element_type=F32)))
        for r, (gt, up) in zip(rows, gate_up):
            y_ref[r, :] = jnp.dot((_silu(gt) * up).astype(BF16), wd16_ref[...], preferred_element_type=F32)

    @pl.when(i >= nu_ref[0])
    def _():
        y_ref[...] = jnp.zeros(y_ref.shape, F32)


def _experts(xs, block_e, n_used, w_gate, w_up, w_down, blk):
    rows = xs.shape[0]
    n_blocks = rows // blk
    grid_spec = pltpu.PrefetchScalarGridSpec(
        num_scalar_prefetch=2,
        grid=(n_blocks,),
        in_specs=[
            pl.BlockSpec((blk, D_MODEL), lambda i, be, nu: (jnp.minimum(i, nu[0] - 1), 0)),
            pl.BlockSpec((None, D_MODEL, D_EXPERT), lambda i, be, nu: (be[i], 0, 0)),
            pl.BlockSpec((None, D_MODEL, D_EXPERT), lambda i, be, nu: (be[i], 0, 0)),
            pl.BlockSpec((None, D_EXPERT, D_MODEL), lambda i, be, nu: (be[i], 0, 0)),
        ],
        out_specs=pl.BlockSpec((blk, D_MODEL), lambda i, be, nu: (i, 0)),
        scratch_shapes=[pltpu.VMEM((D_MODEL, D_EXPERT), BF16), pltpu.VMEM((D_MODEL, D_EXPERT), BF16),
                        pltpu.VMEM((D_EXPERT, D_MODEL), BF16)],
    )
    return pl.pallas_call(
        _experts_body,
        out_shape=jax.ShapeDtypeStruct((rows, D_MODEL), F32),
        grid_spec=grid_spec,
        compiler_params=_cparams(("arbitrary",), 48),
        name="moe_experts",
    )(block_e, n_used, xs, w_gate, w_up, w_down)


MOE_ROWS_PER_BLOCK = 256
MOE_TOKEN_TILE = 256
DMA_ISSUE_UNROLL = 8


def _row_copy(src, src_row, dst, dst_row, sem):
    return pltpu.make_async_copy(src.at[pl.ds(src_row, 1)], dst.at[pl.ds(dst_row, 1)], sem)


def _dispatch_body(zb_ref, nu_ref, d0_ref, d1_ref, hn_ref, xs_ref, zero_ref, sem, zsem, *, tm, blk, n_blocks):
    @pl.when(pl.program_id(0) == 0)
    def _():
        zero_ref[...] = jnp.zeros(zero_ref.shape, F32)

        def zero_block(b):
            return pltpu.make_async_copy(zero_ref, xs_ref.at[pl.ds(b * blk, blk)], zsem)

        for e in range(N_EXPERTS):
            zero_block(zb_ref[e]).start()

        def start_tail(b, c):
            zero_block(b).start()
            return c

        def wait_tail(b, c):
            zero_block(b).wait()
            return c

        lax.fori_loop(nu_ref[0], n_blocks, start_tail, 0)
        for e in range(N_EXPERTS):
            zero_block(0).wait()
        lax.fori_loop(nu_ref[0], n_blocks, wait_tail, 0)

    def issue(r, c):
        _row_copy(hn_ref, r, xs_ref, d0_ref[r], sem).start()
        _row_copy(hn_ref, r, xs_ref, d1_ref[r], sem).start()
        return c

    lax.fori_loop(0, tm, issue, 0, unroll=DMA_ISSUE_UNROLL)

    def drain(r, c):
        _row_copy(hn_ref, 0, xs_ref, 0, sem).wait()
        _row_copy(hn_ref, 0, xs_ref, 0, sem).wait()
        return c

    lax.fori_loop(0, tm, drain, 0, unroll=DMA_ISSUE_UNROLL)


def _dispatch(hn, d0, d1, zero_blocks, n_used, n_blocks, blk, tm):
    t = hn.shape[0]
    assert t % tm == 0
    smem_tile = lambda: pl.BlockSpec((tm,), lambda i, zb, nu: (i,), memory_space=pltpu.SMEM)
    grid_spec = pltpu.PrefetchScalarGridSpec(
        num_scalar_prefetch=2,
        grid=(t // tm,),
        in_specs=[smem_tile(), smem_tile(), pl.BlockSpec((tm, D_MODEL), lambda i, zb, nu: (i, 0))],
        out_specs=pl.BlockSpec(memory_space=pl.ANY),
        scratch_shapes=[pltpu.VMEM((blk, D_MODEL), F32), pltpu.SemaphoreType.DMA, pltpu.SemaphoreType.DMA],
    )
    return pl.pallas_call(
        functools.partial(_dispatch_body, tm=tm, blk=blk, n_blocks=n_blocks),
        out_shape=jax.ShapeDtypeStruct((n_blocks * blk, D_MODEL), F32),
        grid_spec=grid_spec,
        compiler_params=_cparams(("arbitrary",)),
        name="moe_dispatch",
    )(zero_blocks, n_used, d0, d1, hn)


def _combine_body(d0_ref, d1_ref, d0n_ref, d1n_ref, h1_ref, route_ref, g_ref, ys_ref, o_ref, buf, sem, *, tm, n_tiles):
    i = pl.program_id(0)
    slot = i % 2

    def gather(da_ref, db_ref, s):
        def issue(r, c):
            _row_copy(ys_ref, da_ref[r], buf.at[s, 0], r, sem.at[s]).start()
            _row_copy(ys_ref, db_ref[r], buf.at[s, 1], r, sem.at[s]).start()
            return c
        lax.fori_loop(0, tm, issue, 0, unroll=DMA_ISSUE_UNROLL)

    @pl.when(i == 0)
    def _():
        gather(d0_ref, d1_ref, 0)

    @pl.when(i + 1 < n_tiles)
    def _():
        gather(d0n_ref, d1n_ref, 1 - slot)

    def drain(r, c):
        _row_copy(ys_ref, 0, buf.at[slot, 0], 0, sem.at[slot]).wait()
        _row_copy(ys_ref, 0, buf.at[slot, 1], 0, sem.at[slot]).wait()
        return c

    lax.fori_loop(0, tm, drain, 0, unroll=DMA_ISSUE_UNROLL)
    route = route_ref[...]
    h = (h1_ref[...] + route[:, ROUTE_W0:ROUTE_W0 + 1] * buf[slot, 0]
         + route[:, ROUTE_W1:ROUTE_W1 + 1] * buf[slot, 1])
    ms = jnp.mean(h * h, axis=-1, keepdims=True)
    o_ref[...] = (h * lax.rsqrt(ms + RMS_EPS)) * g_ref[...]


def _combine(h1, ys, d0, d1, route, g, tm):
    t = h1.shape[0]
    n_tiles = t // tm
    row = lambda i: (i, 0)
    cur = lambda: pl.BlockSpec((tm,), lambda i: (i,), memory_space=pltpu.SMEM)
    nxt = lambda: pl.BlockSpec((tm,), lambda i: (jnp.minimum(i + 1, n_tiles - 1),), memory_space=pltpu.SMEM)
    return pl.pallas_call(
        functools.partial(_combine_body, tm=tm, n_tiles=n_tiles),
        out_shape=jax.ShapeDtypeStruct((t, D_MODEL), F32),
        grid=(n_tiles,),
        in_specs=[cur(), cur(), nxt(), nxt(), pl.BlockSpec((tm, D_MODEL), row), pl.BlockSpec((tm, LANES), row),
                  pl.BlockSpec((1, D_MODEL), lambda i: (0, 0)), pl.BlockSpec(memory_space=pl.ANY)],
        out_specs=pl.BlockSpec((tm, D_MODEL), row),
        scratch_shapes=[pltpu.VMEM((2, 2, tm, D_MODEL), F32), pltpu.SemaphoreType.DMA((2,))],
        compiler_params=_cparams(("arbitrary",)),
        name="moe_combine_norm",
    )(d0, d1, d0, d1, h1, route, g.reshape(1, D_MODEL), ys)


def _moe(hn, h1, route, w_gate, w_up, w_down, final_g):
    t = hn.shape[0]
    blk = MOE_ROWS_PER_BLOCK
    pos, cnt = _moe_rank(route, MOE_TOKEN_TILE)
    counts = cnt[0, :N_EXPERTS].astype(jnp.int32)
    padded = (counts + blk - 1) // blk * blk
    ends = jnp.cumsum(padded)
    starts = ends - padded
    expert_ids = jnp.arange(N_EXPERTS, dtype=jnp.int32)

    def dest(e_lane, p_lane):
        e = route[:, e_lane].astype(jnp.int32)
        start = jnp.sum(jnp.where(e[:, None] == expert_ids[None, :], starts[None, :], 0), axis=1)
        return start + pos[:, p_lane].astype(jnp.int32)

    d0 = dest(ROUTE_E0, 0)
    d1 = dest(ROUTE_E1, 1)
    n_blocks = (2 * t + N_EXPERTS * (blk - 1) + blk - 1) // blk
    first_row = jnp.arange(n_blocks, dtype=jnp.int32) * blk
    block_e = jnp.minimum(jnp.sum((ends[None, :] <= first_row[:, None]).astype(jnp.int32), axis=1), N_EXPERTS - 1)
    n_used = (ends[-1] // blk).astype(jnp.int32).reshape(1)
    zero_blocks = jnp.clip((ends - 1) // blk, 0, n_blocks - 1).astype(jnp.int32)
    xs = _dispatch(hn, d0, d1, zero_blocks, n_used, n_blocks, blk, MOE_TOKEN_TILE)
    ys = _experts(xs, block_e, n_used, w_gate, w_up, w_down, blk)
    return _combine(h1, ys, d0, d1, route, final_g, MOE_TOKEN_TILE)


def _conv_history(rows):
    n = rows.shape[0]
    return jnp.concatenate([jnp.zeros((n, CONV_PAD - (CONV_W - 1), CONV_DIM), F32), rows], axis=1)


def _ucols(u, start, width):
    return u[:, start:start + width]


def kernel(x_prompt, x_sample, cache_k, cache_v, cache_logf, state_ssm, state_conv, page_table, meta_tokens, norm1_g, w_in, conv_w, conv_b, dt_bias, a_log, d_skip, ssd_norm_g, b_forget, w_ssd_br, w_attn_br, w_out, norm2_g, w_router_group, b_router_group, w_router_expert, b_router_expert, w_exp_gate, w_exp_up, w_exp_down, final_norm_g):
    nb, sl, _ = x_prompt.shape
    sb, ss, _ = x_sample.shape
    assert w_in.shape[0] == 1, "single-layer step"
    ly = 0
    xp = x_prompt.reshape(nb * sl, D_MODEL)
    xs = x_sample.reshape(sb * ss, D_MODEL)

    wp = _pack_w_in(w_in[ly])
    u_p = _inproj(xp, norm1_g[ly], wp)
    u_s = _inproj(xs, norm1_g[ly], wp)
    u_m = _inproj(meta_tokens, norm1_g[ly], wp)

    sp = _ssd_params(conv_w[ly], conv_b[ly], dt_bias[ly], a_log[ly], d_skip[ly], ssd_norm_g[ly])
    zero_hist = jnp.zeros((1, CONV_PAD, CONV_DIM), F32)
    zero_state = jnp.zeros((1, D_INNER, D_STATE), F32)
    _, h_meta = _ssd(u_m, 1, N_META, N_META, zero_hist, zero_state, sp, BF16)
    hist_meta = _conv_history(_ucols(u_m, U_XBC, CONV_DIM)[None, N_META - (CONV_W - 1):])
    y_p, ssm_p = _ssd(u_p, nb, sl, SSD_CHUNK, hist_meta, h_meta, sp, BF16)
    y_s, ssm_s = _ssd(u_s, sb, ss, ss, _conv_history(state_conv[ly]),
                      state_ssm[ly].reshape(sb, D_INNER, D_STATE), sp, F32)

    slab_m, _, kp_m, vpt_m, kt_m, vt_m = _attn_prep(u_m, b_forget[ly], N_META, N_META, rel_to_last=True)
    slab_p, qp_p, kp_p, vpt_p, kt_p, vt_p = _attn_prep(u_p, b_forget[ly], nb * sl, sl)
    o_p = _fox_prompt(qp_p, kp_p, vpt_p, kp_m, vpt_m, nb, sl, ATTN_TQ)
    o_s, slab_s = _fox_sample(u_s, cache_k[ly], cache_v[ly], cache_logf[ly], page_table, b_forget[ly], sb, ss,
                              SAMPLE_PAGES_PER_STEP)

    mp = _mix_params(w_ssd_br[ly], w_attn_br[ly], w_out[ly], norm2_g[ly], w_router_group[ly], b_router_group[ly],
                     w_router_expert[ly], b_router_expert[ly])
    h1_p, hn_p, route_p = _mix(y_p, o_p, u_p, xp, mp, 256)
    h1_s, hn_s, route_s = _mix(y_s, o_s, u_s, xs, mp, 256)
    out_p = _moe(hn_p, h1_p, route_p, w_exp_gate[ly], w_exp_up[ly], w_exp_down[ly], final_norm_g)
    out_s = _moe(hn_s, h1_s, route_s, w_exp_gate[ly], w_exp_up[ly], w_exp_down[ly], final_norm_g)

    def with_meta(meta_rows, rows, width):
        m = jnp.broadcast_to(meta_rows[None], (nb, N_META, width))
        return jnp.concatenate([m, rows.reshape(nb, sl, width)], axis=1)[None]

    def kv_with_meta(t_meta, t_rows):
        m = jnp.broadcast_to(t_meta.reshape(ATTN_KV_HEADS, HEAD_DIM, 1, N_META), (ATTN_KV_HEADS, HEAD_DIM, nb, N_META))
        full = jnp.concatenate([m, t_rows.reshape(ATTN_KV_HEADS, HEAD_DIM, nb, sl)], axis=3)
        return jnp.transpose(full, (2, 3, 0, 1))[None]

    lf_m = slab_m[:, SSD_HEADS:SSD_HEADS + ATTN_HEADS]
    lf_p = slab_p[:, SSD_HEADS:SSD_HEADS + ATTN_HEADS]
    lf_s = slab_s[:, SSD_HEADS:SSD_HEADS + ATTN_HEADS]
    kv_shape_s = (1, sb, ss, ATTN_KV_HEADS, HEAD_DIM)
    state_shape = (SSD_HEADS, SSD_HEADDIM, D_STATE)
    tail = CONV_W - 1
    return (
        out_p.reshape(nb, sl, D_MODEL),
        out_s.reshape(sb, ss, D_MODEL),
        kv_with_meta(kt_m, kt_p),
        kv_with_meta(vt_m, vt_p),
        with_meta(lf_m, lf_p, ATTN_HEADS),
        ssm_p.reshape((1, nb) + state_shape),
        u_p.reshape(nb, sl, U_COLS)[None, :, sl - tail:, U_XBC:U_XBC + CONV_DIM],
        _ucols(u_s, U_K, KV_DIM).reshape(kv_shape_s),
        _ucols(u_s, U_V, KV_DIM).reshape(kv_shape_s),
        lf_s.reshape(1, sb, ss, ATTN_HEADS),
        ssm_s.reshape((1, sb) + state_shape),
        u_s.reshape(sb, ss, U_COLS)[None, :, ss - tail:, U_XBC:U_XBC + CONV_DIM],
    )
```

```python
import functools

import numpy as np
import jax
import jax.numpy as jnp
from jax import lax
from jax.experimental import pallas as pl
from jax.experimental.pallas import tpu as pltpu

F32 = jnp.float32
BF16 = jnp.bfloat16
HI = lax.Precision.HIGHEST

D_MODEL = 1024
N_META = 16
RMS_EPS = 1e-6
D_INNER = 2048
SSD_HEADDIM = 64
SSD_HEADS = 32
SSD_GROUPS = 8
HEADS_PER_GROUP = SSD_HEADS // SSD_GROUPS
GROUP_W = HEADS_PER_GROUP * SSD_HEADDIM
D_STATE = 128
CONV_W = 4
CONV_DIM = 4096
SSD_CHUNK = 128
ATTN_HEADS = 16
ATTN_KV_HEADS = 4
HEAD_DIM = 64
ATTN_REP = 4
ATTN_DIM = 1024
KV_DIM = 256
ATTN_SCALE = HEAD_DIM ** -0.5
PAGE_SIZE = 128
N_EXPERT_GROUPS = 4
EXPERTS_PER_GROUP = 8
N_EXPERTS = 32
D_EXPERT = 512
LANES = 128

U_XBC = 0
U_Z = 4096
U_Q = 6144
U_GS = 7168
U_GA = 8192
U_K = 9216
U_V = 9472
U_DTF = 9728
U_COLS = 9856
U_COLS_PADDED = 10240
INPROJ_TN = 1024
INPROJ_TM = 2048


def _cparams(sem, vmem_mb=None):
    kw = dict(dimension_semantics=sem)
    if vmem_mb is not None:
        kw["vmem_limit_bytes"] = vmem_mb * 1024 * 1024
    return pltpu.CompilerParams(**kw)


def _inproj_body(x_ref, g_ref, w_ref, o_ref, xn_ref):
    @pl.when(pl.program_id(1) == 0)
    def _():
        x = x_ref[...]
        ms = jnp.mean(x * x, axis=-1, keepdims=True)
        xn_ref[...] = ((x * lax.rsqrt(ms + RMS_EPS)) * g_ref[...]).astype(BF16)

    o_ref[...] = jnp.dot(xn_ref[...], w_ref[...], preferred_element_type=F32)


def _inproj(x, g, wp):
    t = x.shape[0]
    tm = min(t, INPROJ_TM)
    assert t % tm == 0
    return pl.pallas_call(
        _inproj_body,
        out_shape=jax.ShapeDtypeStruct((t, U_COLS), F32),
        grid=(t // tm, U_COLS_PADDED // INPROJ_TN),
        in_specs=[
            pl.BlockSpec((tm, D_MODEL), lambda i, j: (i, 0)),
            pl.BlockSpec((1, D_MODEL), lambda i, j: (0, 0)),
            pl.BlockSpec((D_MODEL, INPROJ_TN), lambda i, j: (0, j)),
        ],
        out_specs=pl.BlockSpec((tm, INPROJ_TN), lambda i, j: (i, j)),
        scratch_shapes=[pltpu.VMEM((tm, D_MODEL), BF16)],
        compiler_params=_cparams(("parallel", "arbitrary"), 56),
        name="inproj",
    )(x, g.reshape(1, D_MODEL), wp)


CONV_PAD = 8
CONV_COL_BLOCKS = CONV_DIM // INPROJ_TN
CONV_ROW_CHUNK = 256


def _conv_silu_rows(staged, rows, w_ref, b_ref):
    acc = b_ref[...] + staged[CONV_PAD:CONV_PAD + rows, :] * w_ref[CONV_W - 1:CONV_W, :]
    for k in range(CONV_W - 1):
        off = CONV_PAD - (CONV_W - 1) + k
        acc = acc + pltpu.roll(staged, CONV_PAD + rows - off, 0)[0:rows, :] * w_ref[k:k + 1, :]
    return _silu(acc)


def _inproj_conv_body(x_ref, g_ref, w_ref, hist_ref, cw_ref, cb_ref, o_ref, tail_ref, xn_ref, stage_ref, carry_ref,
                      *, tm, tiles_per_seq):
    i = pl.program_id(0)
    j = pl.program_id(1)

    @pl.when(j == 0)
    def _():
        x = x_ref[...]
        ms = jnp.mean(x * x, axis=-1, keepdims=True)
        xn_ref[...] = ((x * lax.rsqrt(ms + RMS_EPS)) * g_ref[...]).astype(BF16)

    r = jnp.dot(xn_ref[...], w_ref[...], preferred_element_type=F32)

    @pl.when(j >= CONV_COL_BLOCKS)
    def _():
        o_ref[...] = r

    @pl.when(j < CONV_COL_BLOCKS)
    def _():
        @pl.when(i % tiles_per_seq == 0)
        def _():
            stage_ref[0:CONV_PAD, :] = hist_ref[0]

        @pl.when(i % tiles_per_seq != 0)
        def _():
            stage_ref[0:CONV_PAD, :] = carry_ref[j]

        stage_ref[CONV_PAD:CONV_PAD + tm, :] = r
        tail = stage_ref[tm:tm + CONV_PAD, :]
        carry_ref[j] = tail
        tail_ref[0] = tail
        for c in range(tm // CONV_ROW_CHUNK):
            lo = c * CONV_ROW_CHUNK
            staged = stage_ref[lo:lo + CONV_PAD + CONV_ROW_CHUNK, :]
            o_ref[lo:lo + CONV_ROW_CHUNK, :] = _conv_silu_rows(staged, CONV_ROW_CHUNK, cw_ref, cb_ref)


def _inproj_conv(x, g, wp, hist, conv_w, conv_b, seq_len):
    t = x.shape[0]
    tm = min(seq_len, INPROJ_TM)
    assert t % seq_len == 0 and seq_len % tm == 0 and tm % CONV_ROW_CHUNK == 0
    tiles_per_seq = seq_len // tm
    cblk = lambda i, j: jnp.minimum(j, CONV_COL_BLOCKS - 1)
    return pl.pallas_call(
        functools.partial(_inproj_conv_body, tm=tm, tiles_per_seq=tiles_per_seq),
        out_shape=(jax.ShapeDtypeStruct((t, U_COLS), F32),
                   jax.ShapeDtypeStruct((t // seq_len, CONV_PAD, CONV_DIM), F32)),
        grid=(t // tm, U_COLS_PADDED // INPROJ_TN),
        in_specs=[
            pl.BlockSpec((tm, D_MODEL), lambda i, j: (i, 0)),
            pl.BlockSpec((1, D_MODEL), lambda i, j: (0, 0)),
            pl.BlockSpec((D_MODEL, INPROJ_TN), lambda i, j: (0, j)),
            pl.BlockSpec((1, CONV_PAD, INPROJ_TN), lambda i, j: (0, 0, cblk(i, j))),
            pl.BlockSpec((CONV_W, INPROJ_TN), lambda i, j: (0, cblk(i, j))),
            pl.BlockSpec((1, INPROJ_TN), lambda i, j: (0, cblk(i, j))),
        ],
        out_specs=(pl.BlockSpec((tm, INPROJ_TN), lambda i, j: (i, j)),
                   pl.BlockSpec((1, CONV_PAD, INPROJ_TN), lambda i, j: (i // tiles_per_seq, 0, cblk(i, j)))),
        scratch_shapes=[pltpu.VMEM((tm, D_MODEL), BF16),
                        pltpu.VMEM((CONV_PAD + tm, INPROJ_TN), F32),
                        pltpu.VMEM((CONV_COL_BLOCKS, CONV_PAD, INPROJ_TN), F32)],
        compiler_params=_cparams(("arbitrary", "arbitrary"), 58),
        name="inproj_conv",
    )(x, g.reshape(1, D_MODEL), wp, hist, conv_w, conv_b.reshape(1, CONV_DIM))


def _pack_w_in(w_in):
    o = np.cumsum([0, D_INNER, CONV_DIM, SSD_HEADS, ATTN_DIM, KV_DIM, KV_DIM, ATTN_HEADS, D_MODEL, D_MODEL])
    z, xbc, dt, q, k, v, f, gs, ga = [w_in[:, o[i]:o[i + 1]] for i in range(9)]
    pad_dtf = jnp.zeros((D_MODEL, LANES - SSD_HEADS - ATTN_HEADS), w_in.dtype)
    pad = jnp.zeros((D_MODEL, U_COLS_PADDED - U_COLS), w_in.dtype)
    return jnp.concatenate([xbc, z, q, gs, ga, k, v, dt, f, pad_dtf, pad], axis=1).astype(BF16)


def _log_sigmoid(x):
    return jnp.minimum(x, 0.0) - jnp.log1p(jnp.exp(-jnp.abs(x)))


LOG2E = 1.4426950408889634
QK_ONES_LANE = HEAD_DIM
QK_CK_LANE = HEAD_DIM + 3
V_ONES_ROW = HEAD_DIM
N_SPLIT = 3
ATTN_TQ = 256
QK_AHEAD = 4


def _placement_matrices():
    eq = np.zeros((LANES, ATTN_HEADS * LANES), np.float32)
    ek = np.zeros((LANES, ATTN_KV_HEADS * LANES), np.float32)
    for h in range(ATTN_HEADS):
        g, r = divmod(h, ATTN_REP)
        for s in range(N_SPLIT):
            eq[s * ATTN_HEADS + h, h * LANES + QK_ONES_LANE + s] = 1.0
            ek[s * ATTN_HEADS + h, g * LANES + QK_CK_LANE + s * ATTN_REP + r] = 1.0
    cq = np.zeros((1, ATTN_HEADS * LANES), np.float32)
    for h in range(ATTN_HEADS):
        r = h % ATTN_REP
        for s in range(N_SPLIT):
            cq[0, h * LANES + QK_CK_LANE + s * ATTN_REP + r] = -1.0
    return jnp.asarray(eq, BF16), jnp.asarray(ek, BF16), jnp.asarray(cq)


def _transpose_rows(x, q):
    if q < LANES:
        x = jnp.concatenate([x, jnp.zeros((LANES - q, x.shape[1]), x.dtype)], axis=0)
    return x.T[:, :q]


def _split3(x):
    hi = x.astype(BF16)
    r1 = x - hi.astype(F32)
    mid = r1.astype(BF16)
    lo = (r1 - mid.astype(F32)).astype(BF16)
    return hi, mid, lo


def _split3_stacked(x, axis):
    return jnp.concatenate([t.astype(F32) for t in _split3(x)], axis=axis).astype(BF16)


def _attn_prep_body(q_ref, k_ref, v_ref, dtf_ref, bf_ref, eq_ref, ek_ref, cq_ref, slab_ref, qp_ref, kp_ref, vpt_ref,
                    kt_ref, vt_ref, carry_ref, *, tiles_per_seq, tl, rel_to_last):
    i = pl.program_id(0)

    @pl.when(i % tiles_per_seq == 0)
    def _():
        carry_ref[...] = jnp.zeros_like(carry_ref)

    lf = _log_sigmoid(dtf_ref[...] + bf_ref[...])
    row = lax.broadcasted_iota(jnp.int32, (tl, tl), 0)
    col = lax.broadcasted_iota(jnp.int32, (tl, tl), 1)
    c = jnp.dot((col <= row).astype(F32), lf, precision=HI, preferred_element_type=F32) + carry_ref[...]
    carry_ref[...] = c[tl - 1:tl, :]
    lane = lax.broadcasted_iota(jnp.int32, (tl, LANES), 1)
    c16 = jnp.where(lane < ATTN_HEADS, pltpu.roll(c, LANES - SSD_HEADS, 1), 0.0)
    slab_ref[...] = jnp.where((lane >= SSD_HEADS) & (lane < SSD_HEADS + ATTN_HEADS), lf, c16)
    if rel_to_last:
        c16 = c16 - c16[tl - 1:tl, :]
    hi, mid, lo = _split3(c16 * LOG2E)
    x = jnp.where(lane < ATTN_HEADS, hi.astype(F32),
                  jnp.where(lane < 2 * ATTN_HEADS, pltpu.roll(mid.astype(F32), ATTN_HEADS, 1),
                            pltpu.roll(lo.astype(F32), 2 * ATTN_HEADS, 1))).astype(BF16)
    q_extra = jnp.dot(x, eq_ref[...], preferred_element_type=F32) + cq_ref[...]
    k_extra = jnp.dot(x, ek_ref[...], preferred_element_type=F32)
    low = lane < HEAD_DIM
    for h in range(ATTN_HEADS):
        qx = q_ref[:, (h // 2) * LANES:(h // 2 + 1) * LANES]
        if h % 2:
            qx = pltpu.roll(qx, HEAD_DIM, 1)
        tile = jnp.where(low, qx * (ATTN_SCALE * LOG2E), q_extra[:, h * LANES:(h + 1) * LANES])
        qp_ref[:, h * LANES:(h + 1) * LANES] = tile.astype(BF16)
    ones_k = ((lane >= QK_ONES_LANE) & (lane < QK_CK_LANE)).astype(F32)
    ones_v = (lane == V_ONES_ROW).astype(F32)
    for g in range(ATTN_KV_HEADS):
        kx = k_ref[:, (g // 2) * LANES:(g // 2 + 1) * LANES]
        vx = v_ref[:, (g // 2) * LANES:(g // 2 + 1) * LANES]
        if g % 2:
            kx = pltpu.roll(kx, HEAD_DIM, 1)
            vx = pltpu.roll(vx, HEAD_DIM, 1)
        kp_ref[:, g * LANES:(g + 1) * LANES] = jnp.where(low, kx, k_extra[:, g * LANES:(g + 1) * LANES] + ones_k
                                                         ).astype(BF16)
        vpt_ref[g * LANES:(g + 1) * LANES, :] = _transpose_rows(jnp.where(low, vx, ones_v), tl).astype(BF16)
    kt_ref[...] = _transpose_rows(k_ref[...], tl)
    vt_ref[...] = _transpose_rows(v_ref[...], tl)


def _attn_prep(u, b_forget, n_tokens, seq_len, rel_to_last=False):
    tl = min(n_tokens, 256)
    assert n_tokens % tl == 0 and seq_len % tl == 0
    bf = jnp.zeros((1, LANES), F32).at[0, SSD_HEADS:SSD_HEADS + ATTN_HEADS].set(b_forget)
    eq, ek, cq = _placement_matrices()
    const = lambda i: (0, 0)
    return pl.pallas_call(
        functools.partial(_attn_prep_body, tiles_per_seq=seq_len // tl, tl=tl, rel_to_last=rel_to_last),
        out_shape=(jax.ShapeDtypeStruct((n_tokens, LANES), F32),
                   jax.ShapeDtypeStruct((n_tokens, ATTN_HEADS * LANES), BF16),
                   jax.ShapeDtypeStruct((n_tokens, ATTN_KV_HEADS * LANES), BF16),
                   jax.ShapeDtypeStruct((ATTN_KV_HEADS * LANES, n_tokens), BF16),
                   jax.ShapeDtypeStruct((KV_DIM, n_tokens), F32),
                   jax.ShapeDtypeStruct((KV_DIM, n_tokens), F32)),
        grid=(n_tokens // tl,),
        in_specs=[
            pl.BlockSpec((tl, ATTN_DIM), lambda i: (i, U_Q // ATTN_DIM)),
            pl.BlockSpec((tl, KV_DIM), lambda i: (i, U_K // KV_DIM)),
            pl.BlockSpec((tl, KV_DIM), lambda i: (i, U_V // KV_DIM)),
            pl.BlockSpec((tl, LANES), lambda i: (i, U_DTF // LANES)),
            pl.BlockSpec((1, LANES), const),
            pl.BlockSpec((LANES, ATTN_HEADS * LANES), const),
            pl.BlockSpec((LANES, ATTN_KV_HEADS * LANES), const),
            pl.BlockSpec((1, ATTN_HEADS * LANES), const),
        ],
        out_specs=(pl.BlockSpec((tl, LANES), lambda i: (i, 0)),
                   pl.BlockSpec((tl, ATTN_HEADS * LANES), lambda i: (i, 0)),
                   pl.BlockSpec((tl, ATTN_KV_HEADS * LANES), lambda i: (i, 0)),
                   pl.BlockSpec((ATTN_KV_HEADS * LANES, tl), lambda i: (0, i)),
                   pl.BlockSpec((KV_DIM, tl), lambda i: (0, i)),
                   pl.BlockSpec((KV_DIM, tl), lambda i: (0, i))),
        scratch_shapes=[pltpu.VMEM((1, LANES), F32)],
        compiler_params=_cparams(("arbitrary",), 40),
        name="attn_prep",
    )(u, u, u, u, bf, eq, ek, cq)


def _silu(x):
    return x * (1.0 / (1.0 + jnp.exp(-x)))


def _softplus(x):
    return jnp.maximum(x, 0.0) + jnp.log1p(jnp.exp(-jnp.abs(x)))


def _ssd_body(xbc_ref, z_ref, dtf_ref, conv0_ref, h0_ref, cw_ref, cb_ref, dtb_ref, alog_ref, dskip_ref,
              gn_ref, e_ref, y_ref, hout_ref, xconv_ref, ht_ref, *, q, n_chunks, conv_done):
    c = pl.program_id(1)

    @pl.when(c == 0)
    def _():
        xconv_ref[0:CONV_PAD, :] = conv0_ref[0]
        for g in range(SSD_GROUPS):
            ht_ref[g] = h0_ref[0, g * GROUP_W:(g + 1) * GROUP_W, :].T

    if conv_done:
        xc = xbc_ref[...]
    else:
        xconv_ref[CONV_PAD:CONV_PAD + q, :] = xbc_ref[...]
        xc = _conv_silu_rows(xconv_ref[...], q, cw_ref, cb_ref)
        xconv_ref[CONV_PAD - (CONV_W - 1):CONV_PAD, :] = xconv_ref[CONV_PAD + q - (CONV_W - 1):CONV_PAD + q, :]

    dt = _softplus(dtf_ref[...] + dtb_ref[...])
    a = -jnp.exp(alog_ref[...])
    row = lax.broadcasted_iota(jnp.int32, (q, q), 0)
    col = lax.broadcasted_iota(jnp.int32, (q, q), 1)
    causal = col <= row
    tri = causal.astype(BF16)
    acum = jnp.dot(jnp.concatenate([tri] * N_SPLIT, axis=1), _split3_stacked(dt * a, 0),
                   preferred_element_type=F32)
    acum_t = _transpose_rows(acum, q)
    a_last = acum[q - 1:q, :]
    fac = jnp.concatenate([jnp.exp(acum), jnp.exp(a_last - acum) * dt, dt], axis=0)
    fac = jnp.dot(_split3_stacked(fac, 1), e_ref[...], preferred_element_type=F32)
    ea_full, wst_full, dt_full = fac[0:q], fac[q:2 * q], fac[2 * q:3 * q]

    def group_bc(g):
        bg = xc[:, D_INNER + g * D_STATE:D_INNER + (g + 1) * D_STATE]
        cg = xc[:, D_INNER + SSD_GROUPS * D_STATE + g * D_STATE:D_INNER + SSD_GROUPS * D_STATE + (g + 1) * D_STATE]
        cg16 = cg.astype(BF16)
        cbm = lax.dot_general(cg16, bg.astype(BF16), (((1,), (1,)), ((), ())), preferred_element_type=F32)
        return bg, cg16, cbm

    nxt = group_bc(0)
    for g in range(SSD_GROUPS):
        gs = slice(g * GROUP_W, (g + 1) * GROUP_W)
        xg = xc[:, gs]
        bg, cg16, cbm = nxt
        if g + 1 < SSD_GROUPS:
            nxt = group_bc(g + 1)
        htg = ht_ref[g]
        yoff = jnp.dot(cg16, htg.astype(BF16), preferred_element_type=F32) * ea_full[:, gs]
        xw = (xg * wst_full[:, gs]).astype(BF16)
        bgt = _transpose_rows(bg, q).astype(BF16)
        st = jnp.dot(bgt, xw, preferred_element_type=F32)
        ht_ref[g] = ea_full[q - 1:q, gs] * htg + st
        xdt = (xg * dt_full[:, gs]).astype(BF16)
        yd = []
        for r in range(HEADS_PER_GROUP):
            h = g * HEADS_PER_GROUP + r
            seg = acum[:, h:h + 1] - acum_t[h:h + 1, :]
            m = cbm * jnp.exp(jnp.where(causal, seg, -jnp.inf))
            yd.append(jnp.dot(m.astype(BF16), xdt[:, r * SSD_HEADDIM:(r + 1) * SSD_HEADDIM],
                              preferred_element_type=F32))
        yd = jnp.concatenate(yd, axis=1)
        yg = yd + yoff + dskip_ref[:, gs] * xg
        yz = yg * _silu(z_ref[:, gs])
        ms = jnp.mean(yz * yz, axis=-1, keepdims=True)
        y_ref[:, gs] = (yz * lax.rsqrt(ms + RMS_EPS) * gn_ref[:, gs]).astype(y_ref.dtype)

    @pl.when(c == n_chunks - 1)
    def _():
        for g in range(SSD_GROUPS):
            hout_ref[0, g * GROUP_W:(g + 1) * GROUP_W, :] = ht_ref[g].T


def _head_expand_matrix():
    e = np.zeros((LANES, D_INNER), np.float32)
    for h in range(SSD_HEADS):
        e[h, h * SSD_HEADDIM:(h + 1) * SSD_HEADDIM] = 1.0
    return jnp.asarray(np.tile(e, (N_SPLIT, 1)), BF16)


def _ssd(u, n_seq, seq_len, q, conv0, h0, p, y_dtype, conv_done=False):
    n_chunks = seq_len // q
    assert seq_len % q == 0 and q % 8 == 0
    conv_ix = (lambda b, c: (b, 0, 0)) if conv0.shape[0] == n_seq and n_seq > 1 else (lambda b, c: (0, 0, 0))
    h_ix = (lambda b, c: (b, 0, 0)) if h0.shape[0] == n_seq and n_seq > 1 else (lambda b, c: (0, 0, 0))
    const2 = lambda b, c: (0, 0)
    return pl.pallas_call(
        functools.partial(_ssd_body, q=q, n_chunks=n_chunks, conv_done=conv_done),
        out_shape=(jax.ShapeDtypeStruct((n_seq * seq_len, D_INNER), y_dtype),
                   jax.ShapeDtypeStruct((n_seq, D_INNER, D_STATE), F32)),
        grid=(n_seq, n_chunks),
        in_specs=[
            pl.BlockSpec((q, CONV_DIM), lambda b, c: (b * n_chunks + c, U_XBC // CONV_DIM)),
            pl.BlockSpec((q, D_INNER), lambda b, c: (b * n_chunks + c, U_Z // D_INNER)),
            pl.BlockSpec((q, LANES), lambda b, c: (b * n_chunks + c, U_DTF // LANES)),
            pl.BlockSpec((1, CONV_PAD, CONV_DIM), conv_ix),
            pl.BlockSpec((1, D_INNER, D_STATE), h_ix),
            pl.BlockSpec((CONV_W, CONV_DIM), const2),
            pl.BlockSpec((1, CONV_DIM), const2),
            pl.BlockSpec((1, LANES), const2),
            pl.BlockSpec((1, LANES), const2),
            pl.BlockSpec((1, D_INNER), const2),
            pl.BlockSpec((1, D_INNER), const2),
            pl.BlockSpec((N_SPLIT * LANES, D_INNER), const2),
        ],
        out_specs=(pl.BlockSpec((q, D_INNER), lambda b, c: (b * n_chunks + c, 0)),
                   pl.BlockSpec((1, D_INNER, D_STATE), lambda b, c: (b, 0, 0))),
        scratch_shapes=[pltpu.VMEM((CONV_PAD + q, CONV_DIM), F32),
                        pltpu.VMEM((SSD_GROUPS, D_STATE, GROUP_W), F32)],
        compiler_params=_cparams(("parallel", "arbitrary"), 48),
        name="ssd_q%d" % q,
    )(u, u, u, conv0, h0, p["conv_w"], p["conv_b"], p["dt_bias"], p["a_log"], p["d_skip"], p["ssd_norm_g"],
      p["head_expand"])


def _ssd_params(conv_w, conv_b, dt_bias, a_log, d_skip, ssd_norm_g):
    pad32 = lambda v: jnp.zeros((1, LANES), F32).at[0, :SSD_HEADS].set(v)
    return dict(conv_w=conv_w, conv_b=conv_b.reshape(1, CONV_DIM), dt_bias=pad32(dt_bias), a_log=pad32(a_log),
                d_skip=jnp.repeat(d_skip, SSD_HEADDIM).reshape(1, D_INNER),
                ssd_norm_g=ssd_norm_g.reshape(1, D_INNER), head_expand=_head_expand_matrix())


def _fox_prompt_body(qi_ref, kj_ref, qp_ref, kp_ref, vpt_ref, kpm_ref, vptm_ref, o_ref, m_ref, acc_ref, *, tq, tk):
    step = pl.program_id(1)
    i = qi_ref[step]
    j = kj_ref[step]
    nt = (((1,), (1,)), ((), ()))

    def scores(h, kp):
        g = h // ATTN_REP
        return lax.dot_general(kp[:, g * LANES:(g + 1) * LANES], qp_ref[:, h * LANES:(h + 1) * LANES], nt,
                               preferred_element_type=F32)

    def attend_all(kp, vpt, mask, ahead=QK_AHEAD):
        pending = [scores(h, kp) for h in range(ahead)]
        for h in range(ATTN_HEADS):
            g = h // ATTN_REP
            s = pending.pop(0)
            if h + ahead < ATTN_HEADS:
                pending.append(scores(h + ahead, kp))
            if mask is not None:
                s = jnp.where(mask, s, -jnp.inf)
            m_prev = m_ref[h, 0:1, :]
            m_new = jnp.maximum(m_prev, jnp.max(s, axis=0, keepdims=True))
            alpha = jnp.exp2(m_prev - m_new)
            p = jnp.exp2(s - m_new).astype(BF16)
            m_ref[h, 0:1, :] = m_new
            pv = jnp.dot(vpt[g * LANES:(g + 1) * LANES, :], p, preferred_element_type=F32)
            acc_ref[h] = alpha * acc_ref[h] + pv

    @pl.when(j == 0)
    def _():
        kpm = kpm_ref[...]
        vptm = vptm_ref[...]
        meta_scores = [scores(h, kpm) for h in range(ATTN_HEADS)]
        for h, s in enumerate(meta_scores):
            g = h // ATTN_REP
            m_new = jnp.max(s, axis=0, keepdims=True)
            m_ref[h, 0:1, :] = m_new
            acc_ref[h] = jnp.dot(vptm[g * LANES:(g + 1) * LANES, :], jnp.exp2(s - m_new).astype(BF16),
                                 preferred_element_type=F32)

    @pl.when(j < i)
    def _():
        attend_all(kp_ref[...], vpt_ref[...], None)

    @pl.when(j == i)
    def _():
        key = lax.broadcasted_iota(jnp.int32, (tk, tq), 0)
        qry = lax.broadcasted_iota(jnp.int32, (tk, tq), 1)
        attend_all(kp_ref[...], vpt_ref[...], key <= qry)
        for h in range(ATTN_HEADS):
            acc = acc_ref[h]
            o_t = acc[0:HEAD_DIM, :] * (1.0 / acc[V_ONES_ROW:V_ONES_ROW + 1, :])
            o_ref[:, h * HEAD_DIM:(h + 1) * HEAD_DIM] = o_t.T.astype(o_ref.dtype)


def _fox_prompt(qp, kp, vpt, kp_meta, vpt_meta, n_seq, seq_len, tq):
    nq = seq_len // tq
    assert seq_len % tq == 0
    qi = np.array([i for i in range(nq) for _ in range(i + 1)], np.int32)
    kj = np.array([j for i in range(nq) for j in range(i + 1)], np.int32)
    n_meta = kp_meta.shape[0]
    qw = ATTN_HEADS * LANES
    kw = ATTN_KV_HEADS * LANES
    grid_spec = pltpu.PrefetchScalarGridSpec(
        num_scalar_prefetch=2,
        grid=(n_seq, len(qi)),
        in_specs=[
            pl.BlockSpec((tq, qw), lambda b, s, qi, kj: (b * nq + qi[s], 0)),
            pl.BlockSpec((tq, kw), lambda b, s, qi, kj: (b * nq + kj[s], 0)),
            pl.BlockSpec((kw, tq), lambda b, s, qi, kj: (0, b * nq + kj[s])),
            pl.BlockSpec((n_meta, kw), lambda b, s, qi, kj: (0, 0)),
            pl.BlockSpec((kw, n_meta), lambda b, s, qi, kj: (0, 0)),
        ],
        out_specs=pl.BlockSpec((tq, ATTN_DIM), lambda b, s, qi, kj: (b * nq + qi[s], 0)),
        scratch_shapes=[pltpu.VMEM((ATTN_HEADS, 8, tq), F32),
                        pltpu.VMEM((ATTN_HEADS, LANES, tq), F32)],
    )
    return pl.pallas_call(
        functools.partial(_fox_prompt_body, tq=tq, tk=tq),
        out_shape=jax.ShapeDtypeStruct((n_seq * seq_len, ATTN_DIM), BF16),
        grid_spec=grid_spec,
        compiler_params=_cparams(("parallel", "arbitrary"), 48),
        name="fox_prompt",
    )(jnp.asarray(qi), jnp.asarray(kj), qp, kp, vpt, kp_meta, vpt_meta)


def _fox_sample_body(pt_ref, qaug_ref, kn_ref, vn_ref, dtf_ref, bf_ref, et_ref, suf_ref, ck_hbm, cv_hbm, clf_hbm,
                     o_ref, lfo_ref, kbuf, vbuf, lfbuf, sem, m_ref, l_ref, acc_ref, carry_ref, cnrow_ref,
                     *, npp, n_steps, n_pages, n_seq, s_new):
    seq = pl.program_id(0)
    step = pl.program_id(1)
    total = n_seq * n_steps
    t = seq * n_steps + step
    slot = t % SAMPLE_DMA_SLOTS
    rows = ATTN_HEADS * s_new
    qaug = qaug_ref[...]
    nt = (((1,), (1,)), ((), ()))

    def page_copies(t_i, s):
        seq_i = t_i // n_steps
        step_i = t_i - seq_i * n_steps
        first = seq_i * n_pages + n_pages - (step_i + 1) * npp
        out = []
        for i in range(npp):
            page = pt_ref[first + i]
            out.append(pltpu.make_async_copy(ck_hbm.at[page], kbuf.at[s, i], sem.at[s]))
            out.append(pltpu.make_async_copy(cv_hbm.at[page], vbuf.at[s, i], sem.at[s]))
            out.append(pltpu.make_async_copy(clf_hbm.at[page], lfbuf.at[s, i], sem.at[s]))
        return out

    @pl.when(t == 0)
    def _():
        for t_i in range(SAMPLE_DMA_SLOTS - 1):
            for c in page_copies(t_i, t_i):
                c.start()

    @pl.when(step == 0)
    def _():
        lane = lax.broadcasted_iota(jnp.int32, (s_new, LANES), 1)
        lfn = _log_sigmoid(dtf_ref[...] + bf_ref[...])
        lfo_ref[...] = jnp.where((lane >= SSD_HEADS) & (lane < SSD_HEADS + ATTN_HEADS), lfn, 0.0)
        lf16 = lfn[:, SSD_HEADS:SSD_HEADS + ATTN_HEADS]
        tri = (lax.broadcasted_iota(jnp.int32, (s_new, s_new), 1)
               <= lax.broadcasted_iota(jnp.int32, (s_new, s_new), 0)).astype(F32)
        cn = jnp.dot(tri, lf16, precision=HI, preferred_element_type=F32) * LOG2E
        cne = lax.dot_general(et_ref[...], cn, nt, precision=HI, preferred_element_type=F32)
        trow = lax.broadcasted_iota(jnp.int32, (rows, s_new), 0) % s_new
        tcol = lax.broadcasted_iota(jnp.int32, (rows, s_new), 1)
        cn_row = jnp.sum(jnp.where(tcol == trow, cne, 0.0), axis=-1, keepdims=True)
        cnrow_ref[...] = jnp.broadcast_to(cn_row, cnrow_ref.shape)
        ss = lax.dot_general(qaug[:, 0:KV_DIM], kn_ref[...].astype(BF16), nt, preferred_element_type=F32)
        ss = jnp.where(tcol <= trow, ss + cn_row - cne, -jnp.inf)
        m = jnp.max(ss, axis=-1, keepdims=True)
        p = jnp.exp2(ss - m)
        m_ref[...] = jnp.broadcast_to(m, m_ref.shape)
        l_ref[...] = jnp.broadcast_to(jnp.sum(p, axis=-1, keepdims=True), l_ref.shape)
        acc_ref[...] = jnp.dot(p.astype(BF16), vn_ref[...].astype(BF16), preferred_element_type=F32)
        carry_ref[...] = jnp.zeros(carry_ref.shape, F32)

    for c in page_copies(t, slot):
        c.wait()
    lf_all = jnp.concatenate([lfbuf[slot, i] for i in range(npp)], axis=0)
    in_page = jnp.dot(jnp.concatenate(_split3(lf_all), axis=1), suf_ref[...], preferred_element_type=F32)
    page_total = jnp.sum(lf_all, axis=1, keepdims=True)
    carry = carry_ref[:, 0:1]
    scores = [None] * npp
    for i in reversed(range(npp)):
        hs = slice(i * ATTN_HEADS, (i + 1) * ATTN_HEADS)
        r_hi, r_mid, r_lo = _split3((in_page[hs] + carry) * LOG2E)
        carry = carry + page_total[hs]
        kt = kbuf[slot, i].reshape(KV_DIM, PAGE_SIZE).astype(BF16)
        k_aug = jnp.concatenate([kt, r_hi, r_mid, r_lo], axis=0)
        scores[i] = jnp.dot(qaug, k_aug, preferred_element_type=F32)
    carry_ref[...] = jnp.broadcast_to(carry, carry_ref.shape)
    s_all = jnp.concatenate(scores, axis=1) + cnrow_ref[:, 0:1]
    vt_all = jnp.concatenate([vbuf[slot, i].reshape(KV_DIM, PAGE_SIZE).astype(BF16) for i in range(npp)], axis=1)
    m_prev = m_ref[...]
    m_new = jnp.maximum(m_prev, jnp.max(s_all, axis=-1, keepdims=True))
    alpha = jnp.exp2(m_prev - m_new)
    p = jnp.exp2(s_all - m_new[:, 0:1])
    l_ref[...] = alpha * l_ref[...] + jnp.sum(p, axis=-1, keepdims=True)
    m_ref[...] = m_new
    pv = lax.dot_general(p.astype(BF16), vt_all, nt, preferred_element_type=F32)
    acc_ref[...] = jnp.concatenate([alpha, alpha], axis=1) * acc_ref[...] + pv
    ahead = t + (SAMPLE_DMA_SLOTS - 1)
    for c in page_copies(jnp.where(ahead < total, ahead, t), ahead % SAMPLE_DMA_SLOTS):
        c.start()

    @pl.when(step == n_steps - 1)
    def _():
        inv = 1.0 / l_ref[...]
        o_ref[...] = acc_ref[...] * jnp.concatenate([inv, inv], axis=1)

    @pl.when(t == total - 1)
    def _():
        for back in range(SAMPLE_DMA_SLOTS - 1):
            for c in page_copies(t, (t + 1 + back) % SAMPLE_DMA_SLOTS):
                c.wait()


SAMPLE_PAGES_PER_STEP = 16
SAMPLE_DMA_SLOTS = 3


def _fox_sample(u_s, cache_k, cache_v, cache_logf, page_table, b_forget, n_seq, s_new, npp):
    n_pages = page_table.shape[1]
    assert n_pages % npp == 0 and ATTN_HEADS * s_new == LANES
    n_steps = n_pages // npp
    n_pool = cache_k.shape[0]
    rows = ATTN_HEADS * s_new
    ck = jnp.transpose(cache_k, (0, 2, 3, 1))
    cv = jnp.transpose(cache_v, (0, 2, 3, 1))
    clf = jnp.transpose(cache_logf, (0, 2, 1))
    q = u_s[:, U_Q:U_Q + ATTN_DIM].reshape(n_seq, s_new, ATTN_KV_HEADS, ATTN_REP, HEAD_DIM)
    qbd = jnp.einsum("btgrd,gh->bgrthd", q, jnp.eye(ATTN_KV_HEADS, dtype=F32)).reshape(n_seq, rows, KV_DIM)
    et = np.zeros((rows, ATTN_HEADS), np.float32)
    et[np.arange(rows), np.arange(rows) // s_new] = 1.0
    et_b = jnp.broadcast_to(jnp.asarray(et), (n_seq, rows, ATTN_HEADS))
    qaug = jnp.concatenate([qbd * (ATTN_SCALE * LOG2E)] + [et_b] * N_SPLIT, axis=-1).astype(BF16)
    aug_w = KV_DIM + N_SPLIT * ATTN_HEADS
    suf = np.tile(np.triu(np.ones((PAGE_SIZE, PAGE_SIZE), np.float32), 1).T, (N_SPLIT, 1))
    bf = jnp.zeros((1, LANES), F32).at[0, SSD_HEADS:SSD_HEADS + ATTN_HEADS].set(b_forget)

    const2 = lambda b, s, pt: (0, 0)
    hbm = pl.BlockSpec(memory_space=pl.ANY)
    in_specs = [
        pl.BlockSpec((None, rows, aug_w), lambda b, s, pt: (b, 0, 0)),
        pl.BlockSpec((s_new, KV_DIM), lambda b, s, pt: (b, U_K // KV_DIM)),
        pl.BlockSpec((s_new, KV_DIM), lambda b, s, pt: (b, U_V // KV_DIM)),
        pl.BlockSpec((s_new, LANES), lambda b, s, pt: (b, U_DTF // LANES)),
        pl.BlockSpec((1, LANES), const2),
        pl.BlockSpec((rows, ATTN_HEADS), const2),
        pl.BlockSpec((N_SPLIT * PAGE_SIZE, PAGE_SIZE), const2),
        hbm, hbm, hbm,
    ]
    grid_spec = pltpu.PrefetchScalarGridSpec(
        num_scalar_prefetch=1,
        grid=(n_seq, n_steps),
        in_specs=in_specs,
        out_specs=(pl.BlockSpec((None, rows, KV_DIM), lambda b, s, pt: (b, 0, 0)),
                   pl.BlockSpec((s_new, LANES), lambda b, s, pt: (b, 0))),
        scratch_shapes=[pltpu.VMEM((SAMPLE_DMA_SLOTS, npp, ATTN_KV_HEADS, HEAD_DIM, PAGE_SIZE), F32),
                        pltpu.VMEM((SAMPLE_DMA_SLOTS, npp, ATTN_KV_HEADS, HEAD_DIM, PAGE_SIZE), F32),
                        pltpu.VMEM((SAMPLE_DMA_SLOTS, npp, ATTN_HEADS, PAGE_SIZE), F32),
                        pltpu.SemaphoreType.DMA((SAMPLE_DMA_SLOTS,)),
                        pltpu.VMEM((rows, LANES), F32), pltpu.VMEM((rows, LANES), F32),
                        pltpu.VMEM((rows, KV_DIM), F32), pltpu.VMEM((ATTN_HEADS, LANES), F32),
                        pltpu.VMEM((rows, LANES), F32)],
    )
    o_raw, lf_slab = pl.pallas_call(
        functools.partial(_fox_sample_body, npp=npp, n_steps=n_steps, n_pages=n_pages, n_seq=n_seq, s_new=s_new),
        out_shape=(jax.ShapeDtypeStruct((n_seq, rows, KV_DIM), F32),
                   jax.ShapeDtypeStruct((n_seq * s_new, LANES), F32)),
        grid_spec=grid_spec,
        compiler_params=_cparams(("arbitrary", "arbitrary"), 48),
        name="fox_sample",
    )(page_table.reshape(-1), qaug, u_s, u_s, u_s, bf, jnp.asarray(et), jnp.asarray(suf, BF16), ck, cv, clf)
    o = o_raw.reshape(n_seq, ATTN_KV_HEADS, ATTN_REP, s_new, ATTN_KV_HEADS, HEAD_DIM)
    o = jnp.einsum("bgrtgd->btgrd", o).reshape(n_seq * s_new, ATTN_DIM)
    return o, lf_slab


ROUTE_E0, ROUTE_E1, ROUTE_W0, ROUTE_W1 = 0, 1, 2, 3
ROUTER_EXPERT_LANE0 = N_EXPERT_GROUPS
_BIG_LANE = 4 * LANES


def _sigmoid(x):
    return 1.0 / (1.0 + jnp.exp(-x))


def _mix_body(y_ref, o_ref, gs_ref, ga_ref, h_ref, wssd_ref, wattn_ref, wout_ref, g2_ref, wr_ref, br_ref,
              h1_ref, hn_ref, route_ref):
    ys = jnp.dot(y_ref[...].astype(BF16), wssd_ref[...], preferred_element_type=F32)
    oa = jnp.dot(o_ref[...].astype(BF16), wattn_ref[...], preferred_element_type=F32)
    mix = _sigmoid(gs_ref[...]) * ys + _sigmoid(ga_ref[...]) * oa
    h1 = h_ref[...] + jnp.dot(mix.astype(BF16), wout_ref[...], preferred_element_type=F32)
    h1_ref[...] = h1
    ms = jnp.mean(h1 * h1, axis=-1, keepdims=True)
    hn = (h1 * lax.rsqrt(ms + RMS_EPS)) * g2_ref[...]
    hn_ref[...] = hn
    hn_hi = hn.astype(BF16)
    hn_lo = (hn - hn_hi.astype(F32)).astype(BF16)
    logits = jnp.dot(jnp.concatenate([hn_hi, hn_hi, hn_lo], axis=1), wr_ref[...], preferred_element_type=F32)
    route_ref[...] = _route(logits + br_ref[...])


def _route(logits):
    lane = lax.broadcasted_iota(jnp.int32, logits.shape, 1)
    gl = jnp.where(lane < N_EXPERT_GROUPS, logits, -jnp.inf)
    gmax = jnp.max(gl, axis=-1, keepdims=True)
    gsel = jnp.min(jnp.where(gl == gmax, lane, _BIG_LANE), axis=-1, keepdims=True)
    wgrp = 1.0 / jnp.sum(jnp.exp(gl - gmax), axis=-1, keepdims=True)
    elane = lane - ROUTER_EXPERT_LANE0
    in_group = (elane >= gsel * EXPERTS_PER_GROUP) & (elane < (gsel + 1) * EXPERTS_PER_GROUP)
    el = jnp.where(in_group, logits, -jnp.inf)
    t1 = jnp.max(el, axis=-1, keepdims=True)
    i1 = jnp.min(jnp.where(el == t1, lane, _BIG_LANE), axis=-1, keepdims=True)
    el2 = jnp.where(lane == i1, -jnp.inf, el)
    t2 = jnp.max(el2, axis=-1, keepdims=True)
    i2 = jnp.min(jnp.where(el2 == t2, lane, _BIG_LANE), axis=-1, keepdims=True)
    e21 = jnp.exp(t2 - t1)
    w1 = wgrp / (1.0 + e21)
    w2 = w1 * e21
    return jnp.where(lane == ROUTE_E0, (i1 - ROUTER_EXPERT_LANE0).astype(F32),
                     jnp.where(lane == ROUTE_E1, (i2 - ROUTER_EXPERT_LANE0).astype(F32),
                               jnp.where(lane == ROUTE_W0, w1, jnp.where(lane == ROUTE_W1, w2, 0.0))))


def _mix(y, o, u, h, p, tm):
    t = h.shape[0]
    assert t % tm == 0
    row = lambda i: (i, 0)
    const = lambda i: (0, 0)
    return pl.pallas_call(
        _mix_body,
        out_shape=(jax.ShapeDtypeStruct((t, D_MODEL), F32), jax.ShapeDtypeStruct((t, D_MODEL), F32),
                   jax.ShapeDtypeStruct((t, LANES), F32)),
        grid=(t // tm,),
        in_specs=[
            pl.BlockSpec((tm, D_INNER), row),
            pl.BlockSpec((tm, ATTN_DIM), row),
            pl.BlockSpec((tm, D_MODEL), lambda i: (i, U_GS // D_MODEL)),
            pl.BlockSpec((tm, D_MODEL), lambda i: (i, U_GA // D_MODEL)),
            pl.BlockSpec((tm, D_MODEL), row),
            pl.BlockSpec((D_INNER, D_MODEL), const),
            pl.BlockSpec((ATTN_DIM, D_MODEL), const),
            pl.BlockSpec((D_MODEL, D_MODEL), const),
            pl.BlockSpec((1, D_MODEL), const),
            pl.BlockSpec((3 * D_MODEL, LANES), const),
            pl.BlockSpec((1, LANES), const),
        ],
        out_specs=(pl.BlockSpec((tm, D_MODEL), row), pl.BlockSpec((tm, D_MODEL), row),
                   pl.BlockSpec((tm, LANES), row)),
        compiler_params=_cparams(("parallel",), 56),
        name="mix_route",
    )(y, o, u, u, h, p["w_ssd_br"], p["w_attn_br"], p["w_out"], p["norm2_g"], p["w_router"], p["b_router"])


def _mix_params(w_ssd_br, w_attn_br, w_out, norm2_g, w_rg, b_rg, w_re, b_re):
    n_r = N_EXPERT_GROUPS + N_EXPERTS
    w_router = jnp.concatenate([w_rg, w_re, jnp.zeros((D_MODEL, LANES - n_r), F32)], axis=1)
    w_hi = w_router.astype(BF16)
    w_lo = (w_router - w_hi.astype(F32)).astype(BF16)
    w_router = jnp.concatenate([w_hi, w_lo, w_hi], axis=0)
    b_router = jnp.concatenate([b_rg, b_re, jnp.zeros((LANES - n_r,), F32)]).reshape(1, LANES)
    return dict(w_ssd_br=w_ssd_br.astype(BF16), w_attn_br=w_attn_br.astype(BF16), w_out=w_out.astype(BF16),
                norm2_g=norm2_g.reshape(1, D_MODEL), w_router=w_router, b_router=b_router)


def _moe_rank_body(route_ref, pos_ref, cnt_ref, carry_ref, *, tm):
    @pl.when(pl.program_id(0) == 0)
    def _():
        carry_ref[...] = jnp.zeros(carry_ref.shape, F32)

    route = route_ref[...]
    lane = lax.broadcasted_iota(jnp.int32, (tm, LANES), 1).astype(F32)
    hit0 = lane == route[:, ROUTE_E0:ROUTE_E0 + 1]
    hit1 = lane == route[:, ROUTE_E1:ROUTE_E1 + 1]
    onehot = hit0.astype(F32) + hit1.astype(F32)
    before = (lax.broadcasted_iota(jnp.int32, (tm, tm), 1) < lax.broadcasted_iota(jnp.int32, (tm, tm), 0))
    c = jnp.dot(before.astype(BF16), onehot.astype(BF16), preferred_element_type=F32) + carry_ref[...]
    pos0 = jnp.sum(jnp.where(hit0, c, 0.0), axis=-1, keepdims=True)
    pos1 = jnp.sum(jnp.where(hit1, c, 0.0), axis=-1, keepdims=True)
    pos_ref[...] = jnp.where(lane == 0.0, pos0, jnp.where(lane == 1.0, pos1, 0.0))
    total = carry_ref[...] + jnp.sum(onehot, axis=0, keepdims=True)
    carry_ref[...] = total
    cnt_ref[...] = total


def _moe_rank(route, tm):
    t = route.shape[0]
    assert t % tm == 0
    return pl.pallas_call(
        functools.partial(_moe_rank_body, tm=tm),
        out_shape=(jax.ShapeDtypeStruct((t, LANES), F32), jax.ShapeDtypeStruct((1, LANES), F32)),
        grid=(t // tm,),
        in_specs=[pl.BlockSpec((tm, LANES), lambda i: (i, 0))],
        out_specs=(pl.BlockSpec((tm, LANES), lambda i: (i, 0)), pl.BlockSpec((1, LANES), lambda i: (0, 0))),
        scratch_shapes=[pltpu.VMEM((1, LANES), F32)],
        compiler_params=_cparams(("arbitrary",)),
        name="moe_rank",
    )(route)


def _experts_body(be_ref, nu_ref, x_ref, wg_ref, wu_ref, wd_ref, y_ref, wg16_ref, wu16_ref, wd16_ref):
    i = pl.program_id(0)
    prev = be_ref[jnp.maximum(i - 1, 0)]

    @pl.when((i == 0) | (be_ref[i] != prev))
    def _():
        wg16_ref[...] = wg_ref[...].astype(BF16)
        wu16_ref[...] = wu_ref[...].astype(BF16)
        wd16_ref[...] = wd_ref[...].astype(BF16)

    @pl.when(i < nu_ref[0])
    def _():
        sub = x_ref.shape[0] // 2
        rows = [slice(0, sub), slice(sub, 2 * sub)]
        gate_up = []
        for r in rows:
            x16 = x_ref[r, :].astype(BF16)
            gate_up.append((jnp.dot(x16, wg16_ref[...], preferred_element_type=F32),
                            jnp.dot(x16, wu16_ref[...], preferred_element_type=F32)))
        for r, (gt, up) in zip(rows, gate_up):
            y_ref[r, :] = jnp.dot((_silu(gt) * up).astype(BF16), wd16_ref[...], preferred_element_type=F32)

    @pl.when(i >= nu_ref[0])
    def _():
        y_ref[...] = jnp.zeros(y_ref.shape, F32)


def _experts(xs, block_e, n_used, w_gate, w_up, w_down, blk):
    rows = xs.shape[0]
    n_blocks = rows // blk
    grid_spec = pltpu.PrefetchScalarGridSpec(
        num_scalar_prefetch=2,
        grid=(n_blocks,),
        in_specs=[
            pl.BlockSpec((blk, D_MODEL), lambda i, be, nu: (jnp.minimum(i, nu[0] - 1), 0)),
            pl.BlockSpec((None, D_MODEL, D_EXPERT), lambda i, be, nu: (be[i], 0, 0)),
            pl.BlockSpec((None, D_MODEL, D_EXPERT), lambda i, be, nu: (be[i], 0, 0)),
            pl.BlockSpec((None, D_EXPERT, D_MODEL), lambda i, be, nu: (be[i], 0, 0)),
        ],
        out_specs=pl.BlockSpec((blk, D_MODEL), lambda i, be, nu: (i, 0)),
        scratch_shapes=[pltpu.VMEM((D_MODEL, D_EXPERT), BF16), pltpu.VMEM((D_MODEL, D_EXPERT), BF16),
                        pltpu.VMEM((D_EXPERT, D_MODEL), BF16)],
    )
    return pl.pallas_call(
        _experts_body,
        out_shape=jax.ShapeDtypeStruct((rows, D_MODEL), F32),
        grid_spec=grid_spec,
        compiler_params=_cparams(("arbitrary",), 48),
        name="moe_experts",
    )(block_e, n_used, xs, w_gate, w_up, w_down)


MOE_ROWS_PER_BLOCK = 256
MOE_TOKEN_TILE = 256
DMA_ISSUE_UNROLL = 8


def _row_copy(src, src_row, dst, dst_row, sem):
    return pltpu.make_async_copy(src.at[pl.ds(src_row, 1)], dst.at[pl.ds(dst_row, 1)], sem)


def _dispatch_body(zb_ref, nu_ref, d0_ref, d1_ref, hn_ref, xs_ref, zero_ref, sem, zsem, *, tm, blk, n_blocks):
    @pl.when(pl.program_id(0) == 0)
    def _():
        zero_ref[...] = jnp.zeros(zero_ref.shape, F32)

        def zero_block(b):
            return pltpu.make_async_copy(zero_ref, xs_ref.at[pl.ds(b * blk, blk)], zsem)

        for e in range(N_EXPERTS):
            zero_block(zb_ref[e]).start()

        def start_tail(b, c):
            zero_block(b).start()
            return c

        def wait_tail(b, c):
            zero_block(b).wait()
            return c

        lax.fori_loop(nu_ref[0], n_blocks, start_tail, 0)
        for e in range(N_EXPERTS):
            zero_block(0).wait()
        lax.fori_loop(nu_ref[0], n_blocks, wait_tail, 0)

    def issue(r, c):
        _row_copy(hn_ref, r, xs_ref, d0_ref[r], sem).start()
        _row_copy(hn_ref, r, xs_ref, d1_ref[r], sem).start()
        return c

    lax.fori_loop(0, tm, issue, 0, unroll=DMA_ISSUE_UNROLL)

    def drain(r, c):
        _row_copy(hn_ref, 0, xs_ref, 0, sem).wait()
        _row_copy(hn_ref, 0, xs_ref, 0, sem).wait()
        return c

    lax.fori_loop(0, tm, drain, 0, unroll=DMA_ISSUE_UNROLL)


def _dispatch(hn, d0, d1, zero_blocks, n_used, n_blocks, blk, tm):
    t = hn.shape[0]
    assert t % tm == 0
    smem_tile = lambda: pl.BlockSpec((tm,), lambda i, zb, nu: (i,), memory_space=pltpu.SMEM)
    grid_spec = pltpu.PrefetchScalarGridSpec(
        num_scalar_prefetch=2,
        grid=(t // tm,),
        in_specs=[smem_tile(), smem_tile(), pl.BlockSpec((tm, D_MODEL), lambda i, zb, nu: (i, 0))],
        out_specs=pl.BlockSpec(memory_space=pl.ANY),
        scratch_shapes=[pltpu.VMEM((blk, D_MODEL), F32), pltpu.SemaphoreType.DMA, pltpu.SemaphoreType.DMA],
    )
    return pl.pallas_call(
        functools.partial(_dispatch_body, tm=tm, blk=blk, n_blocks=n_blocks),
        out_shape=jax.ShapeDtypeStruct((n_blocks * blk, D_MODEL), F32),
        grid_spec=grid_spec,
        compiler_params=_cparams(("arbitrary",)),
        name="moe_dispatch",
    )(zero_blocks, n_used, d0, d1, hn)


def _combine_body(d0_ref, d1_ref, d0n_ref, d1n_ref, h1_ref, route_ref, g_ref, ys_ref, o_ref, buf, sem, *, tm, n_tiles):
    i = pl.program_id(0)
    slot = i % 2

    def gather(da_ref, db_ref, s):
        def issue(r, c):
            _row_copy(ys_ref, da_ref[r], buf.at[s, 0], r, sem.at[s]).start()
            _row_copy(ys_ref, db_ref[r], buf.at[s, 1], r, sem.at[s]).start()
            return c
        lax.fori_loop(0, tm, issue, 0, unroll=DMA_ISSUE_UNROLL)

    @pl.when(i == 0)
    def _():
        gather(d0_ref, d1_ref, 0)

    @pl.when(i + 1 < n_tiles)
    def _():
        gather(d0n_ref, d1n_ref, 1 - slot)

    def drain(r, c):
        _row_copy(ys_ref, 0, buf.at[slot, 0], 0, sem.at[slot]).wait()
        _row_copy(ys_ref, 0, buf.at[slot, 1], 0, sem.at[slot]).wait()
        return c

    lax.fori_loop(0, tm, drain, 0, unroll=DMA_ISSUE_UNROLL)
    route = route_ref[...]
    h = (h1_ref[...] + route[:, ROUTE_W0:ROUTE_W0 + 1] * buf[slot, 0]
         + route[:, ROUTE_W1:ROUTE_W1 + 1] * buf[slot, 1])
    ms = jnp.mean(h * h, axis=-1, keepdims=True)
    o_ref[...] = (h * lax.rsqrt(ms + RMS_EPS)) * g_ref[...]


def _combine(h1, ys, d0, d1, route, g, tm):
    t = h1.shape[0]
    n_tiles = t // tm
    row = lambda i: (i, 0)
    cur = lambda: pl.BlockSpec((tm,), lambda i: (i,), memory_space=pltpu.SMEM)
    nxt = lambda: pl.BlockSpec((tm,), lambda i: (jnp.minimum(i + 1, n_tiles - 1),), memory_space=pltpu.SMEM)
    return pl.pallas_call(
        functools.partial(_combine_body, tm=tm, n_tiles=n_tiles),
        out_shape=jax.ShapeDtypeStruct((t, D_MODEL), F32),
        grid=(n_tiles,),
        in_specs=[cur(), cur(), nxt(), nxt(), pl.BlockSpec((tm, D_MODEL), row), pl.BlockSpec((tm, LANES), row),
                  pl.BlockSpec((1, D_MODEL), lambda i: (0, 0)), pl.BlockSpec(memory_space=pl.ANY)],
        out_specs=pl.BlockSpec((tm, D_MODEL), row),
        scratch_shapes=[pltpu.VMEM((2, 2, tm, D_MODEL), F32), pltpu.SemaphoreType.DMA((2,))],
        compiler_params=_cparams(("arbitrary",)),
        name="moe_combine_norm",
    )(d0, d1, d0, d1, h1, route, g.reshape(1, D_MODEL), ys)


def _moe(hn, h1, route, w_gate, w_up, w_down, final_g):
    t = hn.shape[0]
    blk = MOE_ROWS_PER_BLOCK
    pos, cnt = _moe_rank(route, MOE_TOKEN_TILE)
    counts = cnt[0, :N_EXPERTS].astype(jnp.int32)
    padded = (counts + blk - 1) // blk * blk
    ends = jnp.cumsum(padded)
    starts = ends - padded
    expert_ids = jnp.arange(N_EXPERTS, dtype=jnp.int32)

    def dest(e_lane, p_lane):
        e = route[:, e_lane].astype(jnp.int32)
        start = jnp.sum(jnp.where(e[:, None] == expert_ids[None, :], starts[None, :], 0), axis=1)
        return start + pos[:, p_lane].astype(jnp.int32)

    d0 = dest(ROUTE_E0, 0)
    d1 = dest(ROUTE_E1, 1)
    n_blocks = (2 * t + N_EXPERTS * (blk - 1) + blk - 1) // blk
    first_row = jnp.arange(n_blocks, dtype=jnp.int32) * blk
    block_e = jnp.minimum(jnp.sum((ends[None, :] <= first_row[:, None]).astype(jnp.int32), axis=1), N_EXPERTS - 1)
    n_used = (ends[-1] // blk).astype(jnp.int32).reshape(1)
    zero_blocks = jnp.clip((ends - 1) // blk, 0, n_blocks - 1).astype(jnp.int32)
    xs = _dispatch(hn, d0, d1, zero_blocks, n_used, n_blocks, blk, MOE_TOKEN_TILE)
    ys = _experts(xs, block_e, n_used, w_gate, w_up, w_down, blk)
    return _combine(h1, ys, d0, d1, route, final_g, MOE_TOKEN_TILE)


def _conv_history(rows):
    n = rows.shape[0]
    return jnp.concatenate([jnp.zeros((n, CONV_PAD - (CONV_W - 1), CONV_DIM), F32), rows], axis=1)


def _ucols(u, start, width):
    return u[:, start:start + width]


def kernel(x_prompt, x_sample, cache_k, cache_v, cache_logf, state_ssm, state_conv, page_table, meta_tokens, norm1_g, w_in, conv_w, conv_b, dt_bias, a_log, d_skip, ssd_norm_g, b_forget, w_ssd_br, w_attn_br, w_out, norm2_g, w_router_group, b_router_group, w_router_expert, b_router_expert, w_exp_gate, w_exp_up, w_exp_down, final_norm_g):
    nb, sl, _ = x_prompt.shape
    sb, ss, _ = x_sample.shape
    assert w_in.shape[0] == 1, "single-layer step"
    ly = 0
    xp = x_prompt.reshape(nb * sl, D_MODEL)
    xs = x_sample.reshape(sb * ss, D_MODEL)

    wp = _pack_w_in(w_in[ly])
    u_s = _inproj(xs, norm1_g[ly], wp)
    u_m = _inproj(meta_tokens, norm1_g[ly], wp)
    hist_meta = _conv_history(_ucols(u_m, U_XBC, CONV_DIM)[None, N_META - (CONV_W - 1):])
    u_p, conv_tail_p = _inproj_conv(xp, norm1_g[ly], wp, hist_meta, conv_w[ly], conv_b[ly], sl)

    sp = _ssd_params(conv_w[ly], conv_b[ly], dt_bias[ly], a_log[ly], d_skip[ly], ssd_norm_g[ly])
    zero_hist = jnp.zeros((1, CONV_PAD, CONV_DIM), F32)
    zero_state = jnp.zeros((1, D_INNER, D_STATE), F32)
    _, h_meta = _ssd(u_m, 1, N_META, N_META, zero_hist, zero_state, sp, BF16)
    y_p, ssm_p = _ssd(u_p, nb, sl, SSD_CHUNK, hist_meta, h_meta, sp, BF16, conv_done=True)
    y_s, ssm_s = _ssd(u_s, sb, ss, ss, _conv_history(state_conv[ly]),
                      state_ssm[ly].reshape(sb, D_INNER, D_STATE), sp, F32)

    slab_m, _, kp_m, vpt_m, kt_m, vt_m = _attn_prep(u_m, b_forget[ly], N_META, N_META, rel_to_last=True)
    slab_p, qp_p, kp_p, vpt_p, kt_p, vt_p = _attn_prep(u_p, b_forget[ly], nb * sl, sl)
    o_p = _fox_prompt(qp_p, kp_p, vpt_p, kp_m, vpt_m, nb, sl, ATTN_TQ)
    o_s, slab_s = _fox_sample(u_s, cache_k[ly], cache_v[ly], cache_logf[ly], page_table, b_forget[ly], sb, ss,
                              SAMPLE_PAGES_PER_STEP)

    mp = _mix_params(w_ssd_br[ly], w_attn_br[ly], w_out[ly], norm2_g[ly], w_router_group[ly], b_router_group[ly],
                     w_router_expert[ly], b_router_expert[ly])
    h1_p, hn_p, route_p = _mix(y_p, o_p, u_p, xp, mp, 256)
    h1_s, hn_s, route_s = _mix(y_s, o_s, u_s, xs, mp, 256)
    out_p = _moe(hn_p, h1_p, route_p, w_exp_gate[ly], w_exp_up[ly], w_exp_down[ly], final_norm_g)
    out_s = _moe(hn_s, h1_s, route_s, w_exp_gate[ly], w_exp_up[ly], w_exp_down[ly], final_norm_g)

    def with_meta(meta_rows, rows, width):
        m = jnp.broadcast_to(meta_rows[None], (nb, N_META, width))
        return jnp.concatenate([m, rows.reshape(nb, sl, width)], axis=1)[None]

    def kv_with_meta(t_meta, t_rows):
        m = jnp.broadcast_to(t_meta.reshape(ATTN_KV_HEADS, HEAD_DIM, 1, N_META), (ATTN_KV_HEADS, HEAD_DIM, nb, N_META))
        full = jnp.concatenate([m, t_rows.reshape(ATTN_KV_HEADS, HEAD_DIM, nb, sl)], axis=3)
        return jnp.transpose(full, (2, 3, 0, 1))[None]

    lf_m = slab_m[:, SSD_HEADS:SSD_HEADS + ATTN_HEADS]
    lf_p = slab_p[:, SSD_HEADS:SSD_HEADS + ATTN_HEADS]
    lf_s = slab_s[:, SSD_HEADS:SSD_HEADS + ATTN_HEADS]
    kv_shape_s = (1, sb, ss, ATTN_KV_HEADS, HEAD_DIM)
    state_shape = (SSD_HEADS, SSD_HEADDIM, D_STATE)
    tail = CONV_W - 1
    return (
        out_p.reshape(nb, sl, D_MODEL),
        out_s.reshape(sb, ss, D_MODEL),
        kv_with_meta(kt_m, kt_p),
        kv_with_meta(vt_m, vt_p),
        with_meta(lf_m, lf_p, ATTN_HEADS),
        ssm_p.reshape((1, nb) + state_shape),
        conv_tail_p[None, :, CONV_PAD - tail:],
        _ucols(u_s, U_K, KV_DIM).reshape(kv_shape_s),
        _ucols(u_s, U_V, KV_DIM).reshape(kv_shape_s),
        lf_s.reshape(1, sb, ss, ATTN_HEADS),
        ssm_s.reshape((1, sb) + state_shape),
        u_s.reshape(sb, ss, U_COLS)[None, :, ss - tail:, U_XBC:U_XBC + CONV_DIM],
    )
```

```python
import functools

import numpy as np
import jax
import jax.numpy as jnp
from jax import lax
from jax.experimental import pallas as pl
from jax.experimental.pallas import tpu as pltpu

F32 = jnp.float32
BF16 = jnp.bfloat16
HI = lax.Precision.HIGHEST

D_MODEL = 1024
N_META = 16
RMS_EPS = 1e-6
D_INNER = 2048
SSD_HEADDIM = 64
SSD_HEADS = 32
SSD_GROUPS = 8
HEADS_PER_GROUP = SSD_HEADS // SSD_GROUPS
GROUP_W = HEADS_PER_GROUP * SSD_HEADDIM
D_STATE = 128
CONV_W = 4
CONV_DIM = 4096
SSD_CHUNK = 128
ATTN_HEADS = 16
ATTN_KV_HEADS = 4
HEAD_DIM = 64
ATTN_REP = 4
ATTN_DIM = 1024
KV_DIM = 256
ATTN_SCALE = HEAD_DIM ** -0.5
PAGE_SIZE = 128
N_EXPERT_GROUPS = 4
EXPERTS_PER_GROUP = 8
N_EXPERTS = 32
D_EXPERT = 512
LANES = 128

U_XBC = 0
U_Z = 4096
U_Q = 6144
U_GS = 7168
U_GA = 8192
U_K = 9216
U_V = 9472
U_DTF = 9728
U_COLS = 9856
U_COLS_PADDED = 10240
INPROJ_TN = 1024
INPROJ_TM = 2048


def _cparams(sem, vmem_mb=None):
    kw = dict(dimension_semantics=sem)
    if vmem_mb is not None:
        kw["vmem_limit_bytes"] = vmem_mb * 1024 * 1024
    return pltpu.CompilerParams(**kw)


def _inproj_body(x_ref, g_ref, w_ref, o_ref, xn_ref):
    @pl.when(pl.program_id(1) == 0)
    def _():
        x = x_ref[...]
        ms = jnp.mean(x * x, axis=-1, keepdims=True)
        xn_ref[...] = ((x * lax.rsqrt(ms + RMS_EPS)) * g_ref[...]).astype(BF16)

    o_ref[...] = jnp.dot(xn_ref[...], w_ref[...], preferred_element_type=F32)


def _inproj(x, g, wp):
    t = x.shape[0]
    tm = min(t, INPROJ_TM)
    assert t % tm == 0
    return pl.pallas_call(
        _inproj_body,
        out_shape=jax.ShapeDtypeStruct((t, U_COLS), F32),
        grid=(t // tm, U_COLS_PADDED // INPROJ_TN),
        in_specs=[
            pl.BlockSpec((tm, D_MODEL), lambda i, j: (i, 0)),
            pl.BlockSpec((1, D_MODEL), lambda i, j: (0, 0)),
            pl.BlockSpec((D_MODEL, INPROJ_TN), lambda i, j: (0, j)),
        ],
        out_specs=pl.BlockSpec((tm, INPROJ_TN), lambda i, j: (i, j)),
        scratch_shapes=[pltpu.VMEM((tm, D_MODEL), BF16)],
        compiler_params=_cparams(("parallel", "arbitrary"), 56),
        name="inproj",
    )(x, g.reshape(1, D_MODEL), wp)


CONV_PAD = 8


def _conv_silu_rows(staged, rows, w_ref, b_ref):
    acc = b_ref[...] + staged[CONV_PAD:CONV_PAD + rows, :] * w_ref[CONV_W - 1:CONV_W, :]
    for k in range(CONV_W - 1):
        off = CONV_PAD - (CONV_W - 1) + k
        acc = acc + pltpu.roll(staged, CONV_PAD + rows - off, 0)[0:rows, :] * w_ref[k:k + 1, :]
    return _silu(acc)


def _pack_w_in(w_in):
    o = np.cumsum([0, D_INNER, CONV_DIM, SSD_HEADS, ATTN_DIM, KV_DIM, KV_DIM, ATTN_HEADS, D_MODEL, D_MODEL])
    z, xbc, dt, q, k, v, f, gs, ga = [w_in[:, o[i]:o[i + 1]] for i in range(9)]
    pad_dtf = jnp.zeros((D_MODEL, LANES - SSD_HEADS - ATTN_HEADS), w_in.dtype)
    pad = jnp.zeros((D_MODEL, U_COLS_PADDED - U_COLS), w_in.dtype)
    return jnp.concatenate([xbc, z, q, gs, ga, k, v, dt, f, pad_dtf, pad], axis=1).astype(BF16)


def _log_sigmoid(x):
    return jnp.minimum(x, 0.0) - jnp.log1p(jnp.exp(-jnp.abs(x)))


LOG2E = 1.4426950408889634
QK_ONES_LANE = HEAD_DIM
QK_CK_LANE = HEAD_DIM + 3
V_ONES_ROW = HEAD_DIM
N_SPLIT = 3
ATTN_TQ = 256
QK_AHEAD = 4


def _placement_matrices():
    eq = np.zeros((LANES, ATTN_HEADS * LANES), np.float32)
    ek = np.zeros((LANES, ATTN_KV_HEADS * LANES), np.float32)
    for h in range(ATTN_HEADS):
        g, r = divmod(h, ATTN_REP)
        for s in range(N_SPLIT):
            eq[s * ATTN_HEADS + h, h * LANES + QK_ONES_LANE + s] = 1.0
            ek[s * ATTN_HEADS + h, g * LANES + QK_CK_LANE + s * ATTN_REP + r] = 1.0
    cq = np.zeros((1, ATTN_HEADS * LANES), np.float32)
    for h in range(ATTN_HEADS):
        r = h % ATTN_REP
        for s in range(N_SPLIT):
            cq[0, h * LANES + QK_CK_LANE + s * ATTN_REP + r] = -1.0
    return jnp.asarray(eq, BF16), jnp.asarray(ek, BF16), jnp.asarray(cq)


def _transpose_rows(x, q):
    if q < LANES:
        x = jnp.concatenate([x, jnp.zeros((LANES - q, x.shape[1]), x.dtype)], axis=0)
    return x.T[:, :q]


def _split3(x):
    hi = x.astype(BF16)
    r1 = x - hi.astype(F32)
    mid = r1.astype(BF16)
    lo = (r1 - mid.astype(F32)).astype(BF16)
    return hi, mid, lo


def _split3_stacked(x, axis):
    return jnp.concatenate([t.astype(F32) for t in _split3(x)], axis=axis).astype(BF16)


def _attn_prep_body(q_ref, k_ref, v_ref, dtf_ref, bf_ref, eq_ref, ek_ref, cq_ref, slab_ref, qp_ref, kp_ref, vpt_ref,
                    kt_ref, vt_ref, carry_ref, *, tiles_per_seq, tl, rel_to_last):
    i = pl.program_id(0)

    @pl.when(i % tiles_per_seq == 0)
    def _():
        carry_ref[...] = jnp.zeros_like(carry_ref)

    lf = _log_sigmoid(dtf_ref[...] + bf_ref[...])
    row = lax.broadcasted_iota(jnp.int32, (tl, tl), 0)
    col = lax.broadcasted_iota(jnp.int32, (tl, tl), 1)
    c = jnp.dot((col <= row).astype(F32), lf, precision=HI, preferred_element_type=F32) + carry_ref[...]
    carry_ref[...] = c[tl - 1:tl, :]
    lane = lax.broadcasted_iota(jnp.int32, (tl, LANES), 1)
    c16 = jnp.where(lane < ATTN_HEADS, pltpu.roll(c, LANES - SSD_HEADS, 1), 0.0)
    slab_ref[...] = jnp.where((lane >= SSD_HEADS) & (lane < SSD_HEADS + ATTN_HEADS), lf, c16)
    if rel_to_last:
        c16 = c16 - c16[tl - 1:tl, :]
    hi, mid, lo = _split3(c16 * LOG2E)
    x = jnp.where(lane < ATTN_HEADS, hi.astype(F32),
                  jnp.where(lane < 2 * ATTN_HEADS, pltpu.roll(mid.astype(F32), ATTN_HEADS, 1),
                            pltpu.roll(lo.astype(F32), 2 * ATTN_HEADS, 1))).astype(BF16)
    q_extra = jnp.dot(x, eq_ref[...], preferred_element_type=F32) + cq_ref[...]
    k_extra = jnp.dot(x, ek_ref[...], preferred_element_type=F32)
    low = lane < HEAD_DIM
    for h in range(ATTN_HEADS):
        qx = q_ref[:, (h // 2) * LANES:(h // 2 + 1) * LANES]
        if h % 2:
            qx = pltpu.roll(qx, HEAD_DIM, 1)
        tile = jnp.where(low, qx * (ATTN_SCALE * LOG2E), q_extra[:, h * LANES:(h + 1) * LANES])
        qp_ref[:, h * LANES:(h + 1) * LANES] = tile.astype(BF16)
    ones_k = ((lane >= QK_ONES_LANE) & (lane < QK_CK_LANE)).astype(F32)
    ones_v = (lane == V_ONES_ROW).astype(F32)
    for g in range(ATTN_KV_HEADS):
        kx = k_ref[:, (g // 2) * LANES:(g // 2 + 1) * LANES]
        vx = v_ref[:, (g // 2) * LANES:(g // 2 + 1) * LANES]
        if g % 2:
            kx = pltpu.roll(kx, HEAD_DIM, 1)
            vx = pltpu.roll(vx, HEAD_DIM, 1)
        kp_ref[:, g * LANES:(g + 1) * LANES] = jnp.where(low, kx, k_extra[:, g * LANES:(g + 1) * LANES] + ones_k
                                                         ).astype(BF16)
        vpt_ref[g * LANES:(g + 1) * LANES, :] = _transpose_rows(jnp.where(low, vx, ones_v), tl).astype(BF16)
    kt_ref[...] = _transpose_rows(k_ref[...], tl)
    vt_ref[...] = _transpose_rows(v_ref[...], tl)


def _attn_prep(u, b_forget, n_tokens, seq_len, rel_to_last=False):
    tl = min(n_tokens, 256)
    assert n_tokens % tl == 0 and seq_len % tl == 0
    bf = jnp.zeros((1, LANES), F32).at[0, SSD_HEADS:SSD_HEADS + ATTN_HEADS].set(b_forget)
    eq, ek, cq = _placement_matrices()
    const = lambda i: (0, 0)
    return pl.pallas_call(
        functools.partial(_attn_prep_body, tiles_per_seq=seq_len // tl, tl=tl, rel_to_last=rel_to_last),
        out_shape=(jax.ShapeDtypeStruct((n_tokens, LANES), F32),
                   jax.ShapeDtypeStruct((n_tokens, ATTN_HEADS * LANES), BF16),
                   jax.ShapeDtypeStruct((n_tokens, ATTN_KV_HEADS * LANES), BF16),
                   jax.ShapeDtypeStruct((ATTN_KV_HEADS * LANES, n_tokens), BF16),
                   jax.ShapeDtypeStruct((KV_DIM, n_tokens), F32),
                   jax.ShapeDtypeStruct((KV_DIM, n_tokens), F32)),
        grid=(n_tokens // tl,),
        in_specs=[
            pl.BlockSpec((tl, ATTN_DIM), lambda i: (i, U_Q // ATTN_DIM)),
            pl.BlockSpec((tl, KV_DIM), lambda i: (i, U_K // KV_DIM)),
            pl.BlockSpec((tl, KV_DIM), lambda i: (i, U_V // KV_DIM)),
            pl.BlockSpec((tl, LANES), lambda i: (i, U_DTF // LANES)),
            pl.BlockSpec((1, LANES), const),
            pl.BlockSpec((LANES, ATTN_HEADS * LANES), const),
            pl.BlockSpec((LANES, ATTN_KV_HEADS * LANES), const),
            pl.BlockSpec((1, ATTN_HEADS * LANES), const),
        ],
        out_specs=(pl.BlockSpec((tl, LANES), lambda i: (i, 0)),
                   pl.BlockSpec((tl, ATTN_HEADS * LANES), lambda i: (i, 0)),
                   pl.BlockSpec((tl, ATTN_KV_HEADS * LANES), lambda i: (i, 0)),
                   pl.BlockSpec((ATTN_KV_HEADS * LANES, tl), lambda i: (0, i)),
                   pl.BlockSpec((KV_DIM, tl), lambda i: (0, i)),
                   pl.BlockSpec((KV_DIM, tl), lambda i: (0, i))),
        scratch_shapes=[pltpu.VMEM((1, LANES), F32)],
        compiler_params=_cparams(("arbitrary",), 40),
        name="attn_prep",
    )(u, u, u, u, bf, eq, ek, cq)


def _silu(x):
    return x * (1.0 / (1.0 + jnp.exp(-x)))


def _softplus(x):
    return jnp.maximum(x, 0.0) + jnp.log1p(jnp.exp(-jnp.abs(x)))


def _ssd_body(xbc_ref, z_ref, dtf_ref, conv0_ref, h0_ref, cw_ref, cb_ref, dtb_ref, alog_ref, dskip_ref,
              gn_ref, e_ref, y_ref, hout_ref, xconv_ref, ht_ref, *, q, n_chunks):
    c = pl.program_id(1)

    @pl.when(c == 0)
    def _():
        xconv_ref[0:CONV_PAD, :] = conv0_ref[0]
        for g in range(SSD_GROUPS):
            ht_ref[g] = h0_ref[0, g * GROUP_W:(g + 1) * GROUP_W, :].T

    xconv_ref[CONV_PAD:CONV_PAD + q, :] = xbc_ref[...]
    xc = _conv_silu_rows(xconv_ref[...], q, cw_ref, cb_ref)
    xconv_ref[CONV_PAD - (CONV_W - 1):CONV_PAD, :] = xconv_ref[CONV_PAD + q - (CONV_W - 1):CONV_PAD + q, :]

    dt = _softplus(dtf_ref[...] + dtb_ref[...])
    a = -jnp.exp(alog_ref[...])
    row = lax.broadcasted_iota(jnp.int32, (q, q), 0)
    col = lax.broadcasted_iota(jnp.int32, (q, q), 1)
    causal = col <= row
    tri = causal.astype(BF16)
    acum = jnp.dot(jnp.concatenate([tri] * N_SPLIT, axis=1), _split3_stacked(dt * a, 0),
                   preferred_element_type=F32)
    acum_t = _transpose_rows(acum, q)
    a_last = acum[q - 1:q, :]
    fac = jnp.concatenate([jnp.exp(acum), jnp.exp(a_last - acum) * dt, dt], axis=0)
    fac = jnp.dot(_split3_stacked(fac, 1), e_ref[...], preferred_element_type=F32)
    ea_full, wst_full, dt_full = fac[0:q], fac[q:2 * q], fac[2 * q:3 * q]

    def group_bc(g):
        bg = xc[:, D_INNER + g * D_STATE:D_INNER + (g + 1) * D_STATE]
        cg = xc[:, D_INNER + SSD_GROUPS * D_STATE + g * D_STATE:D_INNER + SSD_GROUPS * D_STATE + (g + 1) * D_STATE]
        cg16 = cg.astype(BF16)
        cbm = lax.dot_general(cg16, bg.astype(BF16), (((1,), (1,)), ((), ())), preferred_element_type=F32)
        return bg, cg16, cbm

    nxt = group_bc(0)
    for g in range(SSD_GROUPS):
        gs = slice(g * GROUP_W, (g + 1) * GROUP_W)
        xg = xc[:, gs]
        bg, cg16, cbm = nxt
        if g + 1 < SSD_GROUPS:
            nxt = group_bc(g + 1)
        htg = ht_ref[g]
        yoff = jnp.dot(cg16, htg.astype(BF16), preferred_element_type=F32) * ea_full[:, gs]
        xw = (xg * wst_full[:, gs]).astype(BF16)
        bgt = _transpose_rows(bg, q).astype(BF16)
        st = jnp.dot(bgt, xw, preferred_element_type=F32)
        ht_ref[g] = ea_full[q - 1:q, gs] * htg + st
        xdt = (xg * dt_full[:, gs]).astype(BF16)
        yd = []
        for r in range(HEADS_PER_GROUP):
            h = g * HEADS_PER_GROUP + r
            seg = acum[:, h:h + 1] - acum_t[h:h + 1, :]
            m = cbm * jnp.exp(jnp.where(causal, seg, -jnp.inf))
            yd.append(jnp.dot(m.astype(BF16), xdt[:, r * SSD_HEADDIM:(r + 1) * SSD_HEADDIM],
                              preferred_element_type=F32))
        yd = jnp.concatenate(yd, axis=1)
        yg = yd + yoff + dskip_ref[:, gs] * xg
        yz = yg * _silu(z_ref[:, gs])
        ms = jnp.mean(yz * yz, axis=-1, keepdims=True)
        y_ref[:, gs] = (yz * lax.rsqrt(ms + RMS_EPS) * gn_ref[:, gs]).astype(y_ref.dtype)

    @pl.when(c == n_chunks - 1)
    def _():
        for g in range(SSD_GROUPS):
            hout_ref[0, g * GROUP_W:(g + 1) * GROUP_W, :] = ht_ref[g].T


def _head_expand_matrix():
    e = np.zeros((LANES, D_INNER), np.float32)
    for h in range(SSD_HEADS):
        e[h, h * SSD_HEADDIM:(h + 1) * SSD_HEADDIM] = 1.0
    return jnp.asarray(np.tile(e, (N_SPLIT, 1)), BF16)


def _ssd(u, n_seq, seq_len, q, conv0, h0, p, y_dtype):
    n_chunks = seq_len // q
    assert seq_len % q == 0 and q % 8 == 0
    conv_ix = (lambda b, c: (b, 0, 0)) if conv0.shape[0] == n_seq and n_seq > 1 else (lambda b, c: (0, 0, 0))
    h_ix = (lambda b, c: (b, 0, 0)) if h0.shape[0] == n_seq and n_seq > 1 else (lambda b, c: (0, 0, 0))
    const2 = lambda b, c: (0, 0)
    return pl.pallas_call(
        functools.partial(_ssd_body, q=q, n_chunks=n_chunks),
        out_shape=(jax.ShapeDtypeStruct((n_seq * seq_len, D_INNER), y_dtype),
                   jax.ShapeDtypeStruct((n_seq, D_INNER, D_STATE), F32)),
        grid=(n_seq, n_chunks),
        in_specs=[
            pl.BlockSpec((q, CONV_DIM), lambda b, c: (b * n_chunks + c, U_XBC // CONV_DIM)),
            pl.BlockSpec((q, D_INNER), lambda b, c: (b * n_chunks + c, U_Z // D_INNER)),
            pl.BlockSpec((q, LANES), lambda b, c: (b * n_chunks + c, U_DTF // LANES)),
            pl.BlockSpec((1, CONV_PAD, CONV_DIM), conv_ix),
            pl.BlockSpec((1, D_INNER, D_STATE), h_ix),
            pl.BlockSpec((CONV_W, CONV_DIM), const2),
            pl.BlockSpec((1, CONV_DIM), const2),
            pl.BlockSpec((1, LANES), const2),
            pl.BlockSpec((1, LANES), const2),
            pl.BlockSpec((1, D_INNER), const2),
            pl.BlockSpec((1, D_INNER), const2),
            pl.BlockSpec((N_SPLIT * LANES, D_INNER), const2),
        ],
        out_specs=(pl.BlockSpec((q, D_INNER), lambda b, c: (b * n_chunks + c, 0)),
                   pl.BlockSpec((1, D_INNER, D_STATE), lambda b, c: (b, 0, 0))),
        scratch_shapes=[pltpu.VMEM((CONV_PAD + q, CONV_DIM), F32),
                        pltpu.VMEM((SSD_GROUPS, D_STATE, GROUP_W), F32)],
        compiler_params=_cparams(("parallel", "arbitrary"), 48),
        name="ssd_q%d" % q,
    )(u, u, u, conv0, h0, p["conv_w"], p["conv_b"], p["dt_bias"], p["a_log"], p["d_skip"], p["ssd_norm_g"],
      p["head_expand"])


def _ssd_params(conv_w, conv_b, dt_bias, a_log, d_skip, ssd_norm_g):
    pad32 = lambda v: jnp.zeros((1, LANES), F32).at[0, :SSD_HEADS].set(v)
    return dict(conv_w=conv_w, conv_b=conv_b.reshape(1, CONV_DIM), dt_bias=pad32(dt_bias), a_log=pad32(a_log),
                d_skip=jnp.repeat(d_skip, SSD_HEADDIM).reshape(1, D_INNER),
                ssd_norm_g=ssd_norm_g.reshape(1, D_INNER), head_expand=_head_expand_matrix())


def _fox_prompt_body(qi_ref, kj_ref, qp_ref, kp_ref, vpt_ref, kpm_ref, vptm_ref, o_ref, m_ref, acc_ref, *, tq, tk):
    step = pl.program_id(1)
    i = qi_ref[step]
    j = kj_ref[step]
    nt = (((1,), (1,)), ((), ()))

    def scores(h, kp):
        g = h // ATTN_REP
        return lax.dot_general(kp[:, g * LANES:(g + 1) * LANES], qp_ref[:, h * LANES:(h + 1) * LANES], nt,
                               preferred_element_type=F32)

    def attend_all(kp, vpt, mask, ahead=QK_AHEAD):
        pending = [scores(h, kp) for h in range(ahead)]
        for h in range(ATTN_HEADS):
            g = h // ATTN_REP
            s = pending.pop(0)
            if h + ahead < ATTN_HEADS:
                pending.append(scores(h + ahead, kp))
            if mask is not None:
                s = jnp.where(mask, s, -jnp.inf)
            m_prev = m_ref[h, 0:1, :]
            m_new = jnp.maximum(m_prev, jnp.max(s, axis=0, keepdims=True))
            alpha = jnp.exp2(m_prev - m_new)
            p = jnp.exp2(s - m_new).astype(BF16)
            m_ref[h, 0:1, :] = m_new
            pv = jnp.dot(vpt[g * LANES:(g + 1) * LANES, :], p, preferred_element_type=F32)
            acc_ref[h] = alpha * acc_ref[h] + pv

    @pl.when(j == 0)
    def _():
        kpm = kpm_ref[...]
        vptm = vptm_ref[...]
        meta_scores = [scores(h, kpm) for h in range(ATTN_HEADS)]
        for h, s in enumerate(meta_scores):
            g = h // ATTN_REP
            m_new = jnp.max(s, axis=0, keepdims=True)
            m_ref[h, 0:1, :] = m_new
            acc_ref[h] = jnp.dot(vptm[g * LANES:(g + 1) * LANES, :], jnp.exp2(s - m_new).astype(BF16),
                                 preferred_element_type=F32)

    @pl.when(j < i)
    def _():
        attend_all(kp_ref[...], vpt_ref[...], None)

    @pl.when(j == i)
    def _():
        key = lax.broadcasted_iota(jnp.int32, (tk, tq), 0)
        qry = lax.broadcasted_iota(jnp.int32, (tk, tq), 1)
        attend_all(kp_ref[...], vpt_ref[...], key <= qry)
        for h in range(ATTN_HEADS):
            acc = acc_ref[h]
            o_t = acc[0:HEAD_DIM, :] * (1.0 / acc[V_ONES_ROW:V_ONES_ROW + 1, :])
            o_ref[:, h * HEAD_DIM:(h + 1) * HEAD_DIM] = o_t.T.astype(o_ref.dtype)


def _fox_prompt(qp, kp, vpt, kp_meta, vpt_meta, n_seq, seq_len, tq):
    nq = seq_len // tq
    assert seq_len % tq == 0
    qi = np.array([i for i in range(nq) for _ in range(i + 1)], np.int32)
    kj = np.array([j for i in range(nq) for j in range(i + 1)], np.int32)
    n_meta = kp_meta.shape[0]
    qw = ATTN_HEADS * LANES
    kw = ATTN_KV_HEADS * LANES
    grid_spec = pltpu.PrefetchScalarGridSpec(
        num_scalar_prefetch=2,
        grid=(n_seq, len(qi)),
        in_specs=[
            pl.BlockSpec((tq, qw), lambda b, s, qi, kj: (b * nq + qi[s], 0)),
            pl.BlockSpec((tq, kw), lambda b, s, qi, kj: (b * nq + kj[s], 0)),
            pl.BlockSpec((kw, tq), lambda b, s, qi, kj: (0, b * nq + kj[s])),
            pl.BlockSpec((n_meta, kw), lambda b, s, qi, kj: (0, 0)),
            pl.BlockSpec((kw, n_meta), lambda b, s, qi, kj: (0, 0)),
        ],
        out_specs=pl.BlockSpec((tq, ATTN_DIM), lambda b, s, qi, kj: (b * nq + qi[s], 0)),
        scratch_shapes=[pltpu.VMEM((ATTN_HEADS, 8, tq), F32),
                        pltpu.VMEM((ATTN_HEADS, LANES, tq), F32)],
    )
    return pl.pallas_call(
        functools.partial(_fox_prompt_body, tq=tq, tk=tq),
        out_shape=jax.ShapeDtypeStruct((n_seq * seq_len, ATTN_DIM), BF16),
        grid_spec=grid_spec,
        compiler_params=_cparams(("parallel", "arbitrary"), 48),
        name="fox_prompt",
    )(jnp.asarray(qi), jnp.asarray(kj), qp, kp, vpt, kp_meta, vpt_meta)


def _fox_sample_body(pt_ref, qaug_ref, kn_ref, vn_ref, dtf_ref, bf_ref, et_ref, suf_ref, ck_hbm, cv_hbm, clf_hbm,
                     o_ref, lfo_ref, kbuf, vbuf, lfbuf, sem, m_ref, l_ref, acc_ref, carry_ref, cnrow_ref,
                     *, npp, n_steps, n_pages, n_seq, s_new):
    seq = pl.program_id(0)
    step = pl.program_id(1)
    total = n_seq * n_steps
    t = seq * n_steps + step
    slot = t % SAMPLE_DMA_SLOTS
    rows = ATTN_HEADS * s_new
    qaug = qaug_ref[...]
    nt = (((1,), (1,)), ((), ()))

    def page_copies(t_i, s):
        seq_i = t_i // n_steps
        step_i = t_i - seq_i * n_steps
        first = seq_i * n_pages + n_pages - (step_i + 1) * npp
        out = []
        for i in range(npp):
            page = pt_ref[first + i]
            out.append(pltpu.make_async_copy(ck_hbm.at[page], kbuf.at[s, i], sem.at[s]))
            out.append(pltpu.make_async_copy(cv_hbm.at[page], vbuf.at[s, i], sem.at[s]))
            out.append(pltpu.make_async_copy(clf_hbm.at[page], lfbuf.at[s, i], sem.at[s]))
        return out

    @pl.when(t == 0)
    def _():
        for t_i in range(SAMPLE_DMA_SLOTS - 1):
            for c in page_copies(t_i, t_i):
                c.start()

    @pl.when(step == 0)
    def _():
        lane = lax.broadcasted_iota(jnp.int32, (s_new, LANES), 1)
        lfn = _log_sigmoid(dtf_ref[...] + bf_ref[...])
        lfo_ref[...] = jnp.where((lane >= SSD_HEADS) & (lane < SSD_HEADS + ATTN_HEADS), lfn, 0.0)
        lf16 = lfn[:, SSD_HEADS:SSD_HEADS + ATTN_HEADS]
        tri = (lax.broadcasted_iota(jnp.int32, (s_new, s_new), 1)
               <= lax.broadcasted_iota(jnp.int32, (s_new, s_new), 0)).astype(F32)
        cn = jnp.dot(tri, lf16, precision=HI, preferred_element_type=F32) * LOG2E
        cne = lax.dot_general(et_ref[...], cn, nt, precision=HI, preferred_element_type=F32)
        trow = lax.broadcasted_iota(jnp.int32, (rows, s_new), 0) % s_new
        tcol = lax.broadcasted_iota(jnp.int32, (rows, s_new), 1)
        cn_row = jnp.sum(jnp.where(tcol == trow, cne, 0.0), axis=-1, keepdims=True)
        cnrow_ref[...] = jnp.broadcast_to(cn_row, cnrow_ref.shape)
        ss = lax.dot_general(qaug[:, 0:KV_DIM], kn_ref[...].astype(BF16), nt, preferred_element_type=F32)
        ss = jnp.where(tcol <= trow, ss + cn_row - cne, -jnp.inf)
        m = jnp.max(ss, axis=-1, keepdims=True)
        p = jnp.exp2(ss - m)
        m_ref[...] = jnp.broadcast_to(m, m_ref.shape)
        l_ref[...] = jnp.broadcast_to(jnp.sum(p, axis=-1, keepdims=True), l_ref.shape)
        acc_ref[...] = jnp.dot(p.astype(BF16), vn_ref[...].astype(BF16), preferred_element_type=F32)
        carry_ref[...] = jnp.zeros(carry_ref.shape, F32)

    for c in page_copies(t, slot):
        c.wait()
    lf_all = jnp.concatenate([lfbuf[slot, i] for i in range(npp)], axis=0)
    in_page = jnp.dot(jnp.concatenate(_split3(lf_all), axis=1), suf_ref[...], preferred_element_type=F32)
    page_total = jnp.sum(lf_all, axis=1, keepdims=True)
    carry = carry_ref[:, 0:1]
    scores = [None] * npp
    for i in reversed(range(npp)):
        hs = slice(i * ATTN_HEADS, (i + 1) * ATTN_HEADS)
        r_hi, r_mid, r_lo = _split3((in_page[hs] + carry) * LOG2E)
        carry = carry + page_total[hs]
        kt = kbuf[slot, i].reshape(KV_DIM, PAGE_SIZE).astype(BF16)
        k_aug = jnp.concatenate([kt, r_hi, r_mid, r_lo], axis=0)
        scores[i] = jnp.dot(qaug, k_aug, preferred_element_type=F32)
    carry_ref[...] = jnp.broadcast_to(carry, carry_ref.shape)
    s_all = jnp.concatenate(scores, axis=1) + cnrow_ref[:, 0:1]
    vt_all = jnp.concatenate([vbuf[slot, i].reshape(KV_DIM, PAGE_SIZE).astype(BF16) for i in range(npp)], axis=1)
    m_prev = m_ref[...]
    m_new = jnp.maximum(m_prev, jnp.max(s_all, axis=-1, keepdims=True))
    alpha = jnp.exp2(m_prev - m_new)
    p = jnp.exp2(s_all - m_new[:, 0:1])
    l_ref[...] = alpha * l_ref[...] + jnp.sum(p, axis=-1, keepdims=True)
    m_ref[...] = m_new
    pv = lax.dot_general(p.astype(BF16), vt_all, nt, preferred_element_type=F32)
    acc_ref[...] = jnp.concatenate([alpha, alpha], axis=1) * acc_ref[...] + pv
    ahead = t + (SAMPLE_DMA_SLOTS - 1)
    for c in page_copies(jnp.where(ahead < total, ahead, t), ahead % SAMPLE_DMA_SLOTS):
        c.start()

    @pl.when(step == n_steps - 1)
    def _():
        inv = 1.0 / l_ref[...]
        o_ref[...] = acc_ref[...] * jnp.concatenate([inv, inv], axis=1)

    @pl.when(t == total - 1)
    def _():
        for back in range(SAMPLE_DMA_SLOTS - 1):
            for c in page_copies(t, (t + 1 + back) % SAMPLE_DMA_SLOTS):
                c.wait()


SAMPLE_PAGES_PER_STEP = 32
SAMPLE_DMA_SLOTS = 3


def _fox_sample(u_s, cache_k, cache_v, cache_logf, page_table, b_forget, n_seq, s_new, npp):
    n_pages = page_table.shape[1]
    assert n_pages % npp == 0 and ATTN_HEADS * s_new == LANES
    n_steps = n_pages // npp
    n_pool = cache_k.shape[0]
    rows = ATTN_HEADS * s_new
    ck = jnp.transpose(cache_k, (0, 2, 3, 1))
    cv = jnp.transpose(cache_v, (0, 2, 3, 1))
    clf = jnp.transpose(cache_logf, (0, 2, 1))
    q = u_s[:, U_Q:U_Q + ATTN_DIM].reshape(n_seq, s_new, ATTN_KV_HEADS, ATTN_REP, HEAD_DIM)
    qbd = jnp.einsum("btgrd,gh->bgrthd", q, jnp.eye(ATTN_KV_HEADS, dtype=F32)).reshape(n_seq, rows, KV_DIM)
    et = np.zeros((rows, ATTN_HEADS), np.float32)
    et[np.arange(rows), np.arange(rows) // s_new] = 1.0
    et_b = jnp.broadcast_to(jnp.asarray(et), (n_seq, rows, ATTN_HEADS))
    qaug = jnp.concatenate([qbd * (ATTN_SCALE * LOG2E)] + [et_b] * N_SPLIT, axis=-1).astype(BF16)
    aug_w = KV_DIM + N_SPLIT * ATTN_HEADS
    suf = np.tile(np.triu(np.ones((PAGE_SIZE, PAGE_SIZE), np.float32), 1).T, (N_SPLIT, 1))
    bf = jnp.zeros((1, LANES), F32).at[0, SSD_HEADS:SSD_HEADS + ATTN_HEADS].set(b_forget)

    const2 = lambda b, s, pt: (0, 0)
    hbm = pl.BlockSpec(memory_space=pl.ANY)
    in_specs = [
        pl.BlockSpec((None, rows, aug_w), lambda b, s, pt: (b, 0, 0)),
        pl.BlockSpec((s_new, KV_DIM), lambda b, s, pt: (b, U_K // KV_DIM)),
        pl.BlockSpec((s_new, KV_DIM), lambda b, s, pt: (b, U_V // KV_DIM)),
        pl.BlockSpec((s_new, LANES), lambda b, s, pt: (b, U_DTF // LANES)),
        pl.BlockSpec((1, LANES), const2),
        pl.BlockSpec((rows, ATTN_HEADS), const2),
        pl.BlockSpec((N_SPLIT * PAGE_SIZE, PAGE_SIZE), const2),
        hbm, hbm, hbm,
    ]
    grid_spec = pltpu.PrefetchScalarGridSpec(
        num_scalar_prefetch=1,
        grid=(n_seq, n_steps),
        in_specs=in_specs,
        out_specs=(pl.BlockSpec((None, rows, KV_DIM), lambda b, s, pt: (b, 0, 0)),
                   pl.BlockSpec((s_new, LANES), lambda b, s, pt: (b, 0))),
        scratch_shapes=[pltpu.VMEM((SAMPLE_DMA_SLOTS, npp, ATTN_KV_HEADS, HEAD_DIM, PAGE_SIZE), F32),
                        pltpu.VMEM((SAMPLE_DMA_SLOTS, npp, ATTN_KV_HEADS, HEAD_DIM, PAGE_SIZE), F32),
                        pltpu.VMEM((SAMPLE_DMA_SLOTS, npp, ATTN_HEADS, PAGE_SIZE), F32),
                        pltpu.SemaphoreType.DMA((SAMPLE_DMA_SLOTS,)),
                        pltpu.VMEM((rows, LANES), F32), pltpu.VMEM((rows, LANES), F32),
                        pltpu.VMEM((rows, KV_DIM), F32), pltpu.VMEM((ATTN_HEADS, LANES), F32),
                        pltpu.VMEM((rows, LANES), F32)],
    )
    o_raw, lf_slab = pl.pallas_call(
        functools.partial(_fox_sample_body, npp=npp, n_steps=n_steps, n_pages=n_pages, n_seq=n_seq, s_new=s_new),
        out_shape=(jax.ShapeDtypeStruct((n_seq, rows, KV_DIM), F32),
                   jax.ShapeDtypeStruct((n_seq * s_new, LANES), F32)),
        grid_spec=grid_spec,
        compiler_params=_cparams(("arbitrary", "arbitrary"), 48),
        name="fox_sample",
    )(page_table.reshape(-1), qaug, u_s, u_s, u_s, bf, jnp.asarray(et), jnp.asarray(suf, BF16), ck, cv, clf)
    o = o_raw.reshape(n_seq, ATTN_KV_HEADS, ATTN_REP, s_new, ATTN_KV_HEADS, HEAD_DIM)
    o = jnp.einsum("bgrtgd->btgrd", o).reshape(n_seq * s_new, ATTN_DIM)
    return o, lf_slab


ROUTE_E0, ROUTE_E1, ROUTE_W0, ROUTE_W1 = 0, 1, 2, 3
ROUTER_EXPERT_LANE0 = N_EXPERT_GROUPS
_BIG_LANE = 4 * LANES


def _sigmoid(x):
    return 1.0 / (1.0 + jnp.exp(-x))


def _mix_body(y_ref, o_ref, gs_ref, ga_ref, h_ref, wssd_ref, wattn_ref, wout_ref, g2_ref, wr_ref, br_ref,
              h1_ref, hn_ref, route_ref):
    ys = jnp.dot(y_ref[...].astype(BF16), wssd_ref[...], preferred_element_type=F32)
    oa = jnp.dot(o_ref[...].astype(BF16), wattn_ref[...], preferred_element_type=F32)
    mix = _sigmoid(gs_ref[...]) * ys + _sigmoid(ga_ref[...]) * oa
    h1 = h_ref[...] + jnp.dot(mix.astype(BF16), wout_ref[...], preferred_element_type=F32)
    h1_ref[...] = h1
    ms = jnp.mean(h1 * h1, axis=-1, keepdims=True)
    hn = (h1 * lax.rsqrt(ms + RMS_EPS)) * g2_ref[...]
    hn_ref[...] = hn
    hn_hi = hn.astype(BF16)
    hn_lo = (hn - hn_hi.astype(F32)).astype(BF16)
    logits = jnp.dot(jnp.concatenate([hn_hi, hn_hi, hn_lo], axis=1), wr_ref[...], preferred_element_type=F32)
    route_ref[...] = _route(logits + br_ref[...])


def _route(logits):
    lane = lax.broadcasted_iota(jnp.int32, logits.shape, 1)
    gl = jnp.where(lane < N_EXPERT_GROUPS, logits, -jnp.inf)
    gmax = jnp.max(gl, axis=-1, keepdims=True)
    gsel = jnp.min(jnp.where(gl == gmax, lane, _BIG_LANE), axis=-1, keepdims=True)
    wgrp = 1.0 / jnp.sum(jnp.exp(gl - gmax), axis=-1, keepdims=True)
    elane = lane - ROUTER_EXPERT_LANE0
    in_group = (elane >= gsel * EXPERTS_PER_GROUP) & (elane < (gsel + 1) * EXPERTS_PER_GROUP)
    el = jnp.where(in_group, logits, -jnp.inf)
    t1 = jnp.max(el, axis=-1, keepdims=True)
    i1 = jnp.min(jnp.where(el == t1, lane, _BIG_LANE), axis=-1, keepdims=True)
    el2 = jnp.where(lane == i1, -jnp.inf, el)
    t2 = jnp.max(el2, axis=-1, keepdims=True)
    i2 = jnp.min(jnp.where(el2 == t2, lane, _BIG_LANE), axis=-1, keepdims=True)
    e21 = jnp.exp(t2 - t1)
    w1 = wgrp / (1.0 + e21)
    w2 = w1 * e21
    return jnp.where(lane == ROUTE_E0, (i1 - ROUTER_EXPERT_LANE0).astype(F32),
                     jnp.where(lane == ROUTE_E1, (i2 - ROUTER_EXPERT_LANE0).astype(F32),
                               jnp.where(lane == ROUTE_W0, w1, jnp.where(lane == ROUTE_W1, w2, 0.0))))


def _mix(y, o, u, h, p, tm):
    t = h.shape[0]
    assert t % tm == 0
    row = lambda i: (i, 0)
    const = lambda i: (0, 0)
    return pl.pallas_call(
        _mix_body,
        out_shape=(jax.ShapeDtypeStruct((t, D_MODEL), F32), jax.ShapeDtypeStruct((t, D_MODEL), F32),
                   jax.ShapeDtypeStruct((t, LANES), F32)),
        grid=(t // tm,),
        in_specs=[
            pl.BlockSpec((tm, D_INNER), row),
            pl.BlockSpec((tm, ATTN_DIM), row),
            pl.BlockSpec((tm, D_MODEL), lambda i: (i, U_GS // D_MODEL)),
            pl.BlockSpec((tm, D_MODEL), lambda i: (i, U_GA // D_MODEL)),
            pl.BlockSpec((tm, D_MODEL), row),
            pl.BlockSpec((D_INNER, D_MODEL), const),
            pl.BlockSpec((ATTN_DIM, D_MODEL), const),
            pl.BlockSpec((D_MODEL, D_MODEL), const),
            pl.BlockSpec((1, D_MODEL), const),
            pl.BlockSpec((3 * D_MODEL, LANES), const),
            pl.BlockSpec((1, LANES), const),
        ],
        out_specs=(pl.BlockSpec((tm, D_MODEL), row), pl.BlockSpec((tm, D_MODEL), row),
                   pl.BlockSpec((tm, LANES), row)),
        compiler_params=_cparams(("parallel",), 56),
        name="mix_route",
    )(y, o, u, u, h, p["w_ssd_br"], p["w_attn_br"], p["w_out"], p["norm2_g"], p["w_router"], p["b_router"])


def _mix_params(w_ssd_br, w_attn_br, w_out, norm2_g, w_rg, b_rg, w_re, b_re):
    n_r = N_EXPERT_GROUPS + N_EXPERTS
    w_router = jnp.concatenate([w_rg, w_re, jnp.zeros((D_MODEL, LANES - n_r), F32)], axis=1)
    w_hi = w_router.astype(BF16)
    w_lo = (w_router - w_hi.astype(F32)).astype(BF16)
    w_router = jnp.concatenate([w_hi, w_lo, w_hi], axis=0)
    b_router = jnp.concatenate([b_rg, b_re, jnp.zeros((LANES - n_r,), F32)]).reshape(1, LANES)
    return dict(w_ssd_br=w_ssd_br.astype(BF16), w_attn_br=w_attn_br.astype(BF16), w_out=w_out.astype(BF16),
                norm2_g=norm2_g.reshape(1, D_MODEL), w_router=w_router, b_router=b_router)


def _moe_rank_body(route_ref, pos_ref, cnt_ref, carry_ref, *, tm):
    @pl.when(pl.program_id(0) == 0)
    def _():
        carry_ref[...] = jnp.zeros(carry_ref.shape, F32)

    route = route_ref[...]
    lane = lax.broadcasted_iota(jnp.int32, (tm, LANES), 1).astype(F32)
    hit0 = lane == route[:, ROUTE_E0:ROUTE_E0 + 1]
    hit1 = lane == route[:, ROUTE_E1:ROUTE_E1 + 1]
    onehot = hit0.astype(F32) + hit1.astype(F32)
    before = (lax.broadcasted_iota(jnp.int32, (tm, tm), 1) < lax.broadcasted_iota(jnp.int32, (tm, tm), 0))
    c = jnp.dot(before.astype(BF16), onehot.astype(BF16), preferred_element_type=F32) + carry_ref[...]
    pos0 = jnp.sum(jnp.where(hit0, c, 0.0), axis=-1, keepdims=True)
    pos1 = jnp.sum(jnp.where(hit1, c, 0.0), axis=-1, keepdims=True)
    pos_ref[...] = jnp.where(lane == 0.0, pos0, jnp.where(lane == 1.0, pos1, 0.0))
    total = carry_ref[...] + jnp.sum(onehot, axis=0, keepdims=True)
    carry_ref[...] = total
    cnt_ref[...] = total


def _moe_rank(route, tm):
    t = route.shape[0]
    assert t % tm == 0
    return pl.pallas_call(
        functools.partial(_moe_rank_body, tm=tm),
        out_shape=(jax.ShapeDtypeStruct((t, LANES), F32), jax.ShapeDtypeStruct((1, LANES), F32)),
        grid=(t // tm,),
        in_specs=[pl.BlockSpec((tm, LANES), lambda i: (i, 0))],
        out_specs=(pl.BlockSpec((tm, LANES), lambda i: (i, 0)), pl.BlockSpec((1, LANES), lambda i: (0, 0))),
        scratch_shapes=[pltpu.VMEM((1, LANES), F32)],
        compiler_params=_cparams(("arbitrary",)),
        name="moe_rank",
    )(route)


def _experts_body(be_ref, nu_ref, x_ref, wg_ref, wu_ref, wd_ref, y_ref, wg16_ref, wu16_ref, wd16_ref):
    i = pl.program_id(0)
    prev = be_ref[jnp.maximum(i - 1, 0)]

    @pl.when((i == 0) | (be_ref[i] != prev))
    def _():
        wg16_ref[...] = wg_ref[...].astype(BF16)
        wu16_ref[...] = wu_ref[...].astype(BF16)
        wd16_ref[...] = wd_ref[...].astype(BF16)

    @pl.when(i < nu_ref[0])
    def _():
        sub = x_ref.shape[0] // 2
        rows = [slice(0, sub), slice(sub, 2 * sub)]
        gate_up = []
        for r in rows:
            x16 = x_ref[r, :].astype(BF16)
            gate_up.append((jnp.dot(x16, wg16_ref[...], preferred_element_type=F32),
                            jnp.dot(x16, wu16_ref[...], preferred_element_type=F32)))
        for r, (gt, up) in zip(rows, gate_up):
            y_ref[r, :] = jnp.dot((_silu(gt) * up).astype(BF16), wd16_ref[...], preferred_element_type=F32)

    @pl.when(i >= nu_ref[0])
    def _():
        y_ref[...] = jnp.zeros(y_ref.shape, F32)


def _experts(xs, block_e, n_used, w_gate, w_up, w_down, blk):
    rows = xs.shape[0]
    n_blocks = rows // blk
    grid_spec = pltpu.PrefetchScalarGridSpec(
        num_scalar_prefetch=2,
        grid=(n_blocks,),
        in_specs=[
            pl.BlockSpec((blk, D_MODEL), lambda i, be, nu: (jnp.minimum(i, nu[0] - 1), 0)),
            pl.BlockSpec((None, D_MODEL, D_EXPERT), lambda i, be, nu: (be[i], 0, 0)),
            pl.BlockSpec((None, D_MODEL, D_EXPERT), lambda i, be, nu: (be[i], 0, 0)),
            pl.BlockSpec((None, D_EXPERT, D_MODEL), lambda i, be, nu: (be[i], 0, 0)),
        ],
        out_specs=pl.BlockSpec((blk, D_MODEL), lambda i, be, nu: (i, 0)),
        scratch_shapes=[pltpu.VMEM((D_MODEL, D_EXPERT), BF16), pltpu.VMEM((D_MODEL, D_EXPERT), BF16),
                        pltpu.VMEM((D_EXPERT, D_MODEL), BF16)],
    )
    return pl.pallas_call(
        _experts_body,
        out_shape=jax.ShapeDtypeStruct((rows, D_MODEL), F32),
        grid_spec=grid_spec,
        compiler_params=_cparams(("arbitrary",), 48),
        name="moe_experts",
    )(block_e, n_used, xs, w_gate, w_up, w_down)


MOE_ROWS_PER_BLOCK = 256
MOE_ROWS_PER_BLOCK_FEW_TOKENS = 128
MOE_TOKEN_TILE = 256
DMA_ISSUE_UNROLL = 8


def _row_copy(src, src_row, dst, dst_row, sem):
    return pltpu.make_async_copy(src.at[pl.ds(src_row, 1)], dst.at[pl.ds(dst_row, 1)], sem)


def _dispatch_body(zb_ref, nu_ref, d0_ref, d1_ref, hn_ref, xs_ref, zero_ref, sem, zsem, *, tm, blk, n_blocks):
    @pl.when(pl.program_id(0) == 0)
    def _():
        zero_ref[...] = jnp.zeros(zero_ref.shape, F32)

        def zero_block(b):
            return pltpu.make_async_copy(zero_ref, xs_ref.at[pl.ds(b * blk, blk)], zsem)

        for e in range(N_EXPERTS):
            zero_block(zb_ref[e]).start()

        def start_tail(b, c):
            zero_block(b).start()
            return c

        def wait_tail(b, c):
            zero_block(b).wait()
            return c

        lax.fori_loop(nu_ref[0], n_blocks, start_tail, 0)
        for e in range(N_EXPERTS):
            zero_block(0).wait()
        lax.fori_loop(nu_ref[0], n_blocks, wait_tail, 0)

    def issue(r, c):
        _row_copy(hn_ref, r, xs_ref, d0_ref[r], sem).start()
        _row_copy(hn_ref, r, xs_ref, d1_ref[r], sem).start()
        return c

    lax.fori_loop(0, tm, issue, 0, unroll=DMA_ISSUE_UNROLL)

    def drain(r, c):
        _row_copy(hn_ref, 0, xs_ref, 0, sem).wait()
        _row_copy(hn_ref, 0, xs_ref, 0, sem).wait()
        return c

    lax.fori_loop(0, tm, drain, 0, unroll=DMA_ISSUE_UNROLL)


def _dispatch(hn, d0, d1, zero_blocks, n_used, n_blocks, blk, tm):
    t = hn.shape[0]
    assert t % tm == 0
    smem_tile = lambda: pl.BlockSpec((tm,), lambda i, zb, nu: (i,), memory_space=pltpu.SMEM)
    grid_spec = pltpu.PrefetchScalarGridSpec(
        num_scalar_prefetch=2,
        grid=(t // tm,),
        in_specs=[smem_tile(), smem_tile(), pl.BlockSpec((tm, D_MODEL), lambda i, zb, nu: (i, 0))],
        out_specs=pl.BlockSpec(memory_space=pl.ANY),
        scratch_shapes=[pltpu.VMEM((blk, D_MODEL), F32), pltpu.SemaphoreType.DMA, pltpu.SemaphoreType.DMA],
    )
    return pl.pallas_call(
        functools.partial(_dispatch_body, tm=tm, blk=blk, n_blocks=n_blocks),
        out_shape=jax.ShapeDtypeStruct((n_blocks * blk, D_MODEL), F32),
        grid_spec=grid_spec,
        compiler_params=_cparams(("arbitrary",)),
        name="moe_dispatch",
    )(zero_blocks, n_used, d0, d1, hn)


def _combine_body(d0_ref, d1_ref, d0n_ref, d1n_ref, h1_ref, route_ref, g_ref, ys_ref, o_ref, buf, sem, *, tm, n_tiles):
    i = pl.program_id(0)
    slot = i % 2

    def gather(da_ref, db_ref, s):
        def issue(r, c):
            _row_copy(ys_ref, da_ref[r], buf.at[s, 0], r, sem.at[s]).start()
            _row_copy(ys_ref, db_ref[r], buf.at[s, 1], r, sem.at[s]).start()
            return c
        lax.fori_loop(0, tm, issue, 0, unroll=DMA_ISSUE_UNROLL)

    @pl.when(i == 0)
    def _():
        gather(d0_ref, d1_ref, 0)

    @pl.when(i + 1 < n_tiles)
    def _():
        gather(d0n_ref, d1n_ref, 1 - slot)

    def drain(r, c):
        _row_copy(ys_ref, 0, buf.at[slot, 0], 0, sem.at[slot]).wait()
        _row_copy(ys_ref, 0, buf.at[slot, 1], 0, sem.at[slot]).wait()
        return c

    lax.fori_loop(0, tm, drain, 0, unroll=DMA_ISSUE_UNROLL)
    route = route_ref[...]
    h = (h1_ref[...] + route[:, ROUTE_W0:ROUTE_W0 + 1] * buf[slot, 0]
         + route[:, ROUTE_W1:ROUTE_W1 + 1] * buf[slot, 1])
    ms = jnp.mean(h * h, axis=-1, keepdims=True)
    o_ref[...] = (h * lax.rsqrt(ms + RMS_EPS)) * g_ref[...]


def _combine(h1, ys, d0, d1, route, g, tm):
    t = h1.shape[0]
    n_tiles = t // tm
    row = lambda i: (i, 0)
    cur = lambda: pl.BlockSpec((tm,), lambda i: (i,), memory_space=pltpu.SMEM)
    nxt = lambda: pl.BlockSpec((tm,), lambda i: (jnp.minimum(i + 1, n_tiles - 1),), memory_space=pltpu.SMEM)
    return pl.pallas_call(
        functools.partial(_combine_body, tm=tm, n_tiles=n_tiles),
        out_shape=jax.ShapeDtypeStruct((t, D_MODEL), F32),
        grid=(n_tiles,),
        in_specs=[cur(), cur(), nxt(), nxt(), pl.BlockSpec((tm, D_MODEL), row), pl.BlockSpec((tm, LANES), row),
                  pl.BlockSpec((1, D_MODEL), lambda i: (0, 0)), pl.BlockSpec(memory_space=pl.ANY)],
        out_specs=pl.BlockSpec((tm, D_MODEL), row),
        scratch_shapes=[pltpu.VMEM((2, 2, tm, D_MODEL), F32), pltpu.SemaphoreType.DMA((2,))],
        compiler_params=_cparams(("arbitrary",)),
        name="moe_combine_norm",
    )(d0, d1, d0, d1, h1, route, g.reshape(1, D_MODEL), ys)


def _moe(hn, h1, route, w_gate, w_up, w_down, final_g, blk):
    t = hn.shape[0]
    pos, cnt = _moe_rank(route, MOE_TOKEN_TILE)
    counts = cnt[0, :N_EXPERTS].astype(jnp.int32)
    padded = (counts + blk - 1) // blk * blk
    ends = jnp.cumsum(padded)
    starts = ends - padded
    expert_ids = jnp.arange(N_EXPERTS, dtype=jnp.int32)

    def dest(e_lane, p_lane):
        e = route[:, e_lane].astype(jnp.int32)
        start = jnp.sum(jnp.where(e[:, None] == expert_ids[None, :], starts[None, :], 0), axis=1)
        return start + pos[:, p_lane].astype(jnp.int32)

    d0 = dest(ROUTE_E0, 0)
    d1 = dest(ROUTE_E1, 1)
    n_blocks = (2 * t + N_EXPERTS * (blk - 1) + blk - 1) // blk
    first_row = jnp.arange(n_blocks, dtype=jnp.int32) * blk
    block_e = jnp.minimum(jnp.sum((ends[None, :] <= first_row[:, None]).astype(jnp.int32), axis=1), N_EXPERTS - 1)
    n_used = (ends[-1] // blk).astype(jnp.int32).reshape(1)
    zero_blocks = jnp.clip((ends - 1) // blk, 0, n_blocks - 1).astype(jnp.int32)
    xs = _dispatch(hn, d0, d1, zero_blocks, n_used, n_blocks, blk, MOE_TOKEN_TILE)
    ys = _experts(xs, block_e, n_used, w_gate, w_up, w_down, blk)
    return _combine(h1, ys, d0, d1, route, final_g, MOE_TOKEN_TILE)


def _conv_history(rows):
    n = rows.shape[0]
    return jnp.concatenate([jnp.zeros((n, CONV_PAD - (CONV_W - 1), CONV_DIM), F32), rows], axis=1)


def _ucols(u, start, width):
    return u[:, start:start + width]


def kernel(x_prompt, x_sample, cache_k, cache_v, cache_logf, state_ssm, state_conv, page_table, meta_tokens, norm1_g, w_in, conv_w, conv_b, dt_bias, a_log, d_skip, ssd_norm_g, b_forget, w_ssd_br, w_attn_br, w_out, norm2_g, w_router_group, b_router_group, w_router_expert, b_router_expert, w_exp_gate, w_exp_up, w_exp_down, final_norm_g):
    nb, sl, _ = x_prompt.shape
    sb, ss, _ = x_sample.shape
    assert w_in.shape[0] == 1, "single-layer step"
    ly = 0
    xp = x_prompt.reshape(nb * sl, D_MODEL)
    xs = x_sample.reshape(sb * ss, D_MODEL)

    wp = _pack_w_in(w_in[ly])
    u_p = _inproj(xp, norm1_g[ly], wp)
    u_s = _inproj(xs, norm1_g[ly], wp)
    u_m = _inproj(meta_tokens, norm1_g[ly], wp)

    sp = _ssd_params(conv_w[ly], conv_b[ly], dt_bias[ly], a_log[ly], d_skip[ly], ssd_norm_g[ly])
    zero_hist = jnp.zeros((1, CONV_PAD, CONV_DIM), F32)
    zero_state = jnp.zeros((1, D_INNER, D_STATE), F32)
    _, h_meta = _ssd(u_m, 1, N_META, N_META, zero_hist, zero_state, sp, BF16)
    hist_meta = _conv_history(_ucols(u_m, U_XBC, CONV_DIM)[None, N_META - (CONV_W - 1):])
    y_p, ssm_p = _ssd(u_p, nb, sl, SSD_CHUNK, hist_meta, h_meta, sp, BF16)
    y_s, ssm_s = _ssd(u_s, sb, ss, ss, _conv_history(state_conv[ly]),
                      state_ssm[ly].reshape(sb, D_INNER, D_STATE), sp, F32)

    slab_m, _, kp_m, vpt_m, kt_m, vt_m = _attn_prep(u_m, b_forget[ly], N_META, N_META, rel_to_last=True)
    slab_p, qp_p, kp_p, vpt_p, kt_p, vt_p = _attn_prep(u_p, b_forget[ly], nb * sl, sl)
    o_p = _fox_prompt(qp_p, kp_p, vpt_p, kp_m, vpt_m, nb, sl, ATTN_TQ)
    o_s, slab_s = _fox_sample(u_s, cache_k[ly], cache_v[ly], cache_logf[ly], page_table, b_forget[ly], sb, ss,
                              SAMPLE_PAGES_PER_STEP)

    mp = _mix_params(w_ssd_br[ly], w_attn_br[ly], w_out[ly], norm2_g[ly], w_router_group[ly], b_router_group[ly],
                     w_router_expert[ly], b_router_expert[ly])
    h1_p, hn_p, route_p = _mix(y_p, o_p, u_p, xp, mp, 256)
    h1_s, hn_s, route_s = _mix(y_s, o_s, u_s, xs, mp, 256)
    out_p = _moe(hn_p, h1_p, route_p, w_exp_gate[ly], w_exp_up[ly], w_exp_down[ly], final_norm_g,
                 MOE_ROWS_PER_BLOCK)
    out_s = _moe(hn_s, h1_s, route_s, w_exp_gate[ly], w_exp_up[ly], w_exp_down[ly], final_norm_g,
                 MOE_ROWS_PER_BLOCK_FEW_TOKENS)

    def with_meta(meta_rows, rows, width):
        m = jnp.broadcast_to(meta_rows[None], (nb, N_META, width))
        return jnp.concatenate([m, rows.reshape(nb, sl, width)], axis=1)[None]

    def kv_with_meta(t_meta, t_rows):
        m = jnp.broadcast_to(t_meta.reshape(ATTN_KV_HEADS, HEAD_DIM, 1, N_META), (ATTN_KV_HEADS, HEAD_DIM, nb, N_META))
        full = jnp.concatenate([m, t_rows.reshape(ATTN_KV_HEADS, HEAD_DIM, nb, sl)], axis=3)
        return jnp.transpose(full, (2, 3, 0, 1))[None]

    lf_m = slab_m[:, SSD_HEADS:SSD_HEADS + ATTN_HEADS]
    lf_p = slab_p[:, SSD_HEADS:SSD_HEADS + ATTN_HEADS]
    lf_s = slab_s[:, SSD_HEADS:SSD_HEADS + ATTN_HEADS]
    kv_shape_s = (1, sb, ss, ATTN_KV_HEADS, HEAD_DIM)
    state_shape = (SSD_HEADS, SSD_HEADDIM, D_STATE)
    tail = CONV_W - 1
    return (
        out_p.reshape(nb, sl, D_MODEL),
        out_s.reshape(sb, ss, D_MODEL),
        kv_with_meta(kt_m, kt_p),
        kv_with_meta(vt_m, vt_p),
        with_meta(lf_m, lf_p, ATTN_HEADS),
        ssm_p.reshape((1, nb) + state_shape),
        u_p.reshape(nb, sl, U_COLS)[None, :, sl - tail:, U_XBC:U_XBC + CONV_DIM],
        _ucols(u_s, U_K, KV_DIM).reshape(kv_shape_s),
        _ucols(u_s, U_V, KV_DIM).reshape(kv_shape_s),
        lf_s.reshape(1, sb, ss, ATTN_HEADS),
        ssm_s.reshape((1, sb) + state_shape),
        u_s.reshape(sb, ss, U_COLS)[None, :, ss - tail:, U_XBC:U_XBC + CONV_DIM],
    )
```

```python
import functools

import numpy as np
import jax
import jax.numpy as jnp
from jax import lax
from jax.experimental import pallas as pl
from jax.experimental.pallas import tpu as pltpu

F32 = jnp.float32
BF16 = jnp.bfloat16
HI = lax.Precision.HIGHEST

D_MODEL = 1024
N_META = 16
RMS_EPS = 1e-6
D_INNER = 2048
SSD_HEADDIM = 64
SSD_HEADS = 32
SSD_GROUPS = 8
HEADS_PER_GROUP = SSD_HEADS // SSD_GROUPS
GROUP_W = HEADS_PER_GROUP * SSD_HEADDIM
D_STATE = 128
CONV_W = 4
CONV_DIM = 4096
SSD_CHUNK = 128
ATTN_HEADS = 16
ATTN_KV_HEADS = 4
HEAD_DIM = 64
ATTN_REP = 4
ATTN_DIM = 1024
KV_DIM = 256
ATTN_SCALE = HEAD_DIM ** -0.5
PAGE_SIZE = 128
N_EXPERT_GROUPS = 4
EXPERTS_PER_GROUP = 8
N_EXPERTS = 32
D_EXPERT = 512
LANES = 128

U_XBC = 0
U_Z = 4096
U_Q = 6144
U_GS = 7168
U_GA = 8192
U_K = 9216
U_V = 9472
U_DTF = 9728
U_COLS = 9856
U_COLS_PADDED = 10240
INPROJ_TN = 1024
INPROJ_TM = 2048


def _cparams(sem, vmem_mb=None):
    kw = dict(dimension_semantics=sem)
    if vmem_mb is not None:
        kw["vmem_limit_bytes"] = vmem_mb * 1024 * 1024
    return pltpu.CompilerParams(**kw)


def _inproj_body(x_ref, g_ref, w_ref, o_ref, xn_ref):
    @pl.when(pl.program_id(1) == 0)
    def _():
        x = x_ref[...]
        ms = jnp.mean(x * x, axis=-1, keepdims=True)
        xn_ref[...] = ((x * lax.rsqrt(ms + RMS_EPS)) * g_ref[...]).astype(BF16)

    o_ref[...] = jnp.dot(xn_ref[...], w_ref[...], preferred_element_type=F32)


def _inproj(x, g, wp):
    t = x.shape[0]
    tm = min(t, INPROJ_TM)
    assert t % tm == 0
    return pl.pallas_call(
        _inproj_body,
        out_shape=jax.ShapeDtypeStruct((t, U_COLS), F32),
        grid=(t // tm, U_COLS_PADDED // INPROJ_TN),
        in_specs=[
            pl.BlockSpec((tm, D_MODEL), lambda i, j: (i, 0)),
            pl.BlockSpec((1, D_MODEL), lambda i, j: (0, 0)),
            pl.BlockSpec((D_MODEL, INPROJ_TN), lambda i, j: (0, j)),
        ],
        out_specs=pl.BlockSpec((tm, INPROJ_TN), lambda i, j: (i, j)),
        scratch_shapes=[pltpu.VMEM((tm, D_MODEL), BF16)],
        compiler_params=_cparams(("parallel", "arbitrary"), 56),
        name="inproj",
    )(x, g.reshape(1, D_MODEL), wp)


CONV_PAD = 8


def _conv_silu_rows(staged, rows, w_ref, b_ref):
    acc = b_ref[...] + staged[CONV_PAD:CONV_PAD + rows, :] * w_ref[CONV_W - 1:CONV_W, :]
    for k in range(CONV_W - 1):
        off = CONV_PAD - (CONV_W - 1) + k
        acc = acc + pltpu.roll(staged, CONV_PAD + rows - off, 0)[0:rows, :] * w_ref[k:k + 1, :]
    return _silu(acc)


def _pack_w_in(w_in):
    o = np.cumsum([0, D_INNER, CONV_DIM, SSD_HEADS, ATTN_DIM, KV_DIM, KV_DIM, ATTN_HEADS, D_MODEL, D_MODEL])
    z, xbc, dt, q, k, v, f, gs, ga = [w_in[:, o[i]:o[i + 1]] for i in range(9)]
    pad_dtf = jnp.zeros((D_MODEL, LANES - SSD_HEADS - ATTN_HEADS), w_in.dtype)
    pad = jnp.zeros((D_MODEL, U_COLS_PADDED - U_COLS), w_in.dtype)
    return jnp.concatenate([xbc, z, q, gs, ga, k, v, dt, f, pad_dtf, pad], axis=1).astype(BF16)


def _log_sigmoid(x):
    return jnp.minimum(x, 0.0) - jnp.log1p(jnp.exp(-jnp.abs(x)))


LOG2E = 1.4426950408889634
QK_ONES_LANE = HEAD_DIM
QK_CK_LANE = HEAD_DIM + 3
V_ONES_ROW = HEAD_DIM
N_SPLIT = 3
ATTN_TQ = 256
QK_AHEAD = 4


def _placement_matrices():
    eq = np.zeros((LANES, ATTN_HEADS * LANES), np.float32)
    ek = np.zeros((LANES, ATTN_KV_HEADS * LANES), np.float32)
    for h in range(ATTN_HEADS):
        g, r = divmod(h, ATTN_REP)
        for s in range(N_SPLIT):
            eq[s * ATTN_HEADS + h, h * LANES + QK_ONES_LANE + s] = 1.0
            ek[s * ATTN_HEADS + h, g * LANES + QK_CK_LANE + s * ATTN_REP + r] = 1.0
    cq = np.zeros((1, ATTN_HEADS * LANES), np.float32)
    for h in range(ATTN_HEADS):
        r = h % ATTN_REP
        for s in range(N_SPLIT):
            cq[0, h * LANES + QK_CK_LANE + s * ATTN_REP + r] = -1.0
    return jnp.asarray(eq, BF16), jnp.asarray(ek, BF16), jnp.asarray(cq)


def _transpose_rows(x, q):
    if q < LANES:
        x = jnp.concatenate([x, jnp.zeros((LANES - q, x.shape[1]), x.dtype)], axis=0)
    return x.T[:, :q]


def _split3(x):
    hi = x.astype(BF16)
    r1 = x - hi.astype(F32)
    mid = r1.astype(BF16)
    lo = (r1 - mid.astype(F32)).astype(BF16)
    return hi, mid, lo


def _split3_stacked(x, axis):
    return jnp.concatenate([t.astype(F32) for t in _split3(x)], axis=axis).astype(BF16)


def _attn_prep_body(q_ref, k_ref, v_ref, dtf_ref, bf_ref, eq_ref, ek_ref, cq_ref, slab_ref, qp_ref, kp_ref, vpt_ref,
                    kt_ref, vt_ref, carry_ref, *, tiles_per_seq, tl, rel_to_last):
    i = pl.program_id(0)

    @pl.when(i % tiles_per_seq == 0)
    def _():
        carry_ref[...] = jnp.zeros_like(carry_ref)

    lf = _log_sigmoid(dtf_ref[...] + bf_ref[...])
    row = lax.broadcasted_iota(jnp.int32, (tl, tl), 0)
    col = lax.broadcasted_iota(jnp.int32, (tl, tl), 1)
    c = jnp.dot((col <= row).astype(F32), lf, precision=HI, preferred_element_type=F32) + carry_ref[...]
    carry_ref[...] = c[tl - 1:tl, :]
    lane = lax.broadcasted_iota(jnp.int32, (tl, LANES), 1)
    c16 = jnp.where(lane < ATTN_HEADS, pltpu.roll(c, LANES - SSD_HEADS, 1), 0.0)
    slab_ref[...] = jnp.where((lane >= SSD_HEADS) & (lane < SSD_HEADS + ATTN_HEADS), lf, c16)
    if rel_to_last:
        c16 = c16 - c16[tl - 1:tl, :]
    hi, mid, lo = _split3(c16 * LOG2E)
    x = jnp.where(lane < ATTN_HEADS, hi.astype(F32),
                  jnp.where(lane < 2 * ATTN_HEADS, pltpu.roll(mid.astype(F32), ATTN_HEADS, 1),
                            pltpu.roll(lo.astype(F32), 2 * ATTN_HEADS, 1))).astype(BF16)
    q_extra = jnp.dot(x, eq_ref[...], preferred_element_type=F32) + cq_ref[...]
    k_extra = jnp.dot(x, ek_ref[...], preferred_element_type=F32)
    low = lane < HEAD_DIM
    for h in range(ATTN_HEADS):
        qx = q_ref[:, (h // 2) * LANES:(h // 2 + 1) * LANES]
        if h % 2:
            qx = pltpu.roll(qx, HEAD_DIM, 1)
        tile = jnp.where(low, qx * (ATTN_SCALE * LOG2E), q_extra[:, h * LANES:(h + 1) * LANES])
        qp_ref[:, h * LANES:(h + 1) * LANES] = tile.astype(BF16)
    ones_k = ((lane >= QK_ONES_LANE) & (lane < QK_CK_LANE)).astype(F32)
    ones_v = (lane == V_ONES_ROW).astype(F32)
    for g in range(ATTN_KV_HEADS):
        kx = k_ref[:, (g // 2) * LANES:(g // 2 + 1) * LANES]
        vx = v_ref[:, (g // 2) * LANES:(g // 2 + 1) * LANES]
        if g % 2:
            kx = pltpu.roll(kx, HEAD_DIM, 1)
            vx = pltpu.roll(vx, HEAD_DIM, 1)
        kp_ref[:, g * LANES:(g + 1) * LANES] = jnp.where(low, kx, k_extra[:, g * LANES:(g + 1) * LANES] + ones_k
                                                         ).astype(BF16)
        vpt_ref[g * LANES:(g + 1) * LANES, :] = _transpose_rows(jnp.where(low, vx, ones_v), tl).astype(BF16)
    kt_ref[...] = _transpose_rows(k_ref[...], tl)
    vt_ref[...] = _transpose_rows(v_ref[...], tl)


def _attn_prep(u, b_forget, n_tokens, seq_len, rel_to_last=False):
    tl = min(n_tokens, 256)
    assert n_tokens % tl == 0 and seq_len % tl == 0
    bf = jnp.zeros((1, LANES), F32).at[0, SSD_HEADS:SSD_HEADS + ATTN_HEADS].set(b_forget)
    eq, ek, cq = _placement_matrices()
    const = lambda i: (0, 0)
    return pl.pallas_call(
        functools.partial(_attn_prep_body, tiles_per_seq=seq_len // tl, tl=tl, rel_to_last=rel_to_last),
        out_shape=(jax.ShapeDtypeStruct((n_tokens, LANES), F32),
                   jax.ShapeDtypeStruct((n_tokens, ATTN_HEADS * LANES), BF16),
                   jax.ShapeDtypeStruct((n_tokens, ATTN_KV_HEADS * LANES), BF16),
                   jax.ShapeDtypeStruct((ATTN_KV_HEADS * LANES, n_tokens), BF16),
                   jax.ShapeDtypeStruct((KV_DIM, n_tokens), F32),
                   jax.ShapeDtypeStruct((KV_DIM, n_tokens), F32)),
        grid=(n_tokens // tl,),
        in_specs=[
            pl.BlockSpec((tl, ATTN_DIM), lambda i: (i, U_Q // ATTN_DIM)),
            pl.BlockSpec((tl, KV_DIM), lambda i: (i, U_K // KV_DIM)),
            pl.BlockSpec((tl, KV_DIM), lambda i: (i, U_V // KV_DIM)),
            pl.BlockSpec((tl, LANES), lambda i: (i, U_DTF // LANES)),
            pl.BlockSpec((1, LANES), const),
            pl.BlockSpec((LANES, ATTN_HEADS * LANES), const),
            pl.BlockSpec((LANES, ATTN_KV_HEADS * LANES), const),
            pl.BlockSpec((1, ATTN_HEADS * LANES), const),
        ],
        out_specs=(pl.BlockSpec((tl, LANES), lambda i: (i, 0)),
                   pl.BlockSpec((tl, ATTN_HEADS * LANES), lambda i: (i, 0)),
                   pl.BlockSpec((tl, ATTN_KV_HEADS * LANES), lambda i: (i, 0)),
                   pl.BlockSpec((ATTN_KV_HEADS * LANES, tl), lambda i: (0, i)),
                   pl.BlockSpec((KV_DIM, tl), lambda i: (0, i)),
                   pl.BlockSpec((KV_DIM, tl), lambda i: (0, i))),
        scratch_shapes=[pltpu.VMEM((1, LANES), F32)],
        compiler_params=_cparams(("arbitrary",), 40),
        name="attn_prep",
    )(u, u, u, u, bf, eq, ek, cq)


def _silu(x):
    return x * (1.0 / (1.0 + jnp.exp(-x)))


def _softplus(x):
    return jnp.maximum(x, 0.0) + jnp.log1p(jnp.exp(-jnp.abs(x)))


def _ssd_body(xbc_ref, z_ref, dtf_ref, conv0_ref, h0_ref, cw_ref, cb_ref, dtb_ref, alog_ref, dskip_ref,
              gn_ref, e_ref, y_ref, hout_ref, xconv_ref, ht_ref, *, q, n_chunks):
    c = pl.program_id(1)

    @pl.when(c == 0)
    def _():
        xconv_ref[0:CONV_PAD, :] = conv0_ref[0]
        for g in range(SSD_GROUPS):
            ht_ref[g] = h0_ref[0, g * GROUP_W:(g + 1) * GROUP_W, :].T

    xconv_ref[CONV_PAD:CONV_PAD + q, :] = xbc_ref[...]
    xc = _conv_silu_rows(xconv_ref[...], q, cw_ref, cb_ref)
    xconv_ref[CONV_PAD - (CONV_W - 1):CONV_PAD, :] = xconv_ref[CONV_PAD + q - (CONV_W - 1):CONV_PAD + q, :]

    dt = _softplus(dtf_ref[...] + dtb_ref[...])
    a = -jnp.exp(alog_ref[...])
    row = lax.broadcasted_iota(jnp.int32, (q, q), 0)
    col = lax.broadcasted_iota(jnp.int32, (q, q), 1)
    causal = col <= row
    tri = causal.astype(BF16)
    acum = jnp.dot(jnp.concatenate([tri] * N_SPLIT, axis=1), _split3_stacked(dt * a, 0),
                   preferred_element_type=F32)
    acum_t = _transpose_rows(acum, q)
    a_last = acum[q - 1:q, :]
    fac = jnp.concatenate([jnp.exp(acum), jnp.exp(a_last - acum) * dt, dt], axis=0)
    fac = jnp.dot(_split3_stacked(fac, 1), e_ref[...], preferred_element_type=F32)
    ea_full, wst_full, dt_full = fac[0:q], fac[q:2 * q], fac[2 * q:3 * q]

    def group_bc(g):
        bg = xc[:, D_INNER + g * D_STATE:D_INNER + (g + 1) * D_STATE]
        cg = xc[:, D_INNER + SSD_GROUPS * D_STATE + g * D_STATE:D_INNER + SSD_GROUPS * D_STATE + (g + 1) * D_STATE]
        cg16 = cg.astype(BF16)
        cbm = lax.dot_general(cg16, bg.astype(BF16), (((1,), (1,)), ((), ())), preferred_element_type=F32)
        return bg, cg16, cbm

    nxt = group_bc(0)
    for g in range(SSD_GROUPS):
        gs = slice(g * GROUP_W, (g + 1) * GROUP_W)
        xg = xc[:, gs]
        bg, cg16, cbm = nxt
        if g + 1 < SSD_GROUPS:
            nxt = group_bc(g + 1)
        htg = ht_ref[g]
        yoff = jnp.dot(cg16, htg.astype(BF16), preferred_element_type=F32) * ea_full[:, gs]
        xw = (xg * wst_full[:, gs]).astype(BF16)
        bgt = _transpose_rows(bg, q).astype(BF16)
        st = jnp.dot(bgt, xw, preferred_element_type=F32)
        ht_ref[g] = ea_full[q - 1:q, gs] * htg + st
        xdt = (xg * dt_full[:, gs]).astype(BF16)
        yd = []
        for r in range(HEADS_PER_GROUP):
            h = g * HEADS_PER_GROUP + r
            seg = acum[:, h:h + 1] - acum_t[h:h + 1, :]
            m = cbm * jnp.exp(jnp.where(causal, seg, -jnp.inf))
            yd.append(jnp.dot(m.astype(BF16), xdt[:, r * SSD_HEADDIM:(r + 1) * SSD_HEADDIM],
                              preferred_element_type=F32))
        yd = jnp.concatenate(yd, axis=1)
        yg = yd + yoff + dskip_ref[:, gs] * xg
        yz = yg * _silu(z_ref[:, gs])
        ms = jnp.mean(yz * yz, axis=-1, keepdims=True)
        y_ref[:, gs] = (yz * lax.rsqrt(ms + RMS_EPS) * gn_ref[:, gs]).astype(y_ref.dtype)

    @pl.when(c == n_chunks - 1)
    def _():
        for g in range(SSD_GROUPS):
            hout_ref[0, g * GROUP_W:(g + 1) * GROUP_W, :] = ht_ref[g].T


def _head_expand_matrix():
    e = np.zeros((LANES, D_INNER), np.float32)
    for h in range(SSD_HEADS):
        e[h, h * SSD_HEADDIM:(h + 1) * SSD_HEADDIM] = 1.0
    return jnp.asarray(np.tile(e, (N_SPLIT, 1)), BF16)


def _ssd(u, n_seq, seq_len, q, conv0, h0, p, y_dtype):
    n_chunks = seq_len // q
    assert seq_len % q == 0 and q % 8 == 0
    conv_ix = (lambda b, c: (b, 0, 0)) if conv0.shape[0] == n_seq and n_seq > 1 else (lambda b, c: (0, 0, 0))
    h_ix = (lambda b, c: (b, 0, 0)) if h0.shape[0] == n_seq and n_seq > 1 else (lambda b, c: (0, 0, 0))
    const2 = lambda b, c: (0, 0)
    return pl.pallas_call(
        functools.partial(_ssd_body, q=q, n_chunks=n_chunks),
        out_shape=(jax.ShapeDtypeStruct((n_seq * seq_len, D_INNER), y_dtype),
                   jax.ShapeDtypeStruct((n_seq, D_INNER, D_STATE), F32)),
        grid=(n_seq, n_chunks),
        in_specs=[
            pl.BlockSpec((q, CONV_DIM), lambda b, c: (b * n_chunks + c, U_XBC // CONV_DIM)),
            pl.BlockSpec((q, D_INNER), lambda b, c: (b * n_chunks + c, U_Z // D_INNER)),
            pl.BlockSpec((q, LANES), lambda b, c: (b * n_chunks + c, U_DTF // LANES)),
            pl.BlockSpec((1, CONV_PAD, CONV_DIM), conv_ix),
            pl.BlockSpec((1, D_INNER, D_STATE), h_ix),
            pl.BlockSpec((CONV_W, CONV_DIM), const2),
            pl.BlockSpec((1, CONV_DIM), const2),
            pl.BlockSpec((1, LANES), const2),
            pl.BlockSpec((1, LANES), const2),
            pl.BlockSpec((1, D_INNER), const2),
            pl.BlockSpec((1, D_INNER), const2),
            pl.BlockSpec((N_SPLIT * LANES, D_INNER), const2),
        ],
        out_specs=(pl.BlockSpec((q, D_INNER), lambda b, c: (b * n_chunks + c, 0)),
                   pl.BlockSpec((1, D_INNER, D_STATE), lambda b, c: (b, 0, 0))),
        scratch_shapes=[pltpu.VMEM((CONV_PAD + q, CONV_DIM), F32),
                        pltpu.VMEM((SSD_GROUPS, D_STATE, GROUP_W), F32)],
        compiler_params=_cparams(("parallel", "arbitrary"), 48),
        name="ssd_q%d" % q,
    )(u, u, u, conv0, h0, p["conv_w"], p["conv_b"], p["dt_bias"], p["a_log"], p["d_skip"], p["ssd_norm_g"],
      p["head_expand"])


def _ssd_params(conv_w, conv_b, dt_bias, a_log, d_skip, ssd_norm_g):
    pad32 = lambda v: jnp.zeros((1, LANES), F32).at[0, :SSD_HEADS].set(v)
    return dict(conv_w=conv_w, conv_b=conv_b.reshape(1, CONV_DIM), dt_bias=pad32(dt_bias), a_log=pad32(a_log),
                d_skip=jnp.repeat(d_skip, SSD_HEADDIM).reshape(1, D_INNER),
                ssd_norm_g=ssd_norm_g.reshape(1, D_INNER), head_expand=_head_expand_matrix())


def _fox_prompt_body(qi_ref, kj_ref, qp_ref, kp_ref, vpt_ref, kpm_ref, vptm_ref, o_ref, m_ref, acc_ref, *, tq, tk):
    step = pl.program_id(1)
    i = qi_ref[step]
    j = kj_ref[step]
    nt = (((1,), (1,)), ((), ()))

    def scores(h, kp):
        g = h // ATTN_REP
        return lax.dot_general(kp[:, g * LANES:(g + 1) * LANES], qp_ref[:, h * LANES:(h + 1) * LANES], nt,
                               preferred_element_type=F32)

    def attend_all(kp, vpt, mask, ahead=QK_AHEAD):
        pending = [scores(h, kp) for h in range(ahead)]
        for h in range(ATTN_HEADS):
            g = h // ATTN_REP
            s = pending.pop(0)
            if h + ahead < ATTN_HEADS:
                pending.append(scores(h + ahead, kp))
            if mask is not None:
                s = jnp.where(mask, s, -jnp.inf)
            m_prev = m_ref[h, 0:1, :]
            m_new = jnp.maximum(m_prev, jnp.max(s, axis=0, keepdims=True))
            alpha = jnp.exp2(m_prev - m_new)
            p = jnp.exp2(s - m_new).astype(BF16)
            m_ref[h, 0:1, :] = m_new
            pv = jnp.dot(vpt[g * LANES:(g + 1) * LANES, :], p, preferred_element_type=F32)
            acc_ref[h] = alpha * acc_ref[h] + pv

    @pl.when(j == 0)
    def _():
        kpm = kpm_ref[...]
        vptm = vptm_ref[...]
        meta_scores = [scores(h, kpm) for h in range(ATTN_HEADS)]
        for h, s in enumerate(meta_scores):
            g = h // ATTN_REP
            m_new = jnp.max(s, axis=0, keepdims=True)
            m_ref[h, 0:1, :] = m_new
            acc_ref[h] = jnp.dot(vptm[g * LANES:(g + 1) * LANES, :], jnp.exp2(s - m_new).astype(BF16),
                                 preferred_element_type=F32)

    @pl.when(j < i)
    def _():
        attend_all(kp_ref[...], vpt_ref[...], None)

    @pl.when(j == i)
    def _():
        key = lax.broadcasted_iota(jnp.int32, (tk, tq), 0)
        qry = lax.broadcasted_iota(jnp.int32, (tk, tq), 1)
        attend_all(kp_ref[...], vpt_ref[...], key <= qry)
        for h in range(ATTN_HEADS):
            acc = acc_ref[h]
            o_t = acc[0:HEAD_DIM, :] * (1.0 / acc[V_ONES_ROW:V_ONES_ROW + 1, :])
            o_ref[:, h * HEAD_DIM:(h + 1) * HEAD_DIM] = o_t.T.astype(o_ref.dtype)


def _fox_prompt(qp, kp, vpt, kp_meta, vpt_meta, n_seq, seq_len, tq):
    nq = seq_len // tq
    assert seq_len % tq == 0
    qi = np.array([i for i in range(nq) for _ in range(i + 1)], np.int32)
    kj = np.array([j for i in range(nq) for j in range(i + 1)], np.int32)
    n_meta = kp_meta.shape[0]
    qw = ATTN_HEADS * LANES
    kw = ATTN_KV_HEADS * LANES
    grid_spec = pltpu.PrefetchScalarGridSpec(
        num_scalar_prefetch=2,
        grid=(n_seq, len(qi)),
        in_specs=[
            pl.BlockSpec((tq, qw), lambda b, s, qi, kj: (b * nq + qi[s], 0)),
            pl.BlockSpec((tq, kw), lambda b, s, qi, kj: (b * nq + kj[s], 0)),
            pl.BlockSpec((kw, tq), lambda b, s, qi, kj: (0, b * nq + kj[s])),
            pl.BlockSpec((n_meta, kw), lambda b, s, qi, kj: (0, 0)),
            pl.BlockSpec((kw, n_meta), lambda b, s, qi, kj: (0, 0)),
        ],
        out_specs=pl.BlockSpec((tq, ATTN_DIM), lambda b, s, qi, kj: (b * nq + qi[s], 0)),
        scratch_shapes=[pltpu.VMEM((ATTN_HEADS, 8, tq), F32),
                        pltpu.VMEM((ATTN_HEADS, LANES, tq), F32)],
    )
    return pl.pallas_call(
        functools.partial(_fox_prompt_body, tq=tq, tk=tq),
        out_shape=jax.ShapeDtypeStruct((n_seq * seq_len, ATTN_DIM), BF16),
        grid_spec=grid_spec,
        compiler_params=_cparams(("parallel", "arbitrary"), 48),
        name="fox_prompt",
    )(jnp.asarray(qi), jnp.asarray(kj), qp, kp, vpt, kp_meta, vpt_meta)


def _fox_sample_body(pt_ref, qaug_ref, kn_ref, vn_ref, dtf_ref, bf_ref, et_ref, suf_ref, ck_hbm, cv_hbm, clf_hbm,
                     o_ref, lfo_ref, kbuf, vbuf, lfbuf, sem, m_ref, l_ref, acc_ref, carry_ref, cnrow_ref,
                     *, npp, n_steps, n_pages, n_seq, s_new):
    seq = pl.program_id(0)
    step = pl.program_id(1)
    total = n_seq * n_steps
    t = seq * n_steps + step
    slot = t % SAMPLE_DMA_SLOTS
    rows = ATTN_HEADS * s_new
    qaug = qaug_ref[...]
    nt = (((1,), (1,)), ((), ()))

    def page_copies(t_i, s):
        seq_i = t_i // n_steps
        step_i = t_i - seq_i * n_steps
        first = seq_i * n_pages + n_pages - (step_i + 1) * npp
        out = []
        for i in range(npp):
            page = pt_ref[first + i]
            out.append(pltpu.make_async_copy(ck_hbm.at[page], kbuf.at[s, i], sem.at[s]))
            out.append(pltpu.make_async_copy(cv_hbm.at[page], vbuf.at[s, i], sem.at[s]))
            out.append(pltpu.make_async_copy(clf_hbm.at[page], lfbuf.at[s, i], sem.at[s]))
        return out

    @pl.when(t == 0)
    def _():
        for t_i in range(SAMPLE_DMA_SLOTS - 1):
            for c in page_copies(t_i, t_i):
                c.start()

    @pl.when(step == 0)
    def _():
        lane = lax.broadcasted_iota(jnp.int32, (s_new, LANES), 1)
        lfn = _log_sigmoid(dtf_ref[...] + bf_ref[...])
        lfo_ref[...] = jnp.where((lane >= SSD_HEADS) & (lane < SSD_HEADS + ATTN_HEADS), lfn, 0.0)
        lf16 = lfn[:, SSD_HEADS:SSD_HEADS + ATTN_HEADS]
        tri = (lax.broadcasted_iota(jnp.int32, (s_new, s_new), 1)
               <= lax.broadcasted_iota(jnp.int32, (s_new, s_new), 0)).astype(F32)
        cn = jnp.dot(tri, lf16, precision=HI, preferred_element_type=F32) * LOG2E
        cne = lax.dot_general(et_ref[...], cn, nt, precision=HI, preferred_element_type=F32)
        trow = lax.broadcasted_iota(jnp.int32, (rows, s_new), 0) % s_new
        tcol = lax.broadcasted_iota(jnp.int32, (rows, s_new), 1)
        cn_row = jnp.sum(jnp.where(tcol == trow, cne, 0.0), axis=-1, keepdims=True)
        cnrow_ref[...] = jnp.broadcast_to(cn_row, cnrow_ref.shape)
        ss = lax.dot_general(qaug[:, 0:KV_DIM], kn_ref[...].astype(BF16), nt, preferred_element_type=F32)
        ss = jnp.where(tcol <= trow, ss + cn_row - cne, -jnp.inf)
        m = jnp.max(ss, axis=-1, keepdims=True)
        p = jnp.exp2(ss - m)
        m_ref[...] = jnp.broadcast_to(m, m_ref.shape)
        l_ref[...] = jnp.broadcast_to(jnp.sum(p, axis=-1, keepdims=True), l_ref.shape)
        acc_ref[...] = jnp.dot(p.astype(BF16), vn_ref[...].astype(BF16), preferred_element_type=F32)
        carry_ref[...] = jnp.zeros(carry_ref.shape, F32)

    for c in page_copies(t, slot):
        c.wait()
    lf_all = jnp.concatenate([lfbuf[slot, i] for i in range(npp)], axis=0)
    in_page = jnp.dot(jnp.concatenate(_split3(lf_all), axis=1), suf_ref[...], preferred_element_type=F32)
    page_total = jnp.sum(lf_all, axis=1, keepdims=True)
    carry = carry_ref[:, 0:1]
    scores = [None] * npp
    for i in reversed(range(npp)):
        hs = slice(i * ATTN_HEADS, (i + 1) * ATTN_HEADS)
        r_hi, r_mid, r_lo = _split3((in_page[hs] + carry) * LOG2E)
        carry = carry + page_total[hs]
        kt = kbuf[slot, i].reshape(KV_DIM, PAGE_SIZE).astype(BF16)
        k_aug = jnp.concatenate([kt, r_hi, r_mid, r_lo], axis=0)
        scores[i] = jnp.dot(qaug, k_aug, preferred_element_type=F32)
    carry_ref[...] = jnp.broadcast_to(carry, carry_ref.shape)
    s_all = jnp.concatenate(scores, axis=1) + cnrow_ref[:, 0:1]
    vt_all = jnp.concatenate([vbuf[slot, i].reshape(KV_DIM, PAGE_SIZE).astype(BF16) for i in range(npp)], axis=1)
    m_prev = m_ref[...]
    m_new = jnp.maximum(m_prev, jnp.max(s_all, axis=-1, keepdims=True))
    alpha = jnp.exp2(m_prev - m_new)
    p = jnp.exp2(s_all - m_new[:, 0:1])
    l_ref[...] = alpha * l_ref[...] + jnp.sum(p, axis=-1, keepdims=True)
    m_ref[...] = m_new
    pv = lax.dot_general(p.astype(BF16), vt_all, nt, preferred_element_type=F32)
    acc_ref[...] = jnp.concatenate([alpha, alpha], axis=1) * acc_ref[...] + pv
    ahead = t + (SAMPLE_DMA_SLOTS - 1)
    for c in page_copies(jnp.where(ahead < total, ahead, t), ahead % SAMPLE_DMA_SLOTS):
        c.start()

    @pl.when(step == n_steps - 1)
    def _():
        inv = 1.0 / l_ref[...]
        o_ref[...] = acc_ref[...] * jnp.concatenate([inv, inv], axis=1)

    @pl.when(t == total - 1)
    def _():
        for back in range(SAMPLE_DMA_SLOTS - 1):
            for c in page_copies(t, (t + 1 + back) % SAMPLE_DMA_SLOTS):
                c.wait()


SAMPLE_PAGES_PER_STEP = 32
SAMPLE_DMA_SLOTS = 3


def _fox_sample(u_s, cache_k, cache_v, cache_logf, page_table, b_forget, n_seq, s_new, npp):
    n_pages = page_table.shape[1]
    assert n_pages % npp == 0 and ATTN_HEADS * s_new == LANES
    n_steps = n_pages // npp
    n_pool = cache_k.shape[0]
    rows = ATTN_HEADS * s_new
    ck = jnp.transpose(cache_k, (0, 2, 3, 1))
    cv = jnp.transpose(cache_v, (0, 2, 3, 1))
    clf = jnp.transpose(cache_logf, (0, 2, 1))
    q = u_s[:, U_Q:U_Q + ATTN_DIM].reshape(n_seq, s_new, ATTN_KV_HEADS, ATTN_REP, HEAD_DIM)
    qbd = jnp.einsum("btgrd,gh->bgrthd", q, jnp.eye(ATTN_KV_HEADS, dtype=F32)).reshape(n_seq, rows, KV_DIM)
    et = np.zeros((rows, ATTN_HEADS), np.float32)
    et[np.arange(rows), np.arange(rows) // s_new] = 1.0
    et_b = jnp.broadcast_to(jnp.asarray(et), (n_seq, rows, ATTN_HEADS))
    qaug = jnp.concatenate([qbd * (ATTN_SCALE * LOG2E)] + [et_b] * N_SPLIT, axis=-1).astype(BF16)
    aug_w = KV_DIM + N_SPLIT * ATTN_HEADS
    suf = np.tile(np.triu(np.ones((PAGE_SIZE, PAGE_SIZE), np.float32), 1).T, (N_SPLIT, 1))
    bf = jnp.zeros((1, LANES), F32).at[0, SSD_HEADS:SSD_HEADS + ATTN_HEADS].set(b_forget)

    const2 = lambda b, s, pt: (0, 0)
    hbm = pl.BlockSpec(memory_space=pl.ANY)
    in_specs = [
        pl.BlockSpec((None, rows, aug_w), lambda b, s, pt: (b, 0, 0)),
        pl.BlockSpec((s_new, KV_DIM), lambda b, s, pt: (b, U_K // KV_DIM)),
        pl.BlockSpec((s_new, KV_DIM), lambda b, s, pt: (b, U_V // KV_DIM)),
        pl.BlockSpec((s_new, LANES), lambda b, s, pt: (b, U_DTF // LANES)),
        pl.BlockSpec((1, LANES), const2),
        pl.BlockSpec((rows, ATTN_HEADS), const2),
        pl.BlockSpec((N_SPLIT * PAGE_SIZE, PAGE_SIZE), const2),
        hbm, hbm, hbm,
    ]
    grid_spec = pltpu.PrefetchScalarGridSpec(
        num_scalar_prefetch=1,
        grid=(n_seq, n_steps),
        in_specs=in_specs,
        out_specs=(pl.BlockSpec((None, rows, KV_DIM), lambda b, s, pt: (b, 0, 0)),
                   pl.BlockSpec((s_new, LANES), lambda b, s, pt: (b, 0))),
        scratch_shapes=[pltpu.VMEM((SAMPLE_DMA_SLOTS, npp, ATTN_KV_HEADS, HEAD_DIM, PAGE_SIZE), F32),
                        pltpu.VMEM((SAMPLE_DMA_SLOTS, npp, ATTN_KV_HEADS, HEAD_DIM, PAGE_SIZE), F32),
                        pltpu.VMEM((SAMPLE_DMA_SLOTS, npp, ATTN_HEADS, PAGE_SIZE), F32),
                        pltpu.SemaphoreType.DMA((SAMPLE_DMA_SLOTS,)),
                        pltpu.VMEM((rows, LANES), F32), pltpu.VMEM((rows, LANES), F32),
                        pltpu.VMEM((rows, KV_DIM), F32), pltpu.VMEM((ATTN_HEADS, LANES), F32),
                        pltpu.VMEM((rows, LANES), F32)],
    )
    o_raw, lf_slab = pl.pallas_call(
        functools.partial(_fox_sample_body, npp=npp, n_steps=n_steps, n_pages=n_pages, n_seq=n_seq, s_new=s_new),
        out_shape=(jax.ShapeDtypeStruct((n_seq, rows, KV_DIM), F32),
                   jax.ShapeDtypeStruct((n_seq * s_new, LANES), F32)),
        grid_spec=grid_spec,
        compiler_params=_cparams(("arbitrary", "arbitrary"), 48),
        name="fox_sample",
    )(page_table.reshape(-1), qaug, u_s, u_s, u_s, bf, jnp.asarray(et), jnp.asarray(suf, BF16), ck, cv, clf)
    o = o_raw.reshape(n_seq, ATTN_KV_HEADS, ATTN_REP, s_new, ATTN_KV_HEADS, HEAD_DIM)
    o = jnp.einsum("bgrtgd->btgrd", o).reshape(n_seq * s_new, ATTN_DIM)
    return o, lf_slab


ROUTE_E0, ROUTE_E1, ROUTE_W0, ROUTE_W1 = 0, 1, 2, 3
ROUTER_EXPERT_LANE0 = N_EXPERT_GROUPS
_BIG_LANE = 4 * LANES


def _sigmoid(x):
    return 1.0 / (1.0 + jnp.exp(-x))


def _mix_body(y_ref, o_ref, gs_ref, ga_ref, h_ref, wssd_ref, wattn_ref, wout_ref, g2_ref, wr_ref, br_ref,
              h1_ref, hn_ref, route_ref):
    ys = jnp.dot(y_ref[...].astype(BF16), wssd_ref[...], preferred_element_type=F32)
    oa = jnp.dot(o_ref[...].astype(BF16), wattn_ref[...], preferred_element_type=F32)
    mix = _sigmoid(gs_ref[...]) * ys + _sigmoid(ga_ref[...]) * oa
    h1 = h_ref[...] + jnp.dot(mix.astype(BF16), wout_ref[...], preferred_element_type=F32)
    h1_ref[...] = h1
    ms = jnp.mean(h1 * h1, axis=-1, keepdims=True)
    hn = (h1 * lax.rsqrt(ms + RMS_EPS)) * g2_ref[...]
    hn_ref[...] = hn
    hn_hi = hn.astype(BF16)
    hn_lo = (hn - hn_hi.astype(F32)).astype(BF16)
    logits = jnp.dot(jnp.concatenate([hn_hi, hn_hi, hn_lo], axis=1), wr_ref[...], preferred_element_type=F32)
    route_ref[...] = _route(logits + br_ref[...])


def _route(logits):
    lane = lax.broadcasted_iota(jnp.int32, logits.shape, 1)
    gl = jnp.where(lane < N_EXPERT_GROUPS, logits, -jnp.inf)
    gmax = jnp.max(gl, axis=-1, keepdims=True)
    gsel = jnp.min(jnp.where(gl == gmax, lane, _BIG_LANE), axis=-1, keepdims=True)
    wgrp = 1.0 / jnp.sum(jnp.exp(gl - gmax), axis=-1, keepdims=True)
    elane = lane - ROUTER_EXPERT_LANE0
    in_group = (elane >= gsel * EXPERTS_PER_GROUP) & (elane < (gsel + 1) * EXPERTS_PER_GROUP)
    el = jnp.where(in_group, logits, -jnp.inf)
    t1 = jnp.max(el, axis=-1, keepdims=True)
    i1 = jnp.min(jnp.where(el == t1, lane, _BIG_LANE), axis=-1, keepdims=True)
    el2 = jnp.where(lane == i1, -jnp.inf, el)
    t2 = jnp.max(el2, axis=-1, keepdims=True)
    i2 = jnp.min(jnp.where(el2 == t2, lane, _BIG_LANE), axis=-1, keepdims=True)
    e21 = jnp.exp(t2 - t1)
    w1 = wgrp / (1.0 + e21)
    w2 = w1 * e21
    return jnp.where(lane == ROUTE_E0, (i1 - ROUTER_EXPERT_LANE0).astype(F32),
                     jnp.where(lane == ROUTE_E1, (i2 - ROUTER_EXPERT_LANE0).astype(F32),
                               jnp.where(lane == ROUTE_W0, w1, jnp.where(lane == ROUTE_W1, w2, 0.0))))


def _mix(y, o, u, h, p, tm):
    t = h.shape[0]
    assert t % tm == 0
    row = lambda i: (i, 0)
    const = lambda i: (0, 0)
    return pl.pallas_call(
        _mix_body,
        out_shape=(jax.ShapeDtypeStruct((t, D_MODEL), F32), jax.ShapeDtypeStruct((t, D_MODEL), F32),
                   jax.ShapeDtypeStruct((t, LANES), F32)),
        grid=(t // tm,),
        in_specs=[
            pl.BlockSpec((tm, D_INNER), row),
            pl.BlockSpec((tm, ATTN_DIM), row),
            pl.BlockSpec((tm, D_MODEL), lambda i: (i, U_GS // D_MODEL)),
            pl.BlockSpec((tm, D_MODEL), lambda i: (i, U_GA // D_MODEL)),
            pl.BlockSpec((tm, D_MODEL), row),
            pl.BlockSpec((D_INNER, D_MODEL), const),
            pl.BlockSpec((ATTN_DIM, D_MODEL), const),
            pl.BlockSpec((D_MODEL, D_MODEL), const),
            pl.BlockSpec((1, D_MODEL), const),
            pl.BlockSpec((3 * D_MODEL, LANES), const),
            pl.BlockSpec((1, LANES), const),
        ],
        out_specs=(pl.BlockSpec((tm, D_MODEL), row), pl.BlockSpec((tm, D_MODEL), row),
                   pl.BlockSpec((tm, LANES), row)),
        compiler_params=_cparams(("parallel",), 56),
        name="mix_route",
    )(y, o, u, u, h, p["w_ssd_br"], p["w_attn_br"], p["w_out"], p["norm2_g"], p["w_router"], p["b_router"])


def _mix_params(w_ssd_br, w_attn_br, w_out, norm2_g, w_rg, b_rg, w_re, b_re):
    n_r = N_EXPERT_GROUPS + N_EXPERTS
    w_router = jnp.concatenate([w_rg, w_re, jnp.zeros((D_MODEL, LANES - n_r), F32)], axis=1)
    w_hi = w_router.astype(BF16)
    w_lo = (w_router - w_hi.astype(F32)).astype(BF16)
    w_router = jnp.concatenate([w_hi, w_lo, w_hi], axis=0)
    b_router = jnp.concatenate([b_rg, b_re, jnp.zeros((LANES - n_r,), F32)]).reshape(1, LANES)
    return dict(w_ssd_br=w_ssd_br.astype(BF16), w_attn_br=w_attn_br.astype(BF16), w_out=w_out.astype(BF16),
                norm2_g=norm2_g.reshape(1, D_MODEL), w_router=w_router, b_router=b_router)


def _moe_rank_body(route_ref, pos_ref, cnt_ref, carry_ref, *, tm):
    @pl.when(pl.program_id(0) == 0)
    def _():
        carry_ref[...] = jnp.zeros(carry_ref.shape, F32)

    route = route_ref[...]
    lane = lax.broadcasted_iota(jnp.int32, (tm, LANES), 1).astype(F32)
    hit0 = lane == route[:, ROUTE_E0:ROUTE_E0 + 1]
    hit1 = lane == route[:, ROUTE_E1:ROUTE_E1 + 1]
    onehot = hit0.astype(F32) + hit1.astype(F32)
    before = (lax.broadcasted_iota(jnp.int32, (tm, tm), 1) < lax.broadcasted_iota(jnp.int32, (tm, tm), 0))
    c = jnp.dot(before.astype(BF16), onehot.astype(BF16), preferred_element_type=F32) + carry_ref[...]
    pos0 = jnp.sum(jnp.where(hit0, c, 0.0), axis=-1, keepdims=True)
    pos1 = jnp.sum(jnp.where(hit1, c, 0.0), axis=-1, keepdims=True)
    pos_ref[...] = jnp.where(lane == 0.0, pos0, jnp.where(lane == 1.0, pos1, 0.0))
    total = carry_ref[...] + jnp.sum(onehot, axis=0, keepdims=True)
    carry_ref[...] = total
    cnt_ref[...] = total


def _moe_rank(route, tm):
    t = route.shape[0]
    assert t % tm == 0
    return pl.pallas_call(
        functools.partial(_moe_rank_body, tm=tm),
        out_shape=(jax.ShapeDtypeStruct((t, LANES), F32), jax.ShapeDtypeStruct((1, LANES), F32)),
        grid=(t // tm,),
        in_specs=[pl.BlockSpec((tm, LANES), lambda i: (i, 0))],
        out_specs=(pl.BlockSpec((tm, LANES), lambda i: (i, 0)), pl.BlockSpec((1, LANES), lambda i: (0, 0))),
        scratch_shapes=[pltpu.VMEM((1, LANES), F32)],
        compiler_params=_cparams(("arbitrary",)),
        name="moe_rank",
    )(route)


def _experts_body(be_ref, nu_ref, x_ref, wg_ref, wu_ref, wd_ref, y_ref, wg16_ref, wu16_ref, wd16_ref):
    i = pl.program_id(0)
    prev = be_ref[jnp.maximum(i - 1, 0)]

    @pl.when((i == 0) | (be_ref[i] != prev))
    def _():
        wg16_ref[...] = wg_ref[...].astype(BF16)
        wu16_ref[...] = wu_ref[...].astype(BF16)
        wd16_ref[...] = wd_ref[...].astype(BF16)

    @pl.when(i < nu_ref[0])
    def _():
        sub = x_ref.shape[0] // 2
        rows = [slice(0, sub), slice(sub, 2 * sub)]
        gate_up = []
        for r in rows:
            x16 = x_ref[r, :].astype(BF16)
            gate_up.append((jnp.dot(x16, wg16_ref[...], preferred_element_type=F32),
                            jnp.dot(x16, wu16_ref[...], preferred_element_type=F32)))
        for r, (gt, up) in zip(rows, gate_up):
            y_ref[r, :] = jnp.dot((_silu(gt) * up).astype(BF16), wd16_ref[...], preferred_element_type=F32)

    @pl.when(i >= nu_ref[0])
    def _():
        y_ref[...] = jnp.zeros(y_ref.shape, F32)


def _experts(xs, block_e, n_used, w_gate, w_up, w_down, blk):
    rows = xs.shape[0]
    n_blocks = rows // blk
    grid_spec = pltpu.PrefetchScalarGridSpec(
        num_scalar_prefetch=2,
        grid=(n_blocks,),
        in_specs=[
            pl.BlockSpec((blk, D_MODEL), lambda i, be, nu: (jnp.minimum(i, nu[0] - 1), 0)),
            pl.BlockSpec((None, D_MODEL, D_EXPERT), lambda i, be, nu: (be[i], 0, 0)),
            pl.BlockSpec((None, D_MODEL, D_EXPERT), lambda i, be, nu: (be[i], 0, 0)),
            pl.BlockSpec((None, D_EXPERT, D_MODEL), lambda i, be, nu: (be[i], 0, 0)),
        ],
        out_specs=pl.BlockSpec((blk, D_MODEL), lambda i, be, nu: (i, 0)),
        scratch_shapes=[pltpu.VMEM((D_MODEL, D_EXPERT), BF16), pltpu.VMEM((D_MODEL, D_EXPERT), BF16),
                        pltpu.VMEM((D_EXPERT, D_MODEL), BF16)],
    )
    return pl.pallas_call(
        _experts_body,
        out_shape=jax.ShapeDtypeStruct((rows, D_MODEL), F32),
        grid_spec=grid_spec,
        compiler_params=_cparams(("arbitrary",), 48),
        name="moe_experts",
    )(block_e, n_used, xs, w_gate, w_up, w_down)


MOE_ROWS_PER_BLOCK = 256
MOE_ROWS_PER_BLOCK_FEW_TOKENS = 128
MOE_TOKEN_TILE = 256
DMA_ISSUE_UNROLL = 8


def _row_copy(src, src_row, dst, dst_row, sem):
    return pltpu.make_async_copy(src.at[pl.ds(src_row, 1)], dst.at[pl.ds(dst_row, 1)], sem)


def _dispatch_body(zb_ref, nu_ref, d0_ref, d1_ref, hn_ref, xs_ref, zero_ref, sem, zsem, *, tm, blk, n_blocks):
    @pl.when(pl.program_id(0) == 0)
    def _():
        zero_ref[...] = jnp.zeros(zero_ref.shape, F32)

        def zero_block(b):
            return pltpu.make_async_copy(zero_ref, xs_ref.at[pl.ds(b * blk, blk)], zsem)

        for e in range(N_EXPERTS):
            zero_block(zb_ref[e]).start()

        def start_tail(b, c):
            zero_block(b).start()
            return c

        def wait_tail(b, c):
            zero_block(b).wait()
            return c

        lax.fori_loop(nu_ref[0], n_blocks, start_tail, 0)
        for e in range(N_EXPERTS):
            zero_block(0).wait()
        lax.fori_loop(nu_ref[0], n_blocks, wait_tail, 0)

    for r in range(tm):
        _row_copy(hn_ref, r, xs_ref, d0_ref[r], sem).start()
        _row_copy(hn_ref, r, xs_ref, d1_ref[r], sem).start()

    def drain(r, c):
        _row_copy(hn_ref, 0, xs_ref, 0, sem).wait()
        _row_copy(hn_ref, 0, xs_ref, 0, sem).wait()
        return c

    lax.fori_loop(0, tm, drain, 0, unroll=DMA_ISSUE_UNROLL)


def _dispatch(hn, d0, d1, zero_blocks, n_used, n_blocks, blk, tm):
    t = hn.shape[0]
    assert t % tm == 0
    smem_tile = lambda: pl.BlockSpec((tm,), lambda i, zb, nu: (i,), memory_space=pltpu.SMEM)
    grid_spec = pltpu.PrefetchScalarGridSpec(
        num_scalar_prefetch=2,
        grid=(t // tm,),
        in_specs=[smem_tile(), smem_tile(), pl.BlockSpec((tm, D_MODEL), lambda i, zb, nu: (i, 0))],
        out_specs=pl.BlockSpec(memory_space=pl.ANY),
        scratch_shapes=[pltpu.VMEM((blk, D_MODEL), F32), pltpu.SemaphoreType.DMA, pltpu.SemaphoreType.DMA],
    )
    return pl.pallas_call(
        functools.partial(_dispatch_body, tm=tm, blk=blk, n_blocks=n_blocks),
        out_shape=jax.ShapeDtypeStruct((n_blocks * blk, D_MODEL), F32),
        grid_spec=grid_spec,
        compiler_params=_cparams(("arbitrary",)),
        name="moe_dispatch",
    )(zero_blocks, n_used, d0, d1, hn)


def _combine_body(d0_ref, d1_ref, d0n_ref, d1n_ref, h1_ref, route_ref, g_ref, ys_ref, o_ref, buf, sem, *, tm, n_tiles):
    i = pl.program_id(0)
    slot = i % 2

    def gather(da_ref, db_ref, s):
        for r in range(tm):
            _row_copy(ys_ref, da_ref[r], buf.at[s, 0], r, sem.at[s]).start()
            _row_copy(ys_ref, db_ref[r], buf.at[s, 1], r, sem.at[s]).start()

    @pl.when(i == 0)
    def _():
        gather(d0_ref, d1_ref, 0)

    @pl.when(i + 1 < n_tiles)
    def _():
        gather(d0n_ref, d1n_ref, 1 - slot)

    def drain(r, c):
        _row_copy(ys_ref, 0, buf.at[slot, 0], 0, sem.at[slot]).wait()
        _row_copy(ys_ref, 0, buf.at[slot, 1], 0, sem.at[slot]).wait()
        return c

    lax.fori_loop(0, tm, drain, 0, unroll=DMA_ISSUE_UNROLL)
    route = route_ref[...]
    h = (h1_ref[...] + route[:, ROUTE_W0:ROUTE_W0 + 1] * buf[slot, 0]
         + route[:, ROUTE_W1:ROUTE_W1 + 1] * buf[slot, 1])
    ms = jnp.mean(h * h, axis=-1, keepdims=True)
    o_ref[...] = (h * lax.rsqrt(ms + RMS_EPS)) * g_ref[...]


def _combine(h1, ys, d0, d1, route, g, tm):
    t = h1.shape[0]
    n_tiles = t // tm
    row = lambda i: (i, 0)
    cur = lambda: pl.BlockSpec((tm,), lambda i: (i,), memory_space=pltpu.SMEM)
    nxt = lambda: pl.BlockSpec((tm,), lambda i: (jnp.minimum(i + 1, n_tiles - 1),), memory_space=pltpu.SMEM)
    return pl.pallas_call(
        functools.partial(_combine_body, tm=tm, n_tiles=n_tiles),
        out_shape=jax.ShapeDtypeStruct((t, D_MODEL), F32),
        grid=(n_tiles,),
        in_specs=[cur(), cur(), nxt(), nxt(), pl.BlockSpec((tm, D_MODEL), row), pl.BlockSpec((tm, LANES), row),
                  pl.BlockSpec((1, D_MODEL), lambda i: (0, 0)), pl.BlockSpec(memory_space=pl.ANY)],
        out_specs=pl.BlockSpec((tm, D_MODEL), row),
        scratch_shapes=[pltpu.VMEM((2, 2, tm, D_MODEL), F32), pltpu.SemaphoreType.DMA((2,))],
        compiler_params=_cparams(("arbitrary",)),
        name="moe_combine_norm",
    )(d0, d1, d0, d1, h1, route, g.reshape(1, D_MODEL), ys)


def _moe(hn, h1, route, w_gate, w_up, w_down, final_g, blk):
    t = hn.shape[0]
    pos, cnt = _moe_rank(route, MOE_TOKEN_TILE)
    counts = cnt[0, :N_EXPERTS].astype(jnp.int32)
    padded = (counts + blk - 1) // blk * blk
    ends = jnp.cumsum(padded)
    starts = ends - padded
    expert_ids = jnp.arange(N_EXPERTS, dtype=jnp.int32)

    def dest(e_lane, p_lane):
        e = route[:, e_lane].astype(jnp.int32)
        start = jnp.sum(jnp.where(e[:, None] == expert_ids[None, :], starts[None, :], 0), axis=1)
        return start + pos[:, p_lane].astype(jnp.int32)

    d0 = dest(ROUTE_E0, 0)
    d1 = dest(ROUTE_E1, 1)
    n_blocks = (2 * t + N_EXPERTS * (blk - 1) + blk - 1) // blk
    first_row = jnp.arange(n_blocks, dtype=jnp.int32) * blk
    block_e = jnp.minimum(jnp.sum((ends[None, :] <= first_row[:, None]).astype(jnp.int32), axis=1), N_EXPERTS - 1)
    n_used = (ends[-1] // blk).astype(jnp.int32).reshape(1)
    zero_blocks = jnp.clip((ends - 1) // blk, 0, n_blocks - 1).astype(jnp.int32)
    xs = _dispatch(hn, d0, d1, zero_blocks, n_used, n_blocks, blk, MOE_TOKEN_TILE)
    ys = _experts(xs, block_e, n_used, w_gate, w_up, w_down, blk)
    return _combine(h1, ys, d0, d1, route, final_g, MOE_TOKEN_TILE)


def _conv_history(rows):
    n = rows.shape[0]
    return jnp.concatenate([jnp.zeros((n, CONV_PAD - (CONV_W - 1), CONV_DIM), F32), rows], axis=1)


def _ucols(u, start, width):
    return u[:, start:start + width]


def kernel(x_prompt, x_sample, cache_k, cache_v, cache_logf, state_ssm, state_conv, page_table, meta_tokens, norm1_g, w_in, conv_w, conv_b, dt_bias, a_log, d_skip, ssd_norm_g, b_forget, w_ssd_br, w_attn_br, w_out, norm2_g, w_router_group, b_router_group, w_router_expert, b_router_expert, w_exp_gate, w_exp_up, w_exp_down, final_norm_g):
    nb, sl, _ = x_prompt.shape
    sb, ss, _ = x_sample.shape
    assert w_in.shape[0] == 1, "single-layer step"
    ly = 0
    xp = x_prompt.reshape(nb * sl, D_MODEL)
    xs = x_sample.reshape(sb * ss, D_MODEL)

    wp = _pack_w_in(w_in[ly])
    u_p = _inproj(xp, norm1_g[ly], wp)
    u_s = _inproj(xs, norm1_g[ly], wp)
    u_m = _inproj(meta_tokens, norm1_g[ly], wp)

    sp = _ssd_params(conv_w[ly], conv_b[ly], dt_bias[ly], a_log[ly], d_skip[ly], ssd_norm_g[ly])
    zero_hist = jnp.zeros((1, CONV_PAD, CONV_DIM), F32)
    zero_state = jnp.zeros((1, D_INNER, D_STATE), F32)
    _, h_meta = _ssd(u_m, 1, N_META, N_META, zero_hist, zero_state, sp, BF16)
    hist_meta = _conv_history(_ucols(u_m, U_XBC, CONV_DIM)[None, N_META - (CONV_W - 1):])
    y_p, ssm_p = _ssd(u_p, nb, sl, SSD_CHUNK, hist_meta, h_meta, sp, BF16)
    y_s, ssm_s = _ssd(u_s, sb, ss, ss, _conv_history(state_conv[ly]),
                      state_ssm[ly].reshape(sb, D_INNER, D_STATE), sp, F32)

    slab_m, _, kp_m, vpt_m, kt_m, vt_m = _attn_prep(u_m, b_forget[ly], N_META, N_META, rel_to_last=True)
    slab_p, qp_p, kp_p, vpt_p, kt_p, vt_p = _attn_prep(u_p, b_forget[ly], nb * sl, sl)
    o_p = _fox_prompt(qp_p, kp_p, vpt_p, kp_m, vpt_m, nb, sl, ATTN_TQ)
    o_s, slab_s = _fox_sample(u_s, cache_k[ly], cache_v[ly], cache_logf[ly], page_table, b_forget[ly], sb, ss,
                              SAMPLE_PAGES_PER_STEP)

    mp = _mix_params(w_ssd_br[ly], w_attn_br[ly], w_out[ly], norm2_g[ly], w_router_group[ly], b_router_group[ly],
                     w_router_expert[ly], b_router_expert[ly])
    h1_p, hn_p, route_p = _mix(y_p, o_p, u_p, xp, mp, 256)
    h1_s, hn_s, route_s = _mix(y_s, o_s, u_s, xs, mp, 256)
    out_p = _moe(hn_p, h1_p, route_p, w_exp_gate[ly], w_exp_up[ly], w_exp_down[ly], final_norm_g,
                 MOE_ROWS_PER_BLOCK)
    out_s = _moe(hn_s, h1_s, route_s, w_exp_gate[ly], w_exp_up[ly], w_exp_down[ly], final_norm_g,
                 MOE_ROWS_PER_BLOCK_FEW_TOKENS)

    def with_meta(meta_rows, rows, width):
        m = jnp.broadcast_to(meta_rows[None], (nb, N_META, width))
        return jnp.concatenate([m, rows.reshape(nb, sl, width)], axis=1)[None]

    def kv_with_meta(t_meta, t_rows):
        m = jnp.broadcast_to(t_meta.reshape(ATTN_KV_HEADS, HEAD_DIM, 1, N_META), (ATTN_KV_HEADS, HEAD_DIM, nb, N_META))
        full = jnp.concatenate([m, t_rows.reshape(ATTN_KV_HEADS, HEAD_DIM, nb, sl)], axis=3)
        return jnp.transpose(full, (2, 3, 0, 1))[None]

    lf_m = slab_m[:, SSD_HEADS:SSD_HEADS + ATTN_HEADS]
    lf_p = slab_p[:, SSD_HEADS:SSD_HEADS + ATTN_HEADS]
    lf_s = slab_s[:, SSD_HEADS:SSD_HEADS + ATTN_HEADS]
    kv_shape_s = (1, sb, ss, ATTN_KV_HEADS, HEAD_DIM)
    state_shape = (SSD_HEADS, SSD_HEADDIM, D_STATE)
    tail = CONV_W - 1
    return (
        out_p.reshape(nb, sl, D_MODEL),
        out_s.reshape(sb, ss, D_MODEL),
        kv_with_meta(kt_m, kt_p),
        kv_with_meta(vt_m, vt_p),
        with_meta(lf_m, lf_p, ATTN_HEADS),
        ssm_p.reshape((1, nb) + state_shape),
        u_p.reshape(nb, sl, U_COLS)[None, :, sl - tail:, U_XBC:U_XBC + CONV_DIM],
        _ucols(u_s, U_K, KV_DIM).reshape(kv_shape_s),
        _ucols(u_s, U_V, KV_DIM).reshape(kv_shape_s),
        lf_s.reshape(1, sb, ss, ATTN_HEADS),
        ssm_s.reshape((1, sb) + state_shape),
        u_s.reshape(sb, ss, U_COLS)[None, :, ss - tail:, U_XBC:U_XBC + CONV_DIM],
    )
```

```python
import functools

import numpy as np
import jax
import jax.numpy as jnp
from jax import lax
from jax.experimental import pallas as pl
from jax.experimental.pallas import tpu as pltpu

F32 = jnp.float32
BF16 = jnp.bfloat16
HI = lax.Precision.HIGHEST

D_MODEL = 1024
N_META = 16
RMS_EPS = 1e-6
D_INNER = 2048
SSD_HEADDIM = 64
SSD_HEADS = 32
SSD_GROUPS = 8
HEADS_PER_GROUP = SSD_HEADS // SSD_GROUPS
GROUP_W = HEADS_PER_GROUP * SSD_HEADDIM
D_STATE = 128
CONV_W = 4
CONV_DIM = 4096
SSD_CHUNK = 128
ATTN_HEADS = 16
ATTN_KV_HEADS = 4
HEAD_DIM = 64
ATTN_REP = 4
ATTN_DIM = 1024
KV_DIM = 256
ATTN_SCALE = HEAD_DIM ** -0.5
PAGE_SIZE = 128
N_EXPERT_GROUPS = 4
EXPERTS_PER_GROUP = 8
N_EXPERTS = 32
D_EXPERT = 512
LANES = 128

U_XBC = 0
U_Z = 4096
U_Q = 6144
U_GS = 7168
U_GA = 8192
U_K = 9216
U_V = 9472
U_DTF = 9728
U_COLS = 9856
U_COLS_PADDED = 10240
INPROJ_TN = 1024
INPROJ_TM = 2048


def _cparams(sem, vmem_mb=None):
    kw = dict(dimension_semantics=sem)
    if vmem_mb is not None:
        kw["vmem_limit_bytes"] = vmem_mb * 1024 * 1024
    return pltpu.CompilerParams(**kw)


def _inproj_body(x_ref, g_ref, w_ref, o_ref, xn_ref):
    @pl.when(pl.program_id(1) == 0)
    def _():
        x = x_ref[...]
        ms = jnp.mean(x * x, axis=-1, keepdims=True)
        xn_ref[...] = ((x * lax.rsqrt(ms + RMS_EPS)) * g_ref[...]).astype(BF16)

    o_ref[...] = jnp.dot(xn_ref[...], w_ref[...], preferred_element_type=F32)


def _inproj(x, g, wp):
    t = x.shape[0]
    tm = min(t, INPROJ_TM)
    assert t % tm == 0
    return pl.pallas_call(
        _inproj_body,
        out_shape=jax.ShapeDtypeStruct((t, U_COLS), F32),
        grid=(t // tm, U_COLS_PADDED // INPROJ_TN),
        in_specs=[
            pl.BlockSpec((tm, D_MODEL), lambda i, j: (i, 0)),
            pl.BlockSpec((1, D_MODEL), lambda i, j: (0, 0)),
            pl.BlockSpec((D_MODEL, INPROJ_TN), lambda i, j: (0, j)),
        ],
        out_specs=pl.BlockSpec((tm, INPROJ_TN), lambda i, j: (i, j)),
        scratch_shapes=[pltpu.VMEM((tm, D_MODEL), BF16)],
        compiler_params=_cparams(("parallel", "arbitrary"), 56),
        name="inproj",
    )(x, g.reshape(1, D_MODEL), wp)


CONV_PAD = 8


def _conv_silu_rows(staged, rows, w_ref, b_ref):
    acc = b_ref[...] + staged[CONV_PAD:CONV_PAD + rows, :] * w_ref[CONV_W - 1:CONV_W, :]
    for k in range(CONV_W - 1):
        off = CONV_PAD - (CONV_W - 1) + k
        acc = acc + pltpu.roll(staged, CONV_PAD + rows - off, 0)[0:rows, :] * w_ref[k:k + 1, :]
    return _silu(acc)


def _pack_w_in(w_in):
    o = np.cumsum([0, D_INNER, CONV_DIM, SSD_HEADS, ATTN_DIM, KV_DIM, KV_DIM, ATTN_HEADS, D_MODEL, D_MODEL])
    z, xbc, dt, q, k, v, f, gs, ga = [w_in[:, o[i]:o[i + 1]] for i in range(9)]
    pad_dtf = jnp.zeros((D_MODEL, LANES - SSD_HEADS - ATTN_HEADS), w_in.dtype)
    pad = jnp.zeros((D_MODEL, U_COLS_PADDED - U_COLS), w_in.dtype)
    return jnp.concatenate([xbc, z, q, gs, ga, k, v, dt, f, pad_dtf, pad], axis=1).astype(BF16)


def _log_sigmoid(x):
    return jnp.minimum(x, 0.0) - jnp.log1p(jnp.exp(-jnp.abs(x)))


LOG2E = 1.4426950408889634
QK_ONES_LANE = HEAD_DIM
QK_CK_LANE = HEAD_DIM + 3
V_ONES_ROW = HEAD_DIM
N_SPLIT = 3
ATTN_TQ = 256
QK_AHEAD = 4


def _placement_matrices():
    eq = np.zeros((LANES, ATTN_HEADS * LANES), np.float32)
    ek = np.zeros((LANES, ATTN_KV_HEADS * LANES), np.float32)
    for h in range(ATTN_HEADS):
        g, r = divmod(h, ATTN_REP)
        for s in range(N_SPLIT):
            eq[s * ATTN_HEADS + h, h * LANES + QK_ONES_LANE + s] = 1.0
            ek[s * ATTN_HEADS + h, g * LANES + QK_CK_LANE + s * ATTN_REP + r] = 1.0
    cq = np.zeros((1, ATTN_HEADS * LANES), np.float32)
    for h in range(ATTN_HEADS):
        r = h % ATTN_REP
        for s in range(N_SPLIT):
            cq[0, h * LANES + QK_CK_LANE + s * ATTN_REP + r] = -1.0
    return jnp.asarray(eq, BF16), jnp.asarray(ek, BF16), jnp.asarray(cq)


def _transpose_rows(x, q):
    if q < LANES:
        x = jnp.concatenate([x, jnp.zeros((LANES - q, x.shape[1]), x.dtype)], axis=0)
    return x.T[:, :q]


def _split3(x):
    hi = x.astype(BF16)
    r1 = x - hi.astype(F32)
    mid = r1.astype(BF16)
    lo = (r1 - mid.astype(F32)).astype(BF16)
    return hi, mid, lo


def _split3_stacked(x, axis):
    return jnp.concatenate([t.astype(F32) for t in _split3(x)], axis=axis).astype(BF16)


def _attn_prep_body(q_ref, k_ref, v_ref, dtf_ref, bf_ref, eq_ref, ek_ref, cq_ref, slab_ref, qp_ref, kp_ref, vpt_ref,
                    kt_ref, vt_ref, carry_ref, *, tiles_per_seq, tl, rel_to_last):
    i = pl.program_id(0)

    @pl.when(i % tiles_per_seq == 0)
    def _():
        carry_ref[...] = jnp.zeros_like(carry_ref)

    lf = _log_sigmoid(dtf_ref[...] + bf_ref[...])
    row = lax.broadcasted_iota(jnp.int32, (tl, tl), 0)
    col = lax.broadcasted_iota(jnp.int32, (tl, tl), 1)
    c = jnp.dot((col <= row).astype(F32), lf, precision=HI, preferred_element_type=F32) + carry_ref[...]
    carry_ref[...] = c[tl - 1:tl, :]
    lane = lax.broadcasted_iota(jnp.int32, (tl, LANES), 1)
    c16 = jnp.where(lane < ATTN_HEADS, pltpu.roll(c, LANES - SSD_HEADS, 1), 0.0)
    slab_ref[...] = jnp.where((lane >= SSD_HEADS) & (lane < SSD_HEADS + ATTN_HEADS), lf, c16)
    if rel_to_last:
        c16 = c16 - c16[tl - 1:tl, :]
    hi, mid, lo = _split3(c16 * LOG2E)
    x = jnp.where(lane < ATTN_HEADS, hi.astype(F32),
                  jnp.where(lane < 2 * ATTN_HEADS, pltpu.roll(mid.astype(F32), ATTN_HEADS, 1),
                            pltpu.roll(lo.astype(F32), 2 * ATTN_HEADS, 1))).astype(BF16)
    q_extra = jnp.dot(x, eq_ref[...], preferred_element_type=F32) + cq_ref[...]
    k_extra = jnp.dot(x, ek_ref[...], preferred_element_type=F32)
    low = lane < HEAD_DIM
    for h in range(ATTN_HEADS):
        qx = q_ref[:, (h // 2) * LANES:(h // 2 + 1) * LANES]
        if h % 2:
            qx = pltpu.roll(qx, HEAD_DIM, 1)
        tile = jnp.where(low, qx * (ATTN_SCALE * LOG2E), q_extra[:, h * LANES:(h + 1) * LANES])
        qp_ref[:, h * LANES:(h + 1) * LANES] = tile.astype(BF16)
    ones_k = ((lane >= QK_ONES_LANE) & (lane < QK_CK_LANE)).astype(F32)
    ones_v = (lane == V_ONES_ROW).astype(F32)
    for g in range(ATTN_KV_HEADS):
        kx = k_ref[:, (g // 2) * LANES:(g // 2 + 1) * LANES]
        vx = v_ref[:, (g // 2) * LANES:(g // 2 + 1) * LANES]
        if g % 2:
            kx = pltpu.roll(kx, HEAD_DIM, 1)
            vx = pltpu.roll(vx, HEAD_DIM, 1)
        kp_ref[:, g * LANES:(g + 1) * LANES] = jnp.where(low, kx, k_extra[:, g * LANES:(g + 1) * LANES] + ones_k
                                                         ).astype(BF16)
        vpt_ref[g * LANES:(g + 1) * LANES, :] = _transpose_rows(jnp.where(low, vx, ones_v), tl).astype(BF16)
    kt_ref[...] = _transpose_rows(k_ref[...], tl)
    vt_ref[...] = _transpose_rows(v_ref[...], tl)


def _attn_prep(u, b_forget, n_tokens, seq_len, rel_to_last=False):
    tl = min(n_tokens, 256)
    assert n_tokens % tl == 0 and seq_len % tl == 0
    bf = jnp.zeros((1, LANES), F32).at[0, SSD_HEADS:SSD_HEADS + ATTN_HEADS].set(b_forget)
    eq, ek, cq = _placement_matrices()
    const = lambda i: (0, 0)
    return pl.pallas_call(
        functools.partial(_attn_prep_body, tiles_per_seq=seq_len // tl, tl=tl, rel_to_last=rel_to_last),
        out_shape=(jax.ShapeDtypeStruct((n_tokens, LANES), F32),
                   jax.ShapeDtypeStruct((n_tokens, ATTN_HEADS * LANES), BF16),
                   jax.ShapeDtypeStruct((n_tokens, ATTN_KV_HEADS * LANES), BF16),
                   jax.ShapeDtypeStruct((ATTN_KV_HEADS * LANES, n_tokens), BF16),
                   jax.ShapeDtypeStruct((KV_DIM, n_tokens), F32),
                   jax.ShapeDtypeStruct((KV_DIM, n_tokens), F32)),
        grid=(n_tokens // tl,),
        in_specs=[
            pl.BlockSpec((tl, ATTN_DIM), lambda i: (i, U_Q // ATTN_DIM)),
            pl.BlockSpec((tl, KV_DIM), lambda i: (i, U_K // KV_DIM)),
            pl.BlockSpec((tl, KV_DIM), lambda i: (i, U_V // KV_DIM)),
            pl.BlockSpec((tl, LANES), lambda i: (i, U_DTF // LANES)),
            pl.BlockSpec((1, LANES), const),
            pl.BlockSpec((LANES, ATTN_HEADS * LANES), const),
            pl.BlockSpec((LANES, ATTN_KV_HEADS * LANES), const),
            pl.BlockSpec((1, ATTN_HEADS * LANES), const),
        ],
        out_specs=(pl.BlockSpec((tl, LANES), lambda i: (i, 0)),
                   pl.BlockSpec((tl, ATTN_HEADS * LANES), lambda i: (i, 0)),
                   pl.BlockSpec((tl, ATTN_KV_HEADS * LANES), lambda i: (i, 0)),
                   pl.BlockSpec((ATTN_KV_HEADS * LANES, tl), lambda i: (0, i)),
                   pl.BlockSpec((KV_DIM, tl), lambda i: (0, i)),
                   pl.BlockSpec((KV_DIM, tl), lambda i: (0, i))),
        scratch_shapes=[pltpu.VMEM((1, LANES), F32)],
        compiler_params=_cparams(("arbitrary",), 40),
        name="attn_prep",
    )(u, u, u, u, bf, eq, ek, cq)


def _silu(x):
    return x * (1.0 / (1.0 + jnp.exp(-x)))


def _softplus(x):
    return jnp.maximum(x, 0.0) + jnp.log1p(jnp.exp(-jnp.abs(x)))


def _ssd_body(xbc_ref, z_ref, dtf_ref, conv0_ref, h0_ref, cw_ref, cb_ref, dtb_ref, alog_ref, dskip_ref,
              gn_ref, e_ref, y_ref, hout_ref, xconv_ref, ht_ref, *, q, n_chunks):
    c = pl.program_id(1)

    @pl.when(c == 0)
    def _():
        xconv_ref[0:CONV_PAD, :] = conv0_ref[0]
        for g in range(SSD_GROUPS):
            ht_ref[g] = h0_ref[0, g * GROUP_W:(g + 1) * GROUP_W, :].T

    xconv_ref[CONV_PAD:CONV_PAD + q, :] = xbc_ref[...]
    xc = _conv_silu_rows(xconv_ref[...], q, cw_ref, cb_ref)
    xconv_ref[CONV_PAD - (CONV_W - 1):CONV_PAD, :] = xconv_ref[CONV_PAD + q - (CONV_W - 1):CONV_PAD + q, :]

    dt = _softplus(dtf_ref[...] + dtb_ref[...])
    a = -jnp.exp(alog_ref[...])
    row = lax.broadcasted_iota(jnp.int32, (q, q), 0)
    col = lax.broadcasted_iota(jnp.int32, (q, q), 1)
    causal = col <= row
    tri = causal.astype(BF16)
    acum = jnp.dot(jnp.concatenate([tri] * N_SPLIT, axis=1), _split3_stacked(dt * a, 0),
                   preferred_element_type=F32)
    acum_t = _transpose_rows(acum, q)
    a_last = acum[q - 1:q, :]
    fac = jnp.concatenate([jnp.exp(acum), jnp.exp(a_last - acum) * dt, dt], axis=0)
    fac = jnp.dot(_split3_stacked(fac, 1), e_ref[...], preferred_element_type=F32)
    ea_full, wst_full, dt_full = fac[0:q], fac[q:2 * q], fac[2 * q:3 * q]

    def group_bc(g):
        bg = xc[:, D_INNER + g * D_STATE:D_INNER + (g + 1) * D_STATE]
        cg = xc[:, D_INNER + SSD_GROUPS * D_STATE + g * D_STATE:D_INNER + SSD_GROUPS * D_STATE + (g + 1) * D_STATE]
        cg16 = cg.astype(BF16)
        cbm = lax.dot_general(cg16, bg.astype(BF16), (((1,), (1,)), ((), ())), preferred_element_type=F32)
        return bg, cg16, cbm

    nxt = group_bc(0)
    for g in range(SSD_GROUPS):
        gs = slice(g * GROUP_W, (g + 1) * GROUP_W)
        xg = xc[:, gs]
        bg, cg16, cbm = nxt
        if g + 1 < SSD_GROUPS:
            nxt = group_bc(g + 1)
        htg = ht_ref[g]
        yoff = jnp.dot(cg16, htg.astype(BF16), preferred_element_type=F32) * ea_full[:, gs]
        xw = (xg * wst_full[:, gs]).astype(BF16)
        bgt = _transpose_rows(bg, q).astype(BF16)
        st = jnp.dot(bgt, xw, preferred_element_type=F32)
        ht_ref[g] = ea_full[q - 1:q, gs] * htg + st
        xdt = (xg * dt_full[:, gs]).astype(BF16)
        yd = []
        for r in range(HEADS_PER_GROUP):
            h = g * HEADS_PER_GROUP + r
            seg = acum[:, h:h + 1] - acum_t[h:h + 1, :]
            m = cbm * jnp.exp(jnp.where(causal, seg, -jnp.inf))
            yd.append(jnp.dot(m.astype(BF16), xdt[:, r * SSD_HEADDIM:(r + 1) * SSD_HEADDIM],
                              preferred_element_type=F32))
        yd = jnp.concatenate(yd, axis=1)
        yg = yd + yoff + dskip_ref[:, gs] * xg
        yz = yg * _silu(z_ref[:, gs])
        ms = jnp.mean(yz * yz, axis=-1, keepdims=True)
        y_ref[:, gs] = (yz * lax.rsqrt(ms + RMS_EPS) * gn_ref[:, gs]).astype(y_ref.dtype)

    @pl.when(c == n_chunks - 1)
    def _():
        for g in range(SSD_GROUPS):
            hout_ref[0, g * GROUP_W:(g + 1) * GROUP_W, :] = ht_ref[g].T


def _head_expand_matrix():
    e = np.zeros((LANES, D_INNER), np.float32)
    for h in range(SSD_HEADS):
        e[h, h * SSD_HEADDIM:(h + 1) * SSD_HEADDIM] = 1.0
    return jnp.asarray(np.tile(e, (N_SPLIT, 1)), BF16)


def _ssd(u, n_seq, seq_len, q, conv0, h0, p, y_dtype):
    n_chunks = seq_len // q
    assert seq_len % q == 0 and q % 8 == 0
    conv_ix = (lambda b, c: (b, 0, 0)) if conv0.shape[0] == n_seq and n_seq > 1 else (lambda b, c: (0, 0, 0))
    h_ix = (lambda b, c: (b, 0, 0)) if h0.shape[0] == n_seq and n_seq > 1 else (lambda b, c: (0, 0, 0))
    const2 = lambda b, c: (0, 0)
    return pl.pallas_call(
        functools.partial(_ssd_body, q=q, n_chunks=n_chunks),
        out_shape=(jax.ShapeDtypeStruct((n_seq * seq_len, D_INNER), y_dtype),
                   jax.ShapeDtypeStruct((n_seq, D_INNER, D_STATE), F32)),
        grid=(n_seq, n_chunks),
        in_specs=[
            pl.BlockSpec((q, CONV_DIM), lambda b, c: (b * n_chunks + c, U_XBC // CONV_DIM)),
            pl.BlockSpec((q, D_INNER), lambda b, c: (b * n_chunks + c, U_Z // D_INNER)),
            pl.BlockSpec((q, LANES), lambda b, c: (b * n_chunks + c, U_DTF // LANES)),
            pl.BlockSpec((1, CONV_PAD, CONV_DIM), conv_ix),
            pl.BlockSpec((1, D_INNER, D_STATE), h_ix),
            pl.BlockSpec((CONV_W, CONV_DIM), const2),
            pl.BlockSpec((1, CONV_DIM), const2),
            pl.BlockSpec((1, LANES), const2),
            pl.BlockSpec((1, LANES), const2),
            pl.BlockSpec((1, D_INNER), const2),
            pl.BlockSpec((1, D_INNER), const2),
            pl.BlockSpec((N_SPLIT * LANES, D_INNER), const2),
        ],
        out_specs=(pl.BlockSpec((q, D_INNER), lambda b, c: (b * n_chunks + c, 0)),
                   pl.BlockSpec((1, D_INNER, D_STATE), lambda b, c: (b, 0, 0))),
        scratch_shapes=[pltpu.VMEM((CONV_PAD + q, CONV_DIM), F32),
                        pltpu.VMEM((SSD_GROUPS, D_STATE, GROUP_W), F32)],
        compiler_params=_cparams(("parallel", "arbitrary"), 48),
        name="ssd_q%d" % q,
    )(u, u, u, conv0, h0, p["conv_w"], p["conv_b"], p["dt_bias"], p["a_log"], p["d_skip"], p["ssd_norm_g"],
      p["head_expand"])


def _ssd_params(conv_w, conv_b, dt_bias, a_log, d_skip, ssd_norm_g):
    pad32 = lambda v: jnp.zeros((1, LANES), F32).at[0, :SSD_HEADS].set(v)
    return dict(conv_w=conv_w, conv_b=conv_b.reshape(1, CONV_DIM), dt_bias=pad32(dt_bias), a_log=pad32(a_log),
                d_skip=jnp.repeat(d_skip, SSD_HEADDIM).reshape(1, D_INNER),
                ssd_norm_g=ssd_norm_g.reshape(1, D_INNER), head_expand=_head_expand_matrix())


def _fox_prompt_body(qi_ref, kj_ref, qp_ref, kp_ref, vpt_ref, kpm_ref, vptm_ref, o_ref, m_ref, acc_ref, *, tq, tk):
    step = pl.program_id(1)
    i = qi_ref[step]
    j = kj_ref[step]
    nt = (((1,), (1,)), ((), ()))

    def scores(h, kp):
        g = h // ATTN_REP
        return lax.dot_general(kp[:, g * LANES:(g + 1) * LANES], qp_ref[:, h * LANES:(h + 1) * LANES], nt,
                               preferred_element_type=F32)

    def attend_all(kp, vpt, mask, ahead=QK_AHEAD):
        pending = [scores(h, kp) for h in range(ahead)]
        for h in range(ATTN_HEADS):
            g = h // ATTN_REP
            s = pending.pop(0)
            if h + ahead < ATTN_HEADS:
                pending.append(scores(h + ahead, kp))
            if mask is not None:
                s = jnp.where(mask, s, -jnp.inf)
            m_prev = m_ref[h, 0:1, :]
            m_new = jnp.maximum(m_prev, jnp.max(s, axis=0, keepdims=True))
            alpha = jnp.exp2(m_prev - m_new)
            p = jnp.exp2(s - m_new).astype(BF16)
            m_ref[h, 0:1, :] = m_new
            pv = jnp.dot(vpt[g * LANES:(g + 1) * LANES, :], p, preferred_element_type=F32)
            acc_ref[h] = alpha * acc_ref[h] + pv

    @pl.when(j == 0)
    def _():
        kpm = kpm_ref[...]
        vptm = vptm_ref[...]
        meta_scores = [scores(h, kpm) for h in range(ATTN_HEADS)]
        for h, s in enumerate(meta_scores):
            g = h // ATTN_REP
            m_new = jnp.max(s, axis=0, keepdims=True)
            m_ref[h, 0:1, :] = m_new
            acc_ref[h] = jnp.dot(vptm[g * LANES:(g + 1) * LANES, :], jnp.exp2(s - m_new).astype(BF16),
                                 preferred_element_type=F32)

    @pl.when(j < i)
    def _():
        attend_all(kp_ref[...], vpt_ref[...], None)

    @pl.when(j == i)
    def _():
        key = lax.broadcasted_iota(jnp.int32, (tk, tq), 0)
        qry = lax.broadcasted_iota(jnp.int32, (tk, tq), 1)
        attend_all(kp_ref[...], vpt_ref[...], key <= qry)
        for h in range(ATTN_HEADS):
            acc = acc_ref[h]
            o_t = acc[0:HEAD_DIM, :] * (1.0 / acc[V_ONES_ROW:V_ONES_ROW + 1, :])
            o_ref[:, h * HEAD_DIM:(h + 1) * HEAD_DIM] = o_t.T.astype(o_ref.dtype)


def _fox_prompt(qp, kp, vpt, kp_meta, vpt_meta, n_seq, seq_len, tq):
    nq = seq_len // tq
    assert seq_len % tq == 0
    qi = np.array([i for i in range(nq) for _ in range(i + 1)], np.int32)
    kj = np.array([j for i in range(nq) for j in range(i + 1)], np.int32)
    n_meta = kp_meta.shape[0]
    qw = ATTN_HEADS * LANES
    kw = ATTN_KV_HEADS * LANES
    grid_spec = pltpu.PrefetchScalarGridSpec(
        num_scalar_prefetch=2,
        grid=(n_seq, len(qi)),
        in_specs=[
            pl.BlockSpec((tq, qw), lambda b, s, qi, kj: (b * nq + qi[s], 0)),
            pl.BlockSpec((tq, kw), lambda b, s, qi, kj: (b * nq + kj[s], 0)),
            pl.BlockSpec((kw, tq), lambda b, s, qi, kj: (0, b * nq + kj[s])),
            pl.BlockSpec((n_meta, kw), lambda b, s, qi, kj: (0, 0)),
            pl.BlockSpec((kw, n_meta), lambda b, s, qi, kj: (0, 0)),
        ],
        out_specs=pl.BlockSpec((tq, ATTN_DIM), lambda b, s, qi, kj: (b * nq + qi[s], 0)),
        scratch_shapes=[pltpu.VMEM((ATTN_HEADS, 8, tq), F32),
                        pltpu.VMEM((ATTN_HEADS, LANES, tq), F32)],
    )
    return pl.pallas_call(
        functools.partial(_fox_prompt_body, tq=tq, tk=tq),
        out_shape=jax.ShapeDtypeStruct((n_seq * seq_len, ATTN_DIM), BF16),
        grid_spec=grid_spec,
        compiler_params=_cparams(("parallel", "arbitrary"), 48),
        name="fox_prompt",
    )(jnp.asarray(qi), jnp.asarray(kj), qp, kp, vpt, kp_meta, vpt_meta)


def _fox_sample_body(pt_ref, qaug_ref, kn_ref, vn_ref, dtf_ref, bf_ref, et_ref, suf_ref, ck_hbm, cv_hbm, clf_hbm,
                     o_ref, lfo_ref, kbuf, vbuf, lfbuf, sem, m_ref, l_ref, acc_ref, carry_ref, cnrow_ref,
                     *, npp, n_steps, n_pages, n_seq, s_new):
    seq = pl.program_id(0)
    step = pl.program_id(1)
    total = n_seq * n_steps
    t = seq * n_steps + step
    slot = t % SAMPLE_DMA_SLOTS
    rows = ATTN_HEADS * s_new
    qaug = qaug_ref[...]
    nt = (((1,), (1,)), ((), ()))

    def page_copies(t_i, s):
        seq_i = t_i // n_steps
        step_i = t_i - seq_i * n_steps
        first = seq_i * n_pages + n_pages - (step_i + 1) * npp
        out = []
        for i in range(npp):
            page = pt_ref[first + i]
            out.append(pltpu.make_async_copy(ck_hbm.at[page], kbuf.at[s, i], sem.at[s]))
            out.append(pltpu.make_async_copy(cv_hbm.at[page], vbuf.at[s, i], sem.at[s]))
            out.append(pltpu.make_async_copy(clf_hbm.at[page], lfbuf.at[s, i], sem.at[s]))
        return out

    @pl.when(t == 0)
    def _():
        for t_i in range(SAMPLE_DMA_SLOTS - 1):
            for c in page_copies(t_i, t_i):
                c.start()

    @pl.when(step == 0)
    def _():
        lane = lax.broadcasted_iota(jnp.int32, (s_new, LANES), 1)
        lfn = _log_sigmoid(dtf_ref[...] + bf_ref[...])
        lfo_ref[...] = jnp.where((lane >= SSD_HEADS) & (lane < SSD_HEADS + ATTN_HEADS), lfn, 0.0)
        lf16 = lfn[:, SSD_HEADS:SSD_HEADS + ATTN_HEADS]
        tri = (lax.broadcasted_iota(jnp.int32, (s_new, s_new), 1)
               <= lax.broadcasted_iota(jnp.int32, (s_new, s_new), 0)).astype(F32)
        cn = jnp.dot(tri, lf16, precision=HI, preferred_element_type=F32) * LOG2E
        cne = lax.dot_general(et_ref[...], cn, nt, precision=HI, preferred_element_type=F32)
        trow = lax.broadcasted_iota(jnp.int32, (rows, s_new), 0) % s_new
        tcol = lax.broadcasted_iota(jnp.int32, (rows, s_new), 1)
        cn_row = jnp.sum(jnp.where(tcol == trow, cne, 0.0), axis=-1, keepdims=True)
        cnrow_ref[...] = jnp.broadcast_to(cn_row, cnrow_ref.shape)
        ss = lax.dot_general(qaug[:, 0:KV_DIM], kn_ref[...].astype(BF16), nt, preferred_element_type=F32)
        ss = jnp.where(tcol <= trow, ss + cn_row - cne, -jnp.inf)
        m = jnp.max(ss, axis=-1, keepdims=True)
        p = jnp.exp2(ss - m)
        m_ref[...] = jnp.broadcast_to(m, m_ref.shape)
        l_ref[...] = jnp.broadcast_to(jnp.sum(p, axis=-1, keepdims=True), l_ref.shape)
        acc_ref[...] = jnp.dot(p.astype(BF16), vn_ref[...].astype(BF16), preferred_element_type=F32)
        carry_ref[...] = jnp.zeros(carry_ref.shape, F32)

    for c in page_copies(t, slot):
        c.wait()
    lf_all = jnp.concatenate([lfbuf[slot, i] for i in range(npp)], axis=0)
    in_page = jnp.dot(jnp.concatenate(_split3(lf_all), axis=1), suf_ref[...], preferred_element_type=F32)
    page_total = jnp.sum(lf_all, axis=1, keepdims=True)
    carry = carry_ref[:, 0:1]
    scores = [None] * npp
    for i in reversed(range(npp)):
        hs = slice(i * ATTN_HEADS, (i + 1) * ATTN_HEADS)
        r_hi, r_mid, r_lo = _split3((in_page[hs] + carry) * LOG2E)
        carry = carry + page_total[hs]
        kt = kbuf[slot, i].reshape(KV_DIM, PAGE_SIZE).astype(BF16)
        k_aug = jnp.concatenate([kt, r_hi, r_mid, r_lo], axis=0)
        scores[i] = jnp.dot(qaug, k_aug, preferred_element_type=F32)
    carry_ref[...] = jnp.broadcast_to(carry, carry_ref.shape)
    s_all = jnp.concatenate(scores, axis=1) + cnrow_ref[:, 0:1]
    vt_all = jnp.concatenate([vbuf[slot, i].reshape(KV_DIM, PAGE_SIZE).astype(BF16) for i in range(npp)], axis=1)
    m_prev = m_ref[...]
    m_new = jnp.maximum(m_prev, jnp.max(s_all, axis=-1, keepdims=True))
    alpha = jnp.exp2(m_prev - m_new)
    p = jnp.exp2(s_all - m_new[:, 0:1])
    l_ref[...] = alpha * l_ref[...] + jnp.sum(p, axis=-1, keepdims=True)
    m_ref[...] = m_new
    pv = lax.dot_general(p.astype(BF16), vt_all, nt, preferred_element_type=F32)
    acc_ref[...] = jnp.concatenate([alpha, alpha], axis=1) * acc_ref[...] + pv
    ahead = t + (SAMPLE_DMA_SLOTS - 1)
    for c in page_copies(jnp.where(ahead < total, ahead, t), ahead % SAMPLE_DMA_SLOTS):
        c.start()

    @pl.when(step == n_steps - 1)
    def _():
        inv = 1.0 / l_ref[...]
        o_ref[...] = acc_ref[...] * jnp.concatenate([inv, inv], axis=1)

    @pl.when(t == total - 1)
    def _():
        for back in range(SAMPLE_DMA_SLOTS - 1):
            for c in page_copies(t, (t + 1 + back) % SAMPLE_DMA_SLOTS):
                c.wait()


SAMPLE_PAGES_PER_STEP = 32
SAMPLE_DMA_SLOTS = 3


def _fox_sample(u_s, cache_k, cache_v, cache_logf, page_table, b_forget, n_seq, s_new, npp):
    n_pages = page_table.shape[1]
    assert n_pages % npp == 0 and ATTN_HEADS * s_new == LANES
    n_steps = n_pages // npp
    n_pool = cache_k.shape[0]
    rows = ATTN_HEADS * s_new
    ck = jnp.transpose(cache_k, (0, 2, 3, 1))
    cv = jnp.transpose(cache_v, (0, 2, 3, 1))
    clf = jnp.transpose(cache_logf, (0, 2, 1))
    q = u_s[:, U_Q:U_Q + ATTN_DIM].reshape(n_seq, s_new, ATTN_KV_HEADS, ATTN_REP, HEAD_DIM)
    qbd = jnp.einsum("btgrd,gh->bgrthd", q, jnp.eye(ATTN_KV_HEADS, dtype=F32)).reshape(n_seq, rows, KV_DIM)
    et = np.zeros((rows, ATTN_HEADS), np.float32)
    et[np.arange(rows), np.arange(rows) // s_new] = 1.0
    et_b = jnp.broadcast_to(jnp.asarray(et), (n_seq, rows, ATTN_HEADS))
    qaug = jnp.concatenate([qbd * (ATTN_SCALE * LOG2E)] + [et_b] * N_SPLIT, axis=-1).astype(BF16)
    aug_w = KV_DIM + N_SPLIT * ATTN_HEADS
    suf = np.tile(np.triu(np.ones((PAGE_SIZE, PAGE_SIZE), np.float32), 1).T, (N_SPLIT, 1))
    bf = jnp.zeros((1, LANES), F32).at[0, SSD_HEADS:SSD_HEADS + ATTN_HEADS].set(b_forget)

    const2 = lambda b, s, pt: (0, 0)
    hbm = pl.BlockSpec(memory_space=pl.ANY)
    in_specs = [
        pl.BlockSpec((None, rows, aug_w), lambda b, s, pt: (b, 0, 0)),
        pl.BlockSpec((s_new, KV_DIM), lambda b, s, pt: (b, U_K // KV_DIM)),
        pl.BlockSpec((s_new, KV_DIM), lambda b, s, pt: (b, U_V // KV_DIM)),
        pl.BlockSpec((s_new, LANES), lambda b, s, pt: (b, U_DTF // LANES)),
        pl.BlockSpec((1, LANES), const2),
        pl.BlockSpec((rows, ATTN_HEADS), const2),
        pl.BlockSpec((N_SPLIT * PAGE_SIZE, PAGE_SIZE), const2),
        hbm, hbm, hbm,
    ]
    grid_spec = pltpu.PrefetchScalarGridSpec(
        num_scalar_prefetch=1,
        grid=(n_seq, n_steps),
        in_specs=in_specs,
        out_specs=(pl.BlockSpec((None, rows, KV_DIM), lambda b, s, pt: (b, 0, 0)),
                   pl.BlockSpec((s_new, LANES), lambda b, s, pt: (b, 0))),
        scratch_shapes=[pltpu.VMEM((SAMPLE_DMA_SLOTS, npp, ATTN_KV_HEADS, HEAD_DIM, PAGE_SIZE), F32),
                        pltpu.VMEM((SAMPLE_DMA_SLOTS, npp, ATTN_KV_HEADS, HEAD_DIM, PAGE_SIZE), F32),
                        pltpu.VMEM((SAMPLE_DMA_SLOTS, npp, ATTN_HEADS, PAGE_SIZE), F32),
                        pltpu.SemaphoreType.DMA((SAMPLE_DMA_SLOTS,)),
                        pltpu.VMEM((rows, LANES), F32), pltpu.VMEM((rows, LANES), F32),
                        pltpu.VMEM((rows, KV_DIM), F32), pltpu.VMEM((ATTN_HEADS, LANES), F32),
                        pltpu.VMEM((rows, LANES), F32)],
    )
    o_raw, lf_slab = pl.pallas_call(
        functools.partial(_fox_sample_body, npp=npp, n_steps=n_steps, n_pages=n_pages, n_seq=n_seq, s_new=s_new),
        out_shape=(jax.ShapeDtypeStruct((n_seq, rows, KV_DIM), F32),
                   jax.ShapeDtypeStruct((n_seq * s_new, LANES), F32)),
        grid_spec=grid_spec,
        compiler_params=_cparams(("arbitrary", "arbitrary"), 48),
        name="fox_sample",
    )(page_table.reshape(-1), qaug, u_s, u_s, u_s, bf, jnp.asarray(et), jnp.asarray(suf, BF16), ck, cv, clf)
    o = o_raw.reshape(n_seq, ATTN_KV_HEADS, ATTN_REP, s_new, ATTN_KV_HEADS, HEAD_DIM)
    o = jnp.einsum("bgrtgd->btgrd", o).reshape(n_seq * s_new, ATTN_DIM)
    return o, lf_slab


ROUTE_E0, ROUTE_E1, ROUTE_W0, ROUTE_W1 = 0, 1, 2, 3
ROUTER_EXPERT_LANE0 = N_EXPERT_GROUPS
_BIG_LANE = 4 * LANES


def _sigmoid(x):
    return 1.0 / (1.0 + jnp.exp(-x))


def _mix_body(y_ref, o_ref, gs_ref, ga_ref, h_ref, wssd_ref, wattn_ref, wout_ref, g2_ref, wr_ref, br_ref,
              h1_ref, hn_ref, route_ref):
    ys = jnp.dot(y_ref[...].astype(BF16), wssd_ref[...], preferred_element_type=F32)
    oa = jnp.dot(o_ref[...].astype(BF16), wattn_ref[...], preferred_element_type=F32)
    mix = _sigmoid(gs_ref[...]) * ys + _sigmoid(ga_ref[...]) * oa
    h1 = h_ref[...] + jnp.dot(mix.astype(BF16), wout_ref[...], preferred_element_type=F32)
    h1_ref[...] = h1
    ms = jnp.mean(h1 * h1, axis=-1, keepdims=True)
    hn = (h1 * lax.rsqrt(ms + RMS_EPS)) * g2_ref[...]
    hn_ref[...] = hn
    hn_hi = hn.astype(BF16)
    hn_lo = (hn - hn_hi.astype(F32)).astype(BF16)
    logits = jnp.dot(jnp.concatenate([hn_hi, hn_hi, hn_lo], axis=1), wr_ref[...], preferred_element_type=F32)
    route_ref[...] = _route(logits + br_ref[...])


def _route(logits):
    lane = lax.broadcasted_iota(jnp.int32, logits.shape, 1)
    gl = jnp.where(lane < N_EXPERT_GROUPS, logits, -jnp.inf)
    gmax = jnp.max(gl, axis=-1, keepdims=True)
    gsel = jnp.min(jnp.where(gl == gmax, lane, _BIG_LANE), axis=-1, keepdims=True)
    wgrp = 1.0 / jnp.sum(jnp.exp(gl - gmax), axis=-1, keepdims=True)
    elane = lane - ROUTER_EXPERT_LANE0
    in_group = (elane >= gsel * EXPERTS_PER_GROUP) & (elane < (gsel + 1) * EXPERTS_PER_GROUP)
    el = jnp.where(in_group, logits, -jnp.inf)
    t1 = jnp.max(el, axis=-1, keepdims=True)
    i1 = jnp.min(jnp.where(el == t1, lane, _BIG_LANE), axis=-1, keepdims=True)
    el2 = jnp.where(lane == i1, -jnp.inf, el)
    t2 = jnp.max(el2, axis=-1, keepdims=True)
    i2 = jnp.min(jnp.where(el2 == t2, lane, _BIG_LANE), axis=-1, keepdims=True)
    e21 = jnp.exp(t2 - t1)
    w1 = wgrp / (1.0 + e21)
    w2 = w1 * e21
    return jnp.where(lane == ROUTE_E0, (i1 - ROUTER_EXPERT_LANE0).astype(F32),
                     jnp.where(lane == ROUTE_E1, (i2 - ROUTER_EXPERT_LANE0).astype(F32),
                               jnp.where(lane == ROUTE_W0, w1, jnp.where(lane == ROUTE_W1, w2, 0.0))))


def _mix(y, o, u, h, p, tm):
    t = h.shape[0]
    assert t % tm == 0
    row = lambda i: (i, 0)
    const = lambda i: (0, 0)
    return pl.pallas_call(
        _mix_body,
        out_shape=(jax.ShapeDtypeStruct((t, D_MODEL), F32), jax.ShapeDtypeStruct((t, D_MODEL), F32),
                   jax.ShapeDtypeStruct((t, LANES), F32)),
        grid=(t // tm,),
        in_specs=[
            pl.BlockSpec((tm, D_INNER), row),
            pl.BlockSpec((tm, ATTN_DIM), row),
            pl.BlockSpec((tm, D_MODEL), lambda i: (i, U_GS // D_MODEL)),
            pl.BlockSpec((tm, D_MODEL), lambda i: (i, U_GA // D_MODEL)),
            pl.BlockSpec((tm, D_MODEL), row),
            pl.BlockSpec((D_INNER, D_MODEL), const),
            pl.BlockSpec((ATTN_DIM, D_MODEL), const),
            pl.BlockSpec((D_MODEL, D_MODEL), const),
            pl.BlockSpec((1, D_MODEL), const),
            pl.BlockSpec((3 * D_MODEL, LANES), const),
            pl.BlockSpec((1, LANES), const),
        ],
        out_specs=(pl.BlockSpec((tm, D_MODEL), row), pl.BlockSpec((tm, D_MODEL), row),
                   pl.BlockSpec((tm, LANES), row)),
        compiler_params=_cparams(("parallel",), 56),
        name="mix_route",
    )(y, o, u, u, h, p["w_ssd_br"], p["w_attn_br"], p["w_out"], p["norm2_g"], p["w_router"], p["b_router"])


def _mix_params(w_ssd_br, w_attn_br, w_out, norm2_g, w_rg, b_rg, w_re, b_re):
    n_r = N_EXPERT_GROUPS + N_EXPERTS
    w_router = jnp.concatenate([w_rg, w_re, jnp.zeros((D_MODEL, LANES - n_r), F32)], axis=1)
    w_hi = w_router.astype(BF16)
    w_lo = (w_router - w_hi.astype(F32)).astype(BF16)
    w_router = jnp.concatenate([w_hi, w_lo, w_hi], axis=0)
    b_router = jnp.concatenate([b_rg, b_re, jnp.zeros((LANES - n_r,), F32)]).reshape(1, LANES)
    return dict(w_ssd_br=w_ssd_br.astype(BF16), w_attn_br=w_attn_br.astype(BF16), w_out=w_out.astype(BF16),
                norm2_g=norm2_g.reshape(1, D_MODEL), w_router=w_router, b_router=b_router)


def _moe_rank_body(route_ref, pos_ref, cnt_ref, carry_ref, *, tm):
    @pl.when(pl.program_id(0) == 0)
    def _():
        carry_ref[...] = jnp.zeros(carry_ref.shape, F32)

    route = route_ref[...]
    lane = lax.broadcasted_iota(jnp.int32, (tm, LANES), 1).astype(F32)
    hit0 = lane == route[:, ROUTE_E0:ROUTE_E0 + 1]
    hit1 = lane == route[:, ROUTE_E1:ROUTE_E1 + 1]
    onehot = hit0.astype(F32) + hit1.astype(F32)
    before = (lax.broadcasted_iota(jnp.int32, (tm, tm), 1) < lax.broadcasted_iota(jnp.int32, (tm, tm), 0))
    c = jnp.dot(before.astype(BF16), onehot.astype(BF16), preferred_element_type=F32) + carry_ref[...]
    pos0 = jnp.sum(jnp.where(hit0, c, 0.0), axis=-1, keepdims=True)
    pos1 = jnp.sum(jnp.where(hit1, c, 0.0), axis=-1, keepdims=True)
    pos_ref[...] = jnp.where(lane == 0.0, pos0, jnp.where(lane == 1.0, pos1, 0.0))
    total = carry_ref[...] + jnp.sum(onehot, axis=0, keepdims=True)
    carry_ref[...] = total
    cnt_ref[...] = total


def _moe_rank(route, tm):
    t = route.shape[0]
    assert t % tm == 0
    return pl.pallas_call(
        functools.partial(_moe_rank_body, tm=tm),
        out_shape=(jax.ShapeDtypeStruct((t, LANES), F32), jax.ShapeDtypeStruct((1, LANES), F32)),
        grid=(t // tm,),
        in_specs=[pl.BlockSpec((tm, LANES), lambda i: (i, 0))],
        out_specs=(pl.BlockSpec((tm, LANES), lambda i: (i, 0)), pl.BlockSpec((1, LANES), lambda i: (0, 0))),
        scratch_shapes=[pltpu.VMEM((1, LANES), F32)],
        compiler_params=_cparams(("arbitrary",)),
        name="moe_rank",
    )(route)


def _experts_body(be_ref, nu_ref, x_ref, wg_ref, wu_ref, wd_ref, y_ref, wg16_ref, wu16_ref, wd16_ref):
    i = pl.program_id(0)
    prev = be_ref[jnp.maximum(i - 1, 0)]

    @pl.when((i == 0) | (be_ref[i] != prev))
    def _():
        wg16_ref[...] = wg_ref[...].astype(BF16)
        wu16_ref[...] = wu_ref[...].astype(BF16)
        wd16_ref[...] = wd_ref[...].astype(BF16)

    @pl.when(i < nu_ref[0])
    def _():
        sub = x_ref.shape[0] // 2
        rows = [slice(0, sub), slice(sub, 2 * sub)]
        gate_up = []
        for r in rows:
            x16 = x_ref[r, :].astype(BF16)
            gate_up.append((jnp.dot(x16, wg16_ref[...], preferred_element_type=F32),
                            jnp.dot(x16, wu16_ref[...], preferred_element_type=F32)))
        for r, (gt, up) in zip(rows, gate_up):
            y_ref[r, :] = jnp.dot((_silu(gt) * up).astype(BF16), wd16_ref[...], preferred_element_type=F32)

    @pl.when(i >= nu_ref[0])
    def _():
        y_ref[...] = jnp.zeros(y_ref.shape, F32)


def _experts(xs, block_e, n_used, w_gate, w_up, w_down, blk):
    rows = xs.shape[0]
    n_blocks = rows // blk
    grid_spec = pltpu.PrefetchScalarGridSpec(
        num_scalar_prefetch=2,
        grid=(n_blocks,),
        in_specs=[
            pl.BlockSpec((blk, D_MODEL), lambda i, be, nu: (jnp.minimum(i, nu[0] - 1), 0)),
            pl.BlockSpec((None, D_MODEL, D_EXPERT), lambda i, be, nu: (be[i], 0, 0)),
            pl.BlockSpec((None, D_MODEL, D_EXPERT), lambda i, be, nu: (be[i], 0, 0)),
            pl.BlockSpec((None, D_EXPERT, D_MODEL), lambda i, be, nu: (be[i], 0, 0)),
        ],
        out_specs=pl.BlockSpec((blk, D_MODEL), lambda i, be, nu: (i, 0)),
        scratch_shapes=[pltpu.VMEM((D_MODEL, D_EXPERT), BF16), pltpu.VMEM((D_MODEL, D_EXPERT), BF16),
                        pltpu.VMEM((D_EXPERT, D_MODEL), BF16)],
    )
    return pl.pallas_call(
        _experts_body,
        out_shape=jax.ShapeDtypeStruct((rows, D_MODEL), F32),
        grid_spec=grid_spec,
        compiler_params=_cparams(("arbitrary",), 48),
        name="moe_experts",
    )(block_e, n_used, xs, w_gate, w_up, w_down)


MOE_ROWS_PER_BLOCK = 256
MOE_ROWS_PER_BLOCK_FEW_TOKENS = 128
MOE_TOKEN_TILE = 256
DMA_ISSUE_UNROLL = 8


def _row_copy(src, src_row, dst, dst_row, sem):
    return pltpu.make_async_copy(src.at[pl.ds(src_row, 1)], dst.at[pl.ds(dst_row, 1)], sem)


def _dispatch_body(zb_ref, nu_ref, d0_ref, d1_ref, hn_ref, xs_ref, zero_ref, sem, zsem, *, tm, blk, n_blocks):
    @pl.when(pl.program_id(0) == 0)
    def _():
        zero_ref[...] = jnp.zeros(zero_ref.shape, F32)

        def zero_block(b):
            return pltpu.make_async_copy(zero_ref, xs_ref.at[pl.ds(b * blk, blk)], zsem)

        for e in range(N_EXPERTS):
            zero_block(zb_ref[e]).start()

        def start_tail(b, c):
            zero_block(b).start()
            return c

        def wait_tail(b, c):
            zero_block(b).wait()
            return c

        lax.fori_loop(nu_ref[0], n_blocks, start_tail, 0)
        for e in range(N_EXPERTS):
            zero_block(0).wait()
        lax.fori_loop(nu_ref[0], n_blocks, wait_tail, 0)

    for r in range(tm):
        _row_copy(hn_ref, r, xs_ref, d0_ref[r], sem).start(priority=0)
        _row_copy(hn_ref, r, xs_ref, d1_ref[r], sem).start(priority=1)

    def drain(r, c):
        _row_copy(hn_ref, 0, xs_ref, 0, sem).wait()
        _row_copy(hn_ref, 0, xs_ref, 0, sem).wait()
        return c

    lax.fori_loop(0, tm, drain, 0, unroll=DMA_ISSUE_UNROLL)


def _dispatch(hn, d0, d1, zero_blocks, n_used, n_blocks, blk, tm):
    t = hn.shape[0]
    assert t % tm == 0
    smem_tile = lambda: pl.BlockSpec((tm,), lambda i, zb, nu: (i,), memory_space=pltpu.SMEM)
    grid_spec = pltpu.PrefetchScalarGridSpec(
        num_scalar_prefetch=2,
        grid=(t // tm,),
        in_specs=[smem_tile(), smem_tile(), pl.BlockSpec((tm, D_MODEL), lambda i, zb, nu: (i, 0))],
        out_specs=pl.BlockSpec(memory_space=pl.ANY),
        scratch_shapes=[pltpu.VMEM((blk, D_MODEL), F32), pltpu.SemaphoreType.DMA, pltpu.SemaphoreType.DMA],
    )
    return pl.pallas_call(
        functools.partial(_dispatch_body, tm=tm, blk=blk, n_blocks=n_blocks),
        out_shape=jax.ShapeDtypeStruct((n_blocks * blk, D_MODEL), F32),
        grid_spec=grid_spec,
        compiler_params=_cparams(("arbitrary",)),
        name="moe_dispatch",
    )(zero_blocks, n_used, d0, d1, hn)


def _combine_body(d0_ref, d1_ref, d0n_ref, d1n_ref, h1_ref, route_ref, g_ref, ys_ref, o_ref, buf, sem, *, tm, n_tiles):
    i = pl.program_id(0)
    slot = i % 2

    def gather(da_ref, db_ref, s):
        for r in range(tm):
            _row_copy(ys_ref, da_ref[r], buf.at[s, 0], r, sem.at[s]).start()
            _row_copy(ys_ref, db_ref[r], buf.at[s, 1], r, sem.at[s]).start()

    @pl.when(i == 0)
    def _():
        gather(d0_ref, d1_ref, 0)

    @pl.when(i + 1 < n_tiles)
    def _():
        gather(d0n_ref, d1n_ref, 1 - slot)

    def drain(r, c):
        _row_copy(ys_ref, 0, buf.at[slot, 0], 0, sem.at[slot]).wait()
        _row_copy(ys_ref, 0, buf.at[slot, 1], 0, sem.at[slot]).wait()
        return c

    lax.fori_loop(0, tm, drain, 0, unroll=DMA_ISSUE_UNROLL)
    route = route_ref[...]
    h = (h1_ref[...] + route[:, ROUTE_W0:ROUTE_W0 + 1] * buf[slot, 0]
         + route[:, ROUTE_W1:ROUTE_W1 + 1] * buf[slot, 1])
    ms = jnp.mean(h * h, axis=-1, keepdims=True)
    o_ref[...] = (h * lax.rsqrt(ms + RMS_EPS)) * g_ref[...]


def _combine(h1, ys, d0, d1, route, g, tm):
    t = h1.shape[0]
    n_tiles = t // tm
    row = lambda i: (i, 0)
    cur = lambda: pl.BlockSpec((tm,), lambda i: (i,), memory_space=pltpu.SMEM)
    nxt = lambda: pl.BlockSpec((tm,), lambda i: (jnp.minimum(i + 1, n_tiles - 1),), memory_space=pltpu.SMEM)
    return pl.pallas_call(
        functools.partial(_combine_body, tm=tm, n_tiles=n_tiles),
        out_shape=jax.ShapeDtypeStruct((t, D_MODEL), F32),
        grid=(n_tiles,),
        in_specs=[cur(), cur(), nxt(), nxt(), pl.BlockSpec((tm, D_MODEL), row), pl.BlockSpec((tm, LANES), row),
                  pl.BlockSpec((1, D_MODEL), lambda i: (0, 0)), pl.BlockSpec(memory_space=pl.ANY)],
        out_specs=pl.BlockSpec((tm, D_MODEL), row),
        scratch_shapes=[pltpu.VMEM((2, 2, tm, D_MODEL), F32), pltpu.SemaphoreType.DMA((2,))],
        compiler_params=_cparams(("arbitrary",)),
        name="moe_combine_norm",
    )(d0, d1, d0, d1, h1, route, g.reshape(1, D_MODEL), ys)


def _moe(hn, h1, route, w_gate, w_up, w_down, final_g, blk):
    t = hn.shape[0]
    pos, cnt = _moe_rank(route, MOE_TOKEN_TILE)
    counts = cnt[0, :N_EXPERTS].astype(jnp.int32)
    padded = (counts + blk - 1) // blk * blk
    ends = jnp.cumsum(padded)
    starts = ends - padded
    expert_ids = jnp.arange(N_EXPERTS, dtype=jnp.int32)

    def dest(e_lane, p_lane):
        e = route[:, e_lane].astype(jnp.int32)
        start = jnp.sum(jnp.where(e[:, None] == expert_ids[None, :], starts[None, :], 0), axis=1)
        return start + pos[:, p_lane].astype(jnp.int32)

    d0 = dest(ROUTE_E0, 0)
    d1 = dest(ROUTE_E1, 1)
    n_blocks = (2 * t + N_EXPERTS * (blk - 1) + blk - 1) // blk
    first_row = jnp.arange(n_blocks, dtype=jnp.int32) * blk
    block_e = jnp.minimum(jnp.sum((ends[None, :] <= first_row[:, None]).astype(jnp.int32), axis=1), N_EXPERTS - 1)
    n_used = (ends[-1] // blk).astype(jnp.int32).reshape(1)
    zero_blocks = jnp.clip((ends - 1) // blk, 0, n_blocks - 1).astype(jnp.int32)
    xs = _dispatch(hn, d0, d1, zero_blocks, n_used, n_blocks, blk, MOE_TOKEN_TILE)
    ys = _experts(xs, block_e, n_used, w_gate, w_up, w_down, blk)
    return _combine(h1, ys, d0, d1, route, final_g, MOE_TOKEN_TILE)


def _conv_history(rows):
    n = rows.shape[0]
    return jnp.concatenate([jnp.zeros((n, CONV_PAD - (CONV_W - 1), CONV_DIM), F32), rows], axis=1)


def _ucols(u, start, width):
    return u[:, start:start + width]


def kernel(x_prompt, x_sample, cache_k, cache_v, cache_logf, state_ssm, state_conv, page_table, meta_tokens, norm1_g, w_in, conv_w, conv_b, dt_bias, a_log, d_skip, ssd_norm_g, b_forget, w_ssd_br, w_attn_br, w_out, norm2_g, w_router_group, b_router_group, w_router_expert, b_router_expert, w_exp_gate, w_exp_up, w_exp_down, final_norm_g):
    nb, sl, _ = x_prompt.shape
    sb, ss, _ = x_sample.shape
    assert w_in.shape[0] == 1, "single-layer step"
    ly = 0
    xp = x_prompt.reshape(nb * sl, D_MODEL)
    xs = x_sample.reshape(sb * ss, D_MODEL)

    wp = _pack_w_in(w_in[ly])
    u_p = _inproj(xp, norm1_g[ly], wp)
    u_s = _inproj(xs, norm1_g[ly], wp)
    u_m = _inproj(meta_tokens, norm1_g[ly], wp)

    sp = _ssd_params(conv_w[ly], conv_b[ly], dt_bias[ly], a_log[ly], d_skip[ly], ssd_norm_g[ly])
    zero_hist = jnp.zeros((1, CONV_PAD, CONV_DIM), F32)
    zero_state = jnp.zeros((1, D_INNER, D_STATE), F32)
    _, h_meta = _ssd(u_m, 1, N_META, N_META, zero_hist, zero_state, sp, BF16)
    hist_meta = _conv_history(_ucols(u_m, U_XBC, CONV_DIM)[None, N_META - (CONV_W - 1):])
    y_p, ssm_p = _ssd(u_p, nb, sl, SSD_CHUNK, hist_meta, h_meta, sp, BF16)
    y_s, ssm_s = _ssd(u_s, sb, ss, ss, _conv_history(state_conv[ly]),
                      state_ssm[ly].reshape(sb, D_INNER, D_STATE), sp, F32)

    slab_m, _, kp_m, vpt_m, kt_m, vt_m = _attn_prep(u_m, b_forget[ly], N_META, N_META, rel_to_last=True)
    slab_p, qp_p, kp_p, vpt_p, kt_p, vt_p = _attn_prep(u_p, b_forget[ly], nb * sl, sl)
    o_p = _fox_prompt(qp_p, kp_p, vpt_p, kp_m, vpt_m, nb, sl, ATTN_TQ)
    o_s, slab_s = _fox_sample(u_s, cache_k[ly], cache_v[ly], cache_logf[ly], page_table, b_forget[ly], sb, ss,
                              SAMPLE_PAGES_PER_STEP)

    mp = _mix_params(w_ssd_br[ly], w_attn_br[ly], w_out[ly], norm2_g[ly], w_router_group[ly], b_router_group[ly],
                     w_router_expert[ly], b_router_expert[ly])
    h1_p, hn_p, route_p = _mix(y_p, o_p, u_p, xp, mp, 256)
    h1_s, hn_s, route_s = _mix(y_s, o_s, u_s, xs, mp, 256)
    out_p = _moe(hn_p, h1_p, route_p, w_exp_gate[ly], w_exp_up[ly], w_exp_down[ly], final_norm_g,
                 MOE_ROWS_PER_BLOCK)
    out_s = _moe(hn_s, h1_s, route_s, w_exp_gate[ly], w_exp_up[ly], w_exp_down[ly], final_norm_g,
                 MOE_ROWS_PER_BLOCK_FEW_TOKENS)

    def with_meta(meta_rows, rows, width):
        m = jnp.broadcast_to(meta_rows[None], (nb, N_META, width))
        return jnp.concatenate([m, rows.reshape(nb, sl, width)], axis=1)[None]

    def kv_with_meta(t_meta, t_rows):
        m = jnp.broadcast_to(t_meta.reshape(ATTN_KV_HEADS, HEAD_DIM, 1, N_META), (ATTN_KV_HEADS, HEAD_DIM, nb, N_META))
        full = jnp.concatenate([m, t_rows.reshape(ATTN_KV_HEADS, HEAD_DIM, nb, sl)], axis=3)
        return jnp.transpose(full, (2, 3, 0, 1))[None]

    lf_m = slab_m[:, SSD_HEADS:SSD_HEADS + ATTN_HEADS]
    lf_p = slab_p[:, SSD_HEADS:SSD_HEADS + ATTN_HEADS]
    lf_s = slab_s[:, SSD_HEADS:SSD_HEADS + ATTN_HEADS]
    kv_shape_s = (1, sb, ss, ATTN_KV_HEADS, HEAD_DIM)
    state_shape = (SSD_HEADS, SSD_HEADDIM, D_STATE)
    tail = CONV_W - 1
    return (
        out_p.reshape(nb, sl, D_MODEL),
        out_s.reshape(sb, ss, D_MODEL),
        kv_with_meta(kt_m, kt_p),
        kv_with_meta(vt_m, vt_p),
        with_meta(lf_m, lf_p, ATTN_HEADS),
        ssm_p.reshape((1, nb) + state_shape),
        u_p.reshape(nb, sl, U_COLS)[None, :, sl - tail:, U_XBC:U_XBC + CONV_DIM],
        _ucols(u_s, U_K, KV_DIM).reshape(kv_shape_s),
        _ucols(u_s, U_V, KV_DIM).reshape(kv_shape_s),
        lf_s.reshape(1, sb, ss, ATTN_HEADS),
        ssm_s.reshape((1, sb) + state_shape),
        u_s.reshape(sb, ss, U_COLS)[None, :, ss - tail:, U_XBC:U_XBC + CONV_DIM],
    )
```

```python
import functools

import numpy as np
import jax
import jax.numpy as jnp
from jax import lax
from jax.experimental import pallas as pl
from jax.experimental.pallas import tpu as pltpu

F32 = jnp.float32
BF16 = jnp.bfloat16
HI = lax.Precision.HIGHEST

D_MODEL = 1024
N_META = 16
RMS_EPS = 1e-6
D_INNER = 2048
SSD_HEADDIM = 64
SSD_HEADS = 32
SSD_GROUPS = 8
HEADS_PER_GROUP = SSD_HEADS // SSD_GROUPS
GROUP_W = HEADS_PER_GROUP * SSD_HEADDIM
D_STATE = 128
CONV_W = 4
CONV_DIM = 4096
SSD_CHUNK = 128
ATTN_HEADS = 16
ATTN_KV_HEADS = 4
HEAD_DIM = 64
ATTN_REP = 4
ATTN_DIM = 1024
KV_DIM = 256
ATTN_SCALE = HEAD_DIM ** -0.5
PAGE_SIZE = 128
N_EXPERT_GROUPS = 4
EXPERTS_PER_GROUP = 8
N_EXPERTS = 32
D_EXPERT = 512
LANES = 128

U_XBC = 0
U_Z = 4096
U_Q = 6144
U_GS = 7168
U_GA = 8192
U_K = 9216
U_V = 9472
U_DTF = 9728
U_COLS = 9856
U_COLS_PADDED = 10240
INPROJ_TN = 1024
INPROJ_TM = 2048


def _cparams(sem, vmem_mb=None):
    kw = dict(dimension_semantics=sem)
    if vmem_mb is not None:
        kw["vmem_limit_bytes"] = vmem_mb * 1024 * 1024
    return pltpu.CompilerParams(**kw)


def _inproj_body(x_ref, g_ref, w_ref, o_ref, xn_ref):
    @pl.when(pl.program_id(1) == 0)
    def _():
        x = x_ref[...]
        ms = jnp.mean(x * x, axis=-1, keepdims=True)
        xn_ref[...] = ((x * lax.rsqrt(ms + RMS_EPS)) * g_ref[...]).astype(BF16)

    o_ref[...] = jnp.dot(xn_ref[...], w_ref[...], preferred_element_type=F32)


def _inproj(x, g, wp):
    t = x.shape[0]
    tm = min(t, INPROJ_TM)
    assert t % tm == 0
    return pl.pallas_call(
        _inproj_body,
        out_shape=jax.ShapeDtypeStruct((t, U_COLS), F32),
        grid=(t // tm, U_COLS_PADDED // INPROJ_TN),
        in_specs=[
            pl.BlockSpec((tm, D_MODEL), lambda i, j: (i, 0)),
            pl.BlockSpec((1, D_MODEL), lambda i, j: (0, 0)),
            pl.BlockSpec((D_MODEL, INPROJ_TN), lambda i, j: (0, j)),
        ],
        out_specs=pl.BlockSpec((tm, INPROJ_TN), lambda i, j: (i, j)),
        scratch_shapes=[pltpu.VMEM((tm, D_MODEL), BF16)],
        compiler_params=_cparams(("parallel", "arbitrary"), 56),
        name="inproj",
    )(x, g.reshape(1, D_MODEL), wp)


CONV_PAD = 8


def _conv_silu_rows(staged, rows, w_ref, b_ref):
    acc = b_ref[...] + staged[CONV_PAD:CONV_PAD + rows, :] * w_ref[CONV_W - 1:CONV_W, :]
    for k in range(CONV_W - 1):
        off = CONV_PAD - (CONV_W - 1) + k
        acc = acc + pltpu.roll(staged, CONV_PAD + rows - off, 0)[0:rows, :] * w_ref[k:k + 1, :]
    return _silu(acc)


def _pack_w_in(w_in):
    o = np.cumsum([0, D_INNER, CONV_DIM, SSD_HEADS, ATTN_DIM, KV_DIM, KV_DIM, ATTN_HEADS, D_MODEL, D_MODEL])
    z, xbc, dt, q, k, v, f, gs, ga = [w_in[:, o[i]:o[i + 1]] for i in range(9)]
    pad_dtf = jnp.zeros((D_MODEL, LANES - SSD_HEADS - ATTN_HEADS), w_in.dtype)
    pad = jnp.zeros((D_MODEL, U_COLS_PADDED - U_COLS), w_in.dtype)
    return jnp.concatenate([xbc, z, q, gs, ga, k, v, dt, f, pad_dtf, pad], axis=1).astype(BF16)


def _log_sigmoid(x):
    return jnp.minimum(x, 0.0) - jnp.log1p(jnp.exp(-jnp.abs(x)))


LOG2E = 1.4426950408889634
QK_ONES_LANE = HEAD_DIM
QK_CK_LANE = HEAD_DIM + 3
V_ONES_ROW = HEAD_DIM
N_SPLIT = 3
ATTN_TQ = 256
QK_AHEAD = 4


def _placement_matrices():
    eq = np.zeros((LANES, ATTN_HEADS * LANES), np.float32)
    ek = np.zeros((LANES, ATTN_KV_HEADS * LANES), np.float32)
    for h in range(ATTN_HEADS):
        g, r = divmod(h, ATTN_REP)
        for s in range(N_SPLIT):
            eq[s * ATTN_HEADS + h, h * LANES + QK_ONES_LANE + s] = 1.0
            ek[s * ATTN_HEADS + h, g * LANES + QK_CK_LANE + s * ATTN_REP + r] = 1.0
    cq = np.zeros((1, ATTN_HEADS * LANES), np.float32)
    for h in range(ATTN_HEADS):
        r = h % ATTN_REP
        for s in range(N_SPLIT):
            cq[0, h * LANES + QK_CK_LANE + s * ATTN_REP + r] = -1.0
    return jnp.asarray(eq, BF16), jnp.asarray(ek, BF16), jnp.asarray(cq)


def _transpose_rows(x, q):
    if q < LANES:
        x = jnp.concatenate([x, jnp.zeros((LANES - q, x.shape[1]), x.dtype)], axis=0)
    return x.T[:, :q]


def _split3(x):
    hi = x.astype(BF16)
    r1 = x - hi.astype(F32)
    mid = r1.astype(BF16)
    lo = (r1 - mid.astype(F32)).astype(BF16)
    return hi, mid, lo


def _split3_stacked(x, axis):
    return jnp.concatenate([t.astype(F32) for t in _split3(x)], axis=axis).astype(BF16)


def _attn_prep_body(q_ref, k_ref, v_ref, dtf_ref, bf_ref, eq_ref, ek_ref, cq_ref, slab_ref, qp_ref, kp_ref, vpt_ref,
                    kt_ref, vt_ref, carry_ref, *, tiles_per_seq, tl, rel_to_last):
    i = pl.program_id(0)

    @pl.when(i % tiles_per_seq == 0)
    def _():
        carry_ref[...] = jnp.zeros_like(carry_ref)

    lf = _log_sigmoid(dtf_ref[...] + bf_ref[...])
    row = lax.broadcasted_iota(jnp.int32, (tl, tl), 0)
    col = lax.broadcasted_iota(jnp.int32, (tl, tl), 1)
    c = jnp.dot((col <= row).astype(F32), lf, precision=HI, preferred_element_type=F32) + carry_ref[...]
    carry_ref[...] = c[tl - 1:tl, :]
    lane = lax.broadcasted_iota(jnp.int32, (tl, LANES), 1)
    c16 = jnp.where(lane < ATTN_HEADS, pltpu.roll(c, LANES - SSD_HEADS, 1), 0.0)
    slab_ref[...] = jnp.where((lane >= SSD_HEADS) & (lane < SSD_HEADS + ATTN_HEADS), lf, c16)
    if rel_to_last:
        c16 = c16 - c16[tl - 1:tl, :]
    hi, mid, lo = _split3(c16 * LOG2E)
    x = jnp.where(lane < ATTN_HEADS, hi.astype(F32),
                  jnp.where(lane < 2 * ATTN_HEADS, pltpu.roll(mid.astype(F32), ATTN_HEADS, 1),
                            pltpu.roll(lo.astype(F32), 2 * ATTN_HEADS, 1))).astype(BF16)
    q_extra = jnp.dot(x, eq_ref[...], preferred_element_type=F32) + cq_ref[...]
    k_extra = jnp.dot(x, ek_ref[...], preferred_element_type=F32)
    low = lane < HEAD_DIM
    for h in range(ATTN_HEADS):
        qx = q_ref[:, (h // 2) * LANES:(h // 2 + 1) * LANES]
        if h % 2:
            qx = pltpu.roll(qx, HEAD_DIM, 1)
        tile = jnp.where(low, qx * (ATTN_SCALE * LOG2E), q_extra[:, h * LANES:(h + 1) * LANES])
        qp_ref[:, h * LANES:(h + 1) * LANES] = tile.astype(BF16)
    ones_k = ((lane >= QK_ONES_LANE) & (lane < QK_CK_LANE)).astype(F32)
    ones_v = (lane == V_ONES_ROW).astype(F32)
    for g in range(ATTN_KV_HEADS):
        kx = k_ref[:, (g // 2) * LANES:(g // 2 + 1) * LANES]
        vx = v_ref[:, (g // 2) * LANES:(g // 2 + 1) * LANES]
        if g % 2:
            kx = pltpu.roll(kx, HEAD_DIM, 1)
            vx = pltpu.roll(vx, HEAD_DIM, 1)
        kp_ref[:, g * LANES:(g + 1) * LANES] = jnp.where(low, kx, k_extra[:, g * LANES:(g + 1) * LANES] + ones_k
                                                         ).astype(BF16)
        vpt_ref[g * LANES:(g + 1) * LANES, :] = _transpose_rows(jnp.where(low, vx, ones_v), tl).astype(BF16)
    kt_ref[...] = _transpose_rows(k_ref[...], tl)
    vt_ref[...] = _transpose_rows(v_ref[...], tl)


def _attn_prep(u, b_forget, n_tokens, seq_len, rel_to_last=False):
    tl = min(n_tokens, 256)
    assert n_tokens % tl == 0 and seq_len % tl == 0
    bf = jnp.zeros((1, LANES), F32).at[0, SSD_HEADS:SSD_HEADS + ATTN_HEADS].set(b_forget)
    eq, ek, cq = _placement_matrices()
    const = lambda i: (0, 0)
    return pl.pallas_call(
        functools.partial(_attn_prep_body, tiles_per_seq=seq_len // tl, tl=tl, rel_to_last=rel_to_last),
        out_shape=(jax.ShapeDtypeStruct((n_tokens, LANES), F32),
                   jax.ShapeDtypeStruct((n_tokens, ATTN_HEADS * LANES), BF16),
                   jax.ShapeDtypeStruct((n_tokens, ATTN_KV_HEADS * LANES), BF16),
                   jax.ShapeDtypeStruct((ATTN_KV_HEADS * LANES, n_tokens), BF16),
                   jax.ShapeDtypeStruct((KV_DIM, n_tokens), F32),
                   jax.ShapeDtypeStruct((KV_DIM, n_tokens), F32)),
        grid=(n_tokens // tl,),
        in_specs=[
            pl.BlockSpec((tl, ATTN_DIM), lambda i: (i, U_Q // ATTN_DIM)),
            pl.BlockSpec((tl, KV_DIM), lambda i: (i, U_K // KV_DIM)),
            pl.BlockSpec((tl, KV_DIM), lambda i: (i, U_V // KV_DIM)),
            pl.BlockSpec((tl, LANES), lambda i: (i, U_DTF // LANES)),
            pl.BlockSpec((1, LANES), const),
            pl.BlockSpec((LANES, ATTN_HEADS * LANES), const),
            pl.BlockSpec((LANES, ATTN_KV_HEADS * LANES), const),
            pl.BlockSpec((1, ATTN_HEADS * LANES), const),
        ],
        out_specs=(pl.BlockSpec((tl, LANES), lambda i: (i, 0)),
                   pl.BlockSpec((tl, ATTN_HEADS * LANES), lambda i: (i, 0)),
                   pl.BlockSpec((tl, ATTN_KV_HEADS * LANES), lambda i: (i, 0)),
                   pl.BlockSpec((ATTN_KV_HEADS * LANES, tl), lambda i: (0, i)),
                   pl.BlockSpec((KV_DIM, tl), lambda i: (0, i)),
                   pl.BlockSpec((KV_DIM, tl), lambda i: (0, i))),
        scratch_shapes=[pltpu.VMEM((1, LANES), F32)],
        compiler_params=_cparams(("arbitrary",), 40),
        name="attn_prep",
    )(u, u, u, u, bf, eq, ek, cq)


def _silu(x):
    return x * (1.0 / (1.0 + jnp.exp(-x)))


def _softplus(x):
    return jnp.maximum(x, 0.0) + jnp.log1p(jnp.exp(-jnp.abs(x)))


def _ssd_body(xbc_ref, z_ref, dtf_ref, conv0_ref, h0_ref, cw_ref, cb_ref, dtb_ref, alog_ref, dskip_ref,
              gn_ref, e_ref, y_ref, hout_ref, xconv_ref, ht_ref, *, q, n_chunks):
    c = pl.program_id(1)

    @pl.when(c == 0)
    def _():
        xconv_ref[0:CONV_PAD, :] = conv0_ref[0]
        for g in range(SSD_GROUPS):
            ht_ref[g] = h0_ref[0, g * GROUP_W:(g + 1) * GROUP_W, :].T

    xconv_ref[CONV_PAD:CONV_PAD + q, :] = xbc_ref[...]
    xc = _conv_silu_rows(xconv_ref[...], q, cw_ref, cb_ref)
    xconv_ref[CONV_PAD - (CONV_W - 1):CONV_PAD, :] = xconv_ref[CONV_PAD + q - (CONV_W - 1):CONV_PAD + q, :]

    dt = _softplus(dtf_ref[...] + dtb_ref[...])
    a = -jnp.exp(alog_ref[...])
    row = lax.broadcasted_iota(jnp.int32, (q, q), 0)
    col = lax.broadcasted_iota(jnp.int32, (q, q), 1)
    causal = col <= row
    tri = causal.astype(BF16)
    acum = jnp.dot(jnp.concatenate([tri] * N_SPLIT, axis=1), _split3_stacked(dt * a, 0),
                   preferred_element_type=F32)
    acum_t = _transpose_rows(acum, q)
    a_last = acum[q - 1:q, :]
    fac = jnp.concatenate([jnp.exp(acum), jnp.exp(a_last - acum) * dt, dt], axis=0)
    fac = jnp.dot(_split3_stacked(fac, 1), e_ref[...], preferred_element_type=F32)
    ea_full, wst_full, dt_full = fac[0:q], fac[q:2 * q], fac[2 * q:3 * q]

    def group_bc(g):
        bg = xc[:, D_INNER + g * D_STATE:D_INNER + (g + 1) * D_STATE]
        cg = xc[:, D_INNER + SSD_GROUPS * D_STATE + g * D_STATE:D_INNER + SSD_GROUPS * D_STATE + (g + 1) * D_STATE]
        cg16 = cg.astype(BF16)
        cbm = lax.dot_general(cg16, bg.astype(BF16), (((1,), (1,)), ((), ())), preferred_element_type=F32)
        return bg, cg16, cbm

    nxt = group_bc(0)
    for g in range(SSD_GROUPS):
        gs = slice(g * GROUP_W, (g + 1) * GROUP_W)
        xg = xc[:, gs]
        bg, cg16, cbm = nxt
        if g + 1 < SSD_GROUPS:
            nxt = group_bc(g + 1)
        htg = ht_ref[g]
        yoff = jnp.dot(cg16, htg.astype(BF16), preferred_element_type=F32) * ea_full[:, gs]
        xw = (xg * wst_full[:, gs]).astype(BF16)
        bgt = _transpose_rows(bg, q).astype(BF16)
        st = jnp.dot(bgt, xw, preferred_element_type=F32)
        ht_ref[g] = ea_full[q - 1:q, gs] * htg + st
        xdt = (xg * dt_full[:, gs]).astype(BF16)
        yd = []
        for r in range(HEADS_PER_GROUP):
            h = g * HEADS_PER_GROUP + r
            seg = acum[:, h:h + 1] - acum_t[h:h + 1, :]
            m = cbm * jnp.exp(jnp.where(causal, seg, -jnp.inf))
            yd.append(jnp.dot(m.astype(BF16), xdt[:, r * SSD_HEADDIM:(r + 1) * SSD_HEADDIM],
                              preferred_element_type=F32))
        yd = jnp.concatenate(yd, axis=1)
        yg = yd + yoff + dskip_ref[:, gs] * xg
        yz = yg * _silu(z_ref[:, gs])
        ms = jnp.mean(yz * yz, axis=-1, keepdims=True)
        y_ref[:, gs] = (yz * lax.rsqrt(ms + RMS_EPS) * gn_ref[:, gs]).astype(y_ref.dtype)

    @pl.when(c == n_chunks - 1)
    def _():
        for g in range(SSD_GROUPS):
            hout_ref[0, g * GROUP_W:(g + 1) * GROUP_W, :] = ht_ref[g].T


def _head_expand_matrix():
    e = np.zeros((LANES, D_INNER), np.float32)
    for h in range(SSD_HEADS):
        e[h, h * SSD_HEADDIM:(h + 1) * SSD_HEADDIM] = 1.0
    return jnp.asarray(np.tile(e, (N_SPLIT, 1)), BF16)


def _ssd(u, n_seq, seq_len, q, conv0, h0, p, y_dtype):
    n_chunks = seq_len // q
    assert seq_len % q == 0 and q % 8 == 0
    conv_ix = (lambda b, c: (b, 0, 0)) if conv0.shape[0] == n_seq and n_seq > 1 else (lambda b, c: (0, 0, 0))
    h_ix = (lambda b, c: (b, 0, 0)) if h0.shape[0] == n_seq and n_seq > 1 else (lambda b, c: (0, 0, 0))
    const2 = lambda b, c: (0, 0)
    return pl.pallas_call(
        functools.partial(_ssd_body, q=q, n_chunks=n_chunks),
        out_shape=(jax.ShapeDtypeStruct((n_seq * seq_len, D_INNER), y_dtype),
                   jax.ShapeDtypeStruct((n_seq, D_INNER, D_STATE), F32)),
        grid=(n_seq, n_chunks),
        in_specs=[
            pl.BlockSpec((q, CONV_DIM), lambda b, c: (b * n_chunks + c, U_XBC // CONV_DIM)),
            pl.BlockSpec((q, D_INNER), lambda b, c: (b * n_chunks + c, U_Z // D_INNER)),
            pl.BlockSpec((q, LANES), lambda b, c: (b * n_chunks + c, U_DTF // LANES)),
            pl.BlockSpec((1, CONV_PAD, CONV_DIM), conv_ix),
            pl.BlockSpec((1, D_INNER, D_STATE), h_ix),
            pl.BlockSpec((CONV_W, CONV_DIM), const2),
            pl.BlockSpec((1, CONV_DIM), const2),
            pl.BlockSpec((1, LANES), const2),
            pl.BlockSpec((1, LANES), const2),
            pl.BlockSpec((1, D_INNER), const2),
            pl.BlockSpec((1, D_INNER), const2),
            pl.BlockSpec((N_SPLIT * LANES, D_INNER), const2),
        ],
        out_specs=(pl.BlockSpec((q, D_INNER), lambda b, c: (b * n_chunks + c, 0)),
                   pl.BlockSpec((1, D_INNER, D_STATE), lambda b, c: (b, 0, 0))),
        scratch_shapes=[pltpu.VMEM((CONV_PAD + q, CONV_DIM), F32),
                        pltpu.VMEM((SSD_GROUPS, D_STATE, GROUP_W), F32)],
        compiler_params=_cparams(("parallel", "arbitrary"), 48),
        name="ssd_q%d" % q,
    )(u, u, u, conv0, h0, p["conv_w"], p["conv_b"], p["dt_bias"], p["a_log"], p["d_skip"], p["ssd_norm_g"],
      p["head_expand"])


def _ssd_params(conv_w, conv_b, dt_bias, a_log, d_skip, ssd_norm_g):
    pad32 = lambda v: jnp.zeros((1, LANES), F32).at[0, :SSD_HEADS].set(v)
    return dict(conv_w=conv_w, conv_b=conv_b.reshape(1, CONV_DIM), dt_bias=pad32(dt_bias), a_log=pad32(a_log),
                d_skip=jnp.repeat(d_skip, SSD_HEADDIM).reshape(1, D_INNER),
                ssd_norm_g=ssd_norm_g.reshape(1, D_INNER), head_expand=_head_expand_matrix())


def _fox_prompt_body(qi_ref, kj_ref, qp_ref, kp_ref, vpt_ref, kpm_ref, vptm_ref, o_ref, m_ref, acc_ref, *, tq, tk):
    step = pl.program_id(1)
    i = qi_ref[step]
    j = kj_ref[step]
    nt = (((1,), (1,)), ((), ()))

    def scores(h, kp):
        g = h // ATTN_REP
        return lax.dot_general(kp[:, g * LANES:(g + 1) * LANES], qp_ref[:, h * LANES:(h + 1) * LANES], nt,
                               preferred_element_type=F32)

    def attend_all(kp, vpt, mask, ahead=QK_AHEAD):
        pending = [scores(h, kp) for h in range(ahead)]
        for h in range(ATTN_HEADS):
            g = h // ATTN_REP
            s = pending.pop(0)
            if h + ahead < ATTN_HEADS:
                pending.append(scores(h + ahead, kp))
            if mask is not None:
                s = jnp.where(mask, s, -jnp.inf)
            m_prev = m_ref[h, 0:1, :]
            m_new = jnp.maximum(m_prev, jnp.max(s, axis=0, keepdims=True))
            alpha = jnp.exp2(m_prev - m_new)
            p = jnp.exp2(s - m_new).astype(BF16)
            m_ref[h, 0:1, :] = m_new
            pv = jnp.dot(vpt[g * LANES:(g + 1) * LANES, :], p, preferred_element_type=F32)
            acc_ref[h] = alpha * acc_ref[h] + pv

    @pl.when(j == 0)
    def _():
        kpm = kpm_ref[...]
        vptm = vptm_ref[...]
        meta_scores = [scores(h, kpm) for h in range(ATTN_HEADS)]
        for h, s in enumerate(meta_scores):
            g = h // ATTN_REP
            m_new = jnp.max(s, axis=0, keepdims=True)
            m_ref[h, 0:1, :] = m_new
            acc_ref[h] = jnp.dot(vptm[g * LANES:(g + 1) * LANES, :], jnp.exp2(s - m_new).astype(BF16),
                                 preferred_element_type=F32)

    @pl.when(j < i)
    def _():
        attend_all(kp_ref[...], vpt_ref[...], None)

    @pl.when(j == i)
    def _():
        key = lax.broadcasted_iota(jnp.int32, (tk, tq), 0)
        qry = lax.broadcasted_iota(jnp.int32, (tk, tq), 1)
        attend_all(kp_ref[...], vpt_ref[...], key <= qry)
        for h in range(ATTN_HEADS):
            acc = acc_ref[h]
            o_t = acc[0:HEAD_DIM, :] * (1.0 / acc[V_ONES_ROW:V_ONES_ROW + 1, :])
            o_ref[:, h * HEAD_DIM:(h + 1) * HEAD_DIM] = o_t.T.astype(o_ref.dtype)


def _fox_prompt(qp, kp, vpt, kp_meta, vpt_meta, n_seq, seq_len, tq):
    nq = seq_len // tq
    assert seq_len % tq == 0
    qi = np.array([i for i in range(nq) for _ in range(i + 1)], np.int32)
    kj = np.array([j for i in range(nq) for j in range(i + 1)], np.int32)
    n_meta = kp_meta.shape[0]
    qw = ATTN_HEADS * LANES
    kw = ATTN_KV_HEADS * LANES
    grid_spec = pltpu.PrefetchScalarGridSpec(
        num_scalar_prefetch=2,
        grid=(n_seq, len(qi)),
        in_specs=[
            pl.BlockSpec((tq, qw), lambda b, s, qi, kj: (b * nq + qi[s], 0)),
            pl.BlockSpec((tq, kw), lambda b, s, qi, kj: (b * nq + kj[s], 0)),
            pl.BlockSpec((kw, tq), lambda b, s, qi, kj: (0, b * nq + kj[s])),
            pl.BlockSpec((n_meta, kw), lambda b, s, qi, kj: (0, 0)),
            pl.BlockSpec((kw, n_meta), lambda b, s, qi, kj: (0, 0)),
        ],
        out_specs=pl.BlockSpec((tq, ATTN_DIM), lambda b, s, qi, kj: (b * nq + qi[s], 0)),
        scratch_shapes=[pltpu.VMEM((ATTN_HEADS, 8, tq), F32),
                        pltpu.VMEM((ATTN_HEADS, LANES, tq), F32)],
    )
    return pl.pallas_call(
        functools.partial(_fox_prompt_body, tq=tq, tk=tq),
        out_shape=jax.ShapeDtypeStruct((n_seq * seq_len, ATTN_DIM), BF16),
        grid_spec=grid_spec,
        compiler_params=_cparams(("parallel", "arbitrary"), 48),
        name="fox_prompt",
    )(jnp.asarray(qi), jnp.asarray(kj), qp, kp, vpt, kp_meta, vpt_meta)


def _fox_sample_body(pt_ref, qaug_ref, kn_ref, vn_ref, dtf_ref, bf_ref, et_ref, suf_ref, ck_hbm, cv_hbm, clf_hbm,
                     o_ref, lfo_ref, kbuf, vbuf, lfbuf, sem, m_ref, l_ref, acc_ref, carry_ref, cnrow_ref,
                     *, npp, n_steps, n_pages, n_seq, s_new):
    seq = pl.program_id(0)
    step = pl.program_id(1)
    total = n_seq * n_steps
    t = seq * n_steps + step
    slot = t % SAMPLE_DMA_SLOTS
    rows = ATTN_HEADS * s_new
    qaug = qaug_ref[...]
    nt = (((1,), (1,)), ((), ()))

    def page_copies(t_i, s):
        seq_i = t_i // n_steps
        step_i = t_i - seq_i * n_steps
        first = seq_i * n_pages + n_pages - (step_i + 1) * npp
        out = []
        for i in range(npp):
            page = pt_ref[first + i]
            out.append(pltpu.make_async_copy(ck_hbm.at[page], kbuf.at[s, i], sem.at[s]))
            out.append(pltpu.make_async_copy(cv_hbm.at[page], vbuf.at[s, i], sem.at[s]))
            out.append(pltpu.make_async_copy(clf_hbm.at[page], lfbuf.at[s, i], sem.at[s]))
        return out

    @pl.when(t == 0)
    def _():
        for t_i in range(SAMPLE_DMA_SLOTS - 1):
            for c in page_copies(t_i, t_i):
                c.start()

    @pl.when(step == 0)
    def _():
        lane = lax.broadcasted_iota(jnp.int32, (s_new, LANES), 1)
        lfn = _log_sigmoid(dtf_ref[...] + bf_ref[...])
        lfo_ref[...] = jnp.where((lane >= SSD_HEADS) & (lane < SSD_HEADS + ATTN_HEADS), lfn, 0.0)
        lf16 = lfn[:, SSD_HEADS:SSD_HEADS + ATTN_HEADS]
        tri = (lax.broadcasted_iota(jnp.int32, (s_new, s_new), 1)
               <= lax.broadcasted_iota(jnp.int32, (s_new, s_new), 0)).astype(F32)
        cn = jnp.dot(tri, lf16, precision=HI, preferred_element_type=F32) * LOG2E
        cne = lax.dot_general(et_ref[...], cn, nt, precision=HI, preferred_element_type=F32)
        trow = lax.broadcasted_iota(jnp.int32, (rows, s_new), 0) % s_new
        tcol = lax.broadcasted_iota(jnp.int32, (rows, s_new), 1)
        cn_row = jnp.sum(jnp.where(tcol == trow, cne, 0.0), axis=-1, keepdims=True)
        cnrow_ref[...] = jnp.broadcast_to(cn_row, cnrow_ref.shape)
        ss = lax.dot_general(qaug[:, 0:KV_DIM], kn_ref[...].astype(BF16), nt, preferred_element_type=F32)
        ss = jnp.where(tcol <= trow, ss + cn_row - cne, -jnp.inf)
        m = jnp.max(ss, axis=-1, keepdims=True)
        p = jnp.exp2(ss - m)
        m_ref[...] = jnp.broadcast_to(m, m_ref.shape)
        l_ref[...] = jnp.broadcast_to(jnp.sum(p, axis=-1, keepdims=True), l_ref.shape)
        acc_ref[...] = jnp.dot(p.astype(BF16), vn_ref[...].astype(BF16), preferred_element_type=F32)
        carry_ref[...] = jnp.zeros(carry_ref.shape, F32)

    for c in page_copies(t, slot):
        c.wait()
    lf_all = jnp.concatenate([lfbuf[slot, i] for i in range(npp)], axis=0)
    in_page = jnp.dot(jnp.concatenate(_split3(lf_all), axis=1), suf_ref[...], preferred_element_type=F32)
    page_total = jnp.sum(lf_all, axis=1, keepdims=True)
    carry = carry_ref[:, 0:1]
    scores = [None] * npp
    for i in reversed(range(npp)):
        hs = slice(i * ATTN_HEADS, (i + 1) * ATTN_HEADS)
        r_hi, r_mid, r_lo = _split3((in_page[hs] + carry) * LOG2E)
        carry = carry + page_total[hs]
        kt = kbuf[slot, i].reshape(KV_DIM, PAGE_SIZE).astype(BF16)
        k_aug = jnp.concatenate([kt, r_hi, r_mid, r_lo], axis=0)
        scores[i] = jnp.dot(qaug, k_aug, preferred_element_type=F32)
    carry_ref[...] = jnp.broadcast_to(carry, carry_ref.shape)
    s_all = jnp.concatenate(scores, axis=1) + cnrow_ref[:, 0:1]
    vt_all = jnp.concatenate([vbuf[slot, i].reshape(KV_DIM, PAGE_SIZE).astype(BF16) for i in range(npp)], axis=1)
    m_prev = m_ref[...]
    m_new = jnp.maximum(m_prev, jnp.max(s_all, axis=-1, keepdims=True))
    alpha = jnp.exp2(m_prev - m_new)
    p = jnp.exp2(s_all - m_new[:, 0:1])
    l_ref[...] = alpha * l_ref[...] + jnp.sum(p, axis=-1, keepdims=True)
    m_ref[...] = m_new
    pv = lax.dot_general(p.astype(BF16), vt_all, nt, preferred_element_type=F32)
    acc_ref[...] = jnp.concatenate([alpha, alpha], axis=1) * acc_ref[...] + pv
    ahead = t + (SAMPLE_DMA_SLOTS - 1)
    for c in page_copies(jnp.where(ahead < total, ahead, t), ahead % SAMPLE_DMA_SLOTS):
        c.start()

    @pl.when(step == n_steps - 1)
    def _():
        inv = 1.0 / l_ref[...]
        o_ref[...] = acc_ref[...] * jnp.concatenate([inv, inv], axis=1)

    @pl.when(t == total - 1)
    def _():
        for back in range(SAMPLE_DMA_SLOTS - 1):
            for c in page_copies(t, (t + 1 + back) % SAMPLE_DMA_SLOTS):
                c.wait()


SAMPLE_PAGES_PER_STEP = 32
SAMPLE_DMA_SLOTS = 3


def _fox_sample(u_s, cache_k, cache_v, cache_logf, page_table, b_forget, n_seq, s_new, npp):
    n_pages = page_table.shape[1]
    assert n_pages % npp == 0 and ATTN_HEADS * s_new == LANES
    n_steps = n_pages // npp
    n_pool = cache_k.shape[0]
    rows = ATTN_HEADS * s_new
    ck = jnp.transpose(cache_k, (0, 2, 3, 1))
    cv = jnp.transpose(cache_v, (0, 2, 3, 1))
    clf = jnp.transpose(cache_logf, (0, 2, 1))
    q = u_s[:, U_Q:U_Q + ATTN_DIM].reshape(n_seq, s_new, ATTN_KV_HEADS, ATTN_REP, HEAD_DIM)
    qbd = jnp.einsum("btgrd,gh->bgrthd", q, jnp.eye(ATTN_KV_HEADS, dtype=F32)).reshape(n_seq, rows, KV_DIM)
    et = np.zeros((rows, ATTN_HEADS), np.float32)
    et[np.arange(rows), np.arange(rows) // s_new] = 1.0
    et_b = jnp.broadcast_to(jnp.asarray(et), (n_seq, rows, ATTN_HEADS))
    qaug = jnp.concatenate([qbd * (ATTN_SCALE * LOG2E)] + [et_b] * N_SPLIT, axis=-1).astype(BF16)
    aug_w = KV_DIM + N_SPLIT * ATTN_HEADS
    suf = np.tile(np.triu(np.ones((PAGE_SIZE, PAGE_SIZE), np.float32), 1).T, (N_SPLIT, 1))
    bf = jnp.zeros((1, LANES), F32).at[0, SSD_HEADS:SSD_HEADS + ATTN_HEADS].set(b_forget)

    const2 = lambda b, s, pt: (0, 0)
    hbm = pl.BlockSpec(memory_space=pl.ANY)
    in_specs = [
        pl.BlockSpec((None, rows, aug_w), lambda b, s, pt: (b, 0, 0)),
        pl.BlockSpec((s_new, KV_DIM), lambda b, s, pt: (b, U_K // KV_DIM)),
        pl.BlockSpec((s_new, KV_DIM), lambda b, s, pt: (b, U_V // KV_DIM)),
        pl.BlockSpec((s_new, LANES), lambda b, s, pt: (b, U_DTF // LANES)),
        pl.BlockSpec((1, LANES), const2),
        pl.BlockSpec((rows, ATTN_HEADS), const2),
        pl.BlockSpec((N_SPLIT * PAGE_SIZE, PAGE_SIZE), const2),
        hbm, hbm, hbm,
    ]
    grid_spec = pltpu.PrefetchScalarGridSpec(
        num_scalar_prefetch=1,
        grid=(n_seq, n_steps),
        in_specs=in_specs,
        out_specs=(pl.BlockSpec((None, rows, KV_DIM), lambda b, s, pt: (b, 0, 0)),
                   pl.BlockSpec((s_new, LANES), lambda b, s, pt: (b, 0))),
        scratch_shapes=[pltpu.VMEM((SAMPLE_DMA_SLOTS, npp, ATTN_KV_HEADS, HEAD_DIM, PAGE_SIZE), F32),
                        pltpu.VMEM((SAMPLE_DMA_SLOTS, npp, ATTN_KV_HEADS, HEAD_DIM, PAGE_SIZE), F32),
                        pltpu.VMEM((SAMPLE_DMA_SLOTS, npp, ATTN_HEADS, PAGE_SIZE), F32),
                        pltpu.SemaphoreType.DMA((SAMPLE_DMA_SLOTS,)),
                        pltpu.VMEM((rows, LANES), F32), pltpu.VMEM((rows, LANES), F32),
                        pltpu.VMEM((rows, KV_DIM), F32), pltpu.VMEM((ATTN_HEADS, LANES), F32),
                        pltpu.VMEM((rows, LANES), F32)],
    )
    o_raw, lf_slab = pl.pallas_call(
        functools.partial(_fox_sample_body, npp=npp, n_steps=n_steps, n_pages=n_pages, n_seq=n_seq, s_new=s_new),
        out_shape=(jax.ShapeDtypeStruct((n_seq, rows, KV_DIM), F32),
                   jax.ShapeDtypeStruct((n_seq * s_new, LANES), F32)),
        grid_spec=grid_spec,
        compiler_params=_cparams(("arbitrary", "arbitrary"), 48),
        name="fox_sample",
    )(page_table.reshape(-1), qaug, u_s, u_s, u_s, bf, jnp.asarray(et), jnp.asarray(suf, BF16), ck, cv, clf)
    o = o_raw.reshape(n_seq, ATTN_KV_HEADS, ATTN_REP, s_new, ATTN_KV_HEADS, HEAD_DIM)
    o = jnp.einsum("bgrtgd->btgrd", o).reshape(n_seq * s_new, ATTN_DIM)
    return o, lf_slab


ROUTE_E0, ROUTE_E1, ROUTE_W0, ROUTE_W1 = 0, 1, 2, 3
ROUTER_EXPERT_LANE0 = N_EXPERT_GROUPS
_BIG_LANE = 4 * LANES


def _sigmoid(x):
    return 1.0 / (1.0 + jnp.exp(-x))


def _mix_body(y_ref, o_ref, gs_ref, ga_ref, h_ref, wssd_ref, wattn_ref, wout_ref, g2_ref, wr_ref, br_ref,
              h1_ref, hn_ref, route_ref):
    ys = jnp.dot(y_ref[...].astype(BF16), wssd_ref[...], preferred_element_type=F32)
    oa = jnp.dot(o_ref[...].astype(BF16), wattn_ref[...], preferred_element_type=F32)
    mix = _sigmoid(gs_ref[...]) * ys + _sigmoid(ga_ref[...]) * oa
    h1 = h_ref[...] + jnp.dot(mix.astype(BF16), wout_ref[...], preferred_element_type=F32)
    h1_ref[...] = h1
    ms = jnp.mean(h1 * h1, axis=-1, keepdims=True)
    hn = (h1 * lax.rsqrt(ms + RMS_EPS)) * g2_ref[...]
    hn_ref[...] = hn
    hn_hi = hn.astype(BF16)
    hn_lo = (hn - hn_hi.astype(F32)).astype(BF16)
    logits = jnp.dot(jnp.concatenate([hn_hi, hn_hi, hn_lo], axis=1), wr_ref[...], preferred_element_type=F32)
    route_ref[...] = _route(logits + br_ref[...])


def _route(logits):
    lane = lax.broadcasted_iota(jnp.int32, logits.shape, 1)
    gl = jnp.where(lane < N_EXPERT_GROUPS, logits, -jnp.inf)
    gmax = jnp.max(gl, axis=-1, keepdims=True)
    gsel = jnp.min(jnp.where(gl == gmax, lane, _BIG_LANE), axis=-1, keepdims=True)
    wgrp = 1.0 / jnp.sum(jnp.exp(gl - gmax), axis=-1, keepdims=True)
    elane = lane - ROUTER_EXPERT_LANE0
    in_group = (elane >= gsel * EXPERTS_PER_GROUP) & (elane < (gsel + 1) * EXPERTS_PER_GROUP)
    el = jnp.where(in_group, logits, -jnp.inf)
    t1 = jnp.max(el, axis=-1, keepdims=True)
    i1 = jnp.min(jnp.where(el == t1, lane, _BIG_LANE), axis=-1, keepdims=True)
    el2 = jnp.where(lane == i1, -jnp.inf, el)
    t2 = jnp.max(el2, axis=-1, keepdims=True)
    i2 = jnp.min(jnp.where(el2 == t2, lane, _BIG_LANE), axis=-1, keepdims=True)
    e21 = jnp.exp(t2 - t1)
    w1 = wgrp / (1.0 + e21)
    w2 = w1 * e21
    return jnp.where(lane == ROUTE_E0, (i1 - ROUTER_EXPERT_LANE0).astype(F32),
                     jnp.where(lane == ROUTE_E1, (i2 - ROUTER_EXPERT_LANE0).astype(F32),
                               jnp.where(lane == ROUTE_W0, w1, jnp.where(lane == ROUTE_W1, w2, 0.0))))


def _mix(y, o, u, h, p, tm):
    t = h.shape[0]
    assert t % tm == 0
    row = lambda i: (i, 0)
    const = lambda i: (0, 0)
    return pl.pallas_call(
        _mix_body,
        out_shape=(jax.ShapeDtypeStruct((t, D_MODEL), F32), jax.ShapeDtypeStruct((t, D_MODEL), F32),
                   jax.ShapeDtypeStruct((t, LANES), F32)),
        grid=(t // tm,),
        in_specs=[
            pl.BlockSpec((tm, D_INNER), row),
            pl.BlockSpec((tm, ATTN_DIM), row),
            pl.BlockSpec((tm, D_MODEL), lambda i: (i, U_GS // D_MODEL)),
            pl.BlockSpec((tm, D_MODEL), lambda i: (i, U_GA // D_MODEL)),
            pl.BlockSpec((tm, D_MODEL), row),
            pl.BlockSpec((D_INNER, D_MODEL), const),
            pl.BlockSpec((ATTN_DIM, D_MODEL), const),
            pl.BlockSpec((D_MODEL, D_MODEL), const),
            pl.BlockSpec((1, D_MODEL), const),
            pl.BlockSpec((3 * D_MODEL, LANES), const),
            pl.BlockSpec((1, LANES), const),
        ],
        out_specs=(pl.BlockSpec((tm, D_MODEL), row), pl.BlockSpec((tm, D_MODEL), row),
                   pl.BlockSpec((tm, LANES), row)),
        compiler_params=_cparams(("parallel",), 56),
        name="mix_route",
    )(y, o, u, u, h, p["w_ssd_br"], p["w_attn_br"], p["w_out"], p["norm2_g"], p["w_router"], p["b_router"])


def _mix_params(w_ssd_br, w_attn_br, w_out, norm2_g, w_rg, b_rg, w_re, b_re):
    n_r = N_EXPERT_GROUPS + N_EXPERTS
    w_router = jnp.concatenate([w_rg, w_re, jnp.zeros((D_MODEL, LANES - n_r), F32)], axis=1)
    w_hi = w_router.astype(BF16)
    w_lo = (w_router - w_hi.astype(F32)).astype(BF16)
    w_router = jnp.concatenate([w_hi, w_lo, w_hi], axis=0)
    b_router = jnp.concatenate([b_rg, b_re, jnp.zeros((LANES - n_r,), F32)]).reshape(1, LANES)
    return dict(w_ssd_br=w_ssd_br.astype(BF16), w_attn_br=w_attn_br.astype(BF16), w_out=w_out.astype(BF16),
                norm2_g=norm2_g.reshape(1, D_MODEL), w_router=w_router, b_router=b_router)


def _moe_rank_body(route_ref, pos_ref, cnt_ref, carry_ref, *, tm):
    @pl.when(pl.program_id(0) == 0)
    def _():
        carry_ref[...] = jnp.zeros(carry_ref.shape, F32)

    route = route_ref[...]
    lane = lax.broadcasted_iota(jnp.int32, (tm, LANES), 1).astype(F32)
    hit0 = lane == route[:, ROUTE_E0:ROUTE_E0 + 1]
    hit1 = lane == route[:, ROUTE_E1:ROUTE_E1 + 1]
    onehot = hit0.astype(F32) + hit1.astype(F32)
    before = (lax.broadcasted_iota(jnp.int32, (tm, tm), 1) < lax.broadcasted_iota(jnp.int32, (tm, tm), 0))
    c = jnp.dot(before.astype(BF16), onehot.astype(BF16), preferred_element_type=F32) + carry_ref[...]
    pos0 = jnp.sum(jnp.where(hit0, c, 0.0), axis=-1, keepdims=True)
    pos1 = jnp.sum(jnp.where(hit1, c, 0.0), axis=-1, keepdims=True)
    pos_ref[...] = jnp.where(lane == 0.0, pos0, jnp.where(lane == 1.0, pos1, 0.0))
    total = carry_ref[...] + jnp.sum(onehot, axis=0, keepdims=True)
    carry_ref[...] = total
    cnt_ref[...] = total


def _moe_rank(route, tm):
    t = route.shape[0]
    assert t % tm == 0
    return pl.pallas_call(
        functools.partial(_moe_rank_body, tm=tm),
        out_shape=(jax.ShapeDtypeStruct((t, LANES), F32), jax.ShapeDtypeStruct((1, LANES), F32)),
        grid=(t // tm,),
        in_specs=[pl.BlockSpec((tm, LANES), lambda i: (i, 0))],
        out_specs=(pl.BlockSpec((tm, LANES), lambda i: (i, 0)), pl.BlockSpec((1, LANES), lambda i: (0, 0))),
        scratch_shapes=[pltpu.VMEM((1, LANES), F32)],
        compiler_params=_cparams(("arbitrary",)),
        name="moe_rank",
    )(route)


def _experts_body(be_ref, nu_ref, x_ref, wg_ref, wu_ref, wd_ref, y_ref, wg16_ref, wu16_ref, wd16_ref):
    i = pl.program_id(0)
    prev = be_ref[jnp.maximum(i - 1, 0)]

    @pl.when((i == 0) | (be_ref[i] != prev))
    def _():
        wg16_ref[...] = wg_ref[...].astype(BF16)
        wu16_ref[...] = wu_ref[...].astype(BF16)
        wd16_ref[...] = wd_ref[...].astype(BF16)

    @pl.when(i < nu_ref[0])
    def _():
        sub = x_ref.shape[0] // 2
        rows = [slice(0, sub), slice(sub, 2 * sub)]
        gate_up = []
        for r in rows:
            x16 = x_ref[r, :].astype(BF16)
            gate_up.append((jnp.dot(x16, wg16_ref[...], preferred_element_type=F32),
                            jnp.dot(x16, wu16_ref[...], preferred_element_type=F32)))
        for r, (gt, up) in zip(rows, gate_up):
            y_ref[r, :] = jnp.dot((_silu(gt) * up).astype(BF16), wd16_ref[...], preferred_element_type=F32)

    @pl.when(i >= nu_ref[0])
    def _():
        y_ref[...] = jnp.zeros(y_ref.shape, F32)


def _experts(xs, block_e, n_used, w_gate, w_up, w_down, blk):
    rows = xs.shape[0]
    n_blocks = rows // blk
    grid_spec = pltpu.PrefetchScalarGridSpec(
        num_scalar_prefetch=2,
        grid=(n_blocks,),
        in_specs=[
            pl.BlockSpec((blk, D_MODEL), lambda i, be, nu: (jnp.minimum(i, nu[0] - 1), 0)),
            pl.BlockSpec((None, D_MODEL, D_EXPERT), lambda i, be, nu: (be[i], 0, 0)),
            pl.BlockSpec((None, D_MODEL, D_EXPERT), lambda i, be, nu: (be[i], 0, 0)),
            pl.BlockSpec((None, D_EXPERT, D_MODEL), lambda i, be, nu: (be[i], 0, 0)),
        ],
        out_specs=pl.BlockSpec((blk, D_MODEL), lambda i, be, nu: (i, 0)),
        scratch_shapes=[pltpu.VMEM((D_MODEL, D_EXPERT), BF16), pltpu.VMEM((D_MODEL, D_EXPERT), BF16),
                        pltpu.VMEM((D_EXPERT, D_MODEL), BF16)],
    )
    return pl.pallas_call(
        _experts_body,
        out_shape=jax.ShapeDtypeStruct((rows, D_MODEL), F32),
        grid_spec=grid_spec,
        compiler_params=_cparams(("arbitrary",), 48),
        name="moe_experts",
    )(block_e, n_used, xs, w_gate, w_up, w_down)


MOE_ROWS_PER_BLOCK = 256
MOE_ROWS_PER_BLOCK_FEW_TOKENS = 128
MOE_TOKEN_TILE = 256
DMA_ISSUE_UNROLL = 8


def _row_copy(src, src_row, dst, dst_row, sem):
    return pltpu.make_async_copy(src.at[pl.ds(src_row, 1)], dst.at[pl.ds(dst_row, 1)], sem)


def _dispatch_body(zb_ref, nu_ref, d0_ref, d1_ref, hn_ref, xs_ref, zero_ref, sem, zsem, *, tm, blk, n_blocks):
    @pl.when(pl.program_id(0) == 0)
    def _():
        zero_ref[...] = jnp.zeros(zero_ref.shape, F32)

        def zero_block(b):
            return pltpu.make_async_copy(zero_ref, xs_ref.at[pl.ds(b * blk, blk)], zsem)

        for e in range(N_EXPERTS):
            zero_block(zb_ref[e]).start()

        def start_tail(b, c):
            zero_block(b).start()
            return c

        def wait_tail(b, c):
            zero_block(b).wait()
            return c

        lax.fori_loop(nu_ref[0], n_blocks, start_tail, 0)
        for e in range(N_EXPERTS):
            zero_block(0).wait()
        lax.fori_loop(nu_ref[0], n_blocks, wait_tail, 0)

    for r in range(tm):
        _row_copy(hn_ref, r, xs_ref, d0_ref[r], sem).start(priority=0)
        _row_copy(hn_ref, r, xs_ref, d1_ref[r], sem).start(priority=1)

    def drain(r, c):
        _row_copy(hn_ref, 0, xs_ref, 0, sem).wait()
        _row_copy(hn_ref, 0, xs_ref, 0, sem).wait()
        return c

    lax.fori_loop(0, tm, drain, 0, unroll=DMA_ISSUE_UNROLL)


def _dispatch(hn, d0, d1, zero_blocks, n_used, n_blocks, blk, tm):
    t = hn.shape[0]
    assert t % tm == 0
    smem_tile = lambda: pl.BlockSpec((tm,), lambda i, zb, nu: (i,), memory_space=pltpu.SMEM)
    grid_spec = pltpu.PrefetchScalarGridSpec(
        num_scalar_prefetch=2,
        grid=(t // tm,),
        in_specs=[smem_tile(), smem_tile(), pl.BlockSpec((tm, D_MODEL), lambda i, zb, nu: (i, 0))],
        out_specs=pl.BlockSpec(memory_space=pl.ANY),
        scratch_shapes=[pltpu.VMEM((blk, D_MODEL), F32), pltpu.SemaphoreType.DMA, pltpu.SemaphoreType.DMA],
    )
    return pl.pallas_call(
        functools.partial(_dispatch_body, tm=tm, blk=blk, n_blocks=n_blocks),
        out_shape=jax.ShapeDtypeStruct((n_blocks * blk, D_MODEL), F32),
        grid_spec=grid_spec,
        compiler_params=_cparams(("arbitrary",)),
        name="moe_dispatch",
    )(zero_blocks, n_used, d0, d1, hn)


def _combine_body(d0_ref, d1_ref, d0n_ref, d1n_ref, h1_ref, route_ref, g_ref, ys_ref, o_ref, buf, sem, *, tm, n_tiles):
    i = pl.program_id(0)
    slot = i % 2

    def gather(da_ref, db_ref, s):
        for r in range(tm):
            _row_copy(ys_ref, da_ref[r], buf.at[s, 0], r, sem.at[s]).start()
            _row_copy(ys_ref, db_ref[r], buf.at[s, 1], r, sem.at[s]).start(priority=1)

    @pl.when(i == 0)
    def _():
        gather(d0_ref, d1_ref, 0)

    @pl.when(i + 1 < n_tiles)
    def _():
        gather(d0n_ref, d1n_ref, 1 - slot)

    def drain(r, c):
        _row_copy(ys_ref, 0, buf.at[slot, 0], 0, sem.at[slot]).wait()
        _row_copy(ys_ref, 0, buf.at[slot, 1], 0, sem.at[slot]).wait()
        return c

    lax.fori_loop(0, tm, drain, 0, unroll=DMA_ISSUE_UNROLL)
    route = route_ref[...]
    h = (h1_ref[...] + route[:, ROUTE_W0:ROUTE_W0 + 1] * buf[slot, 0]
         + route[:, ROUTE_W1:ROUTE_W1 + 1] * buf[slot, 1])
    ms = jnp.mean(h * h, axis=-1, keepdims=True)
    o_ref[...] = (h * lax.rsqrt(ms + RMS_EPS)) * g_ref[...]


def _combine(h1, ys, d0, d1, route, g, tm):
    t = h1.shape[0]
    n_tiles = t // tm
    row = lambda i: (i, 0)
    cur = lambda: pl.BlockSpec((tm,), lambda i: (i,), memory_space=pltpu.SMEM)
    nxt = lambda: pl.BlockSpec((tm,), lambda i: (jnp.minimum(i + 1, n_tiles - 1),), memory_space=pltpu.SMEM)
    return pl.pallas_call(
        functools.partial(_combine_body, tm=tm, n_tiles=n_tiles),
        out_shape=jax.ShapeDtypeStruct((t, D_MODEL), F32),
        grid=(n_tiles,),
        in_specs=[cur(), cur(), nxt(), nxt(), pl.BlockSpec((tm, D_MODEL), row), pl.BlockSpec((tm, LANES), row),
                  pl.BlockSpec((1, D_MODEL), lambda i: (0, 0)), pl.BlockSpec(memory_space=pl.ANY)],
        out_specs=pl.BlockSpec((tm, D_MODEL), row),
        scratch_shapes=[pltpu.VMEM((2, 2, tm, D_MODEL), F32), pltpu.SemaphoreType.DMA((2,))],
        compiler_params=_cparams(("arbitrary",)),
        name="moe_combine_norm",
    )(d0, d1, d0, d1, h1, route, g.reshape(1, D_MODEL), ys)


def _moe(hn, h1, route, w_gate, w_up, w_down, final_g, blk):
    t = hn.shape[0]
    pos, cnt = _moe_rank(route, MOE_TOKEN_TILE)
    counts = cnt[0, :N_EXPERTS].astype(jnp.int32)
    padded = (counts + blk - 1) // blk * blk
    ends = jnp.cumsum(padded)
    starts = ends - padded
    expert_ids = jnp.arange(N_EXPERTS, dtype=jnp.int32)

    def dest(e_lane, p_lane):
        e = route[:, e_lane].astype(jnp.int32)
        start = jnp.sum(jnp.where(e[:, None] == expert_ids[None, :], starts[None, :], 0), axis=1)
        return start + pos[:, p_lane].astype(jnp.int32)

    d0 = dest(ROUTE_E0, 0)
    d1 = dest(ROUTE_E1, 1)
    n_blocks = (2 * t + N_EXPERTS * (blk - 1) + blk - 1) // blk
    first_row = jnp.arange(n_blocks, dtype=jnp.int32) * blk
    block_e = jnp.minimum(jnp.sum((ends[None, :] <= first_row[:, None]).astype(jnp.int32), axis=1), N_EXPERTS - 1)
    n_used = (ends[-1] // blk).astype(jnp.int32).reshape(1)
    zero_blocks = jnp.clip((ends - 1) // blk, 0, n_blocks - 1).astype(jnp.int32)
    xs = _dispatch(hn, d0, d1, zero_blocks, n_used, n_blocks, blk, MOE_TOKEN_TILE)
    ys = _experts(xs, block_e, n_used, w_gate, w_up, w_down, blk)
    return _combine(h1, ys, d0, d1, route, final_g, MOE_TOKEN_TILE)


def _conv_history(rows):
    n = rows.shape[0]
    return jnp.concatenate([jnp.zeros((n, CONV_PAD - (CONV_W - 1), CONV_DIM), F32), rows], axis=1)


def _ucols(u, start, width):
    return u[:, start:start + width]


def kernel(x_prompt, x_sample, cache_k, cache_v, cache_logf, state_ssm, state_conv, page_table, meta_tokens, norm1_g, w_in, conv_w, conv_b, dt_bias, a_log, d_skip, ssd_norm_g, b_forget, w_ssd_br, w_attn_br, w_out, norm2_g, w_router_group, b_router_group, w_router_expert, b_router_expert, w_exp_gate, w_exp_up, w_exp_down, final_norm_g):
    nb, sl, _ = x_prompt.shape
    sb, ss, _ = x_sample.shape
    assert w_in.shape[0] == 1, "single-layer step"
    ly = 0
    xp = x_prompt.reshape(nb * sl, D_MODEL)
    xs = x_sample.reshape(sb * ss, D_MODEL)

    wp = _pack_w_in(w_in[ly])
    u_p = _inproj(xp, norm1_g[ly], wp)
    u_s = _inproj(xs, norm1_g[ly], wp)
    u_m = _inproj(meta_tokens, norm1_g[ly], wp)

    sp = _ssd_params(conv_w[ly], conv_b[ly], dt_bias[ly], a_log[ly], d_skip[ly], ssd_norm_g[ly])
    zero_hist = jnp.zeros((1, CONV_PAD, CONV_DIM), F32)
    zero_state = jnp.zeros((1, D_INNER, D_STATE), F32)
    _, h_meta = _ssd(u_m, 1, N_META, N_META, zero_hist, zero_state, sp, BF16)
    hist_meta = _conv_history(_ucols(u_m, U_XBC, CONV_DIM)[None, N_META - (CONV_W - 1):])
    y_p, ssm_p = _ssd(u_p, nb, sl, SSD_CHUNK, hist_meta, h_meta, sp, BF16)
    y_s, ssm_s = _ssd(u_s, sb, ss, ss, _conv_history(state_conv[ly]),
                      state_ssm[ly].reshape(sb, D_INNER, D_STATE), sp, F32)

    slab_m, _, kp_m, vpt_m, kt_m, vt_m = _attn_prep(u_m, b_forget[ly], N_META, N_META, rel_to_last=True)
    slab_p, qp_p, kp_p, vpt_p, kt_p, vt_p = _attn_prep(u_p, b_forget[ly], nb * sl, sl)
    o_p = _fox_prompt(qp_p, kp_p, vpt_p, kp_m, vpt_m, nb, sl, ATTN_TQ)
    o_s, slab_s = _fox_sample(u_s, cache_k[ly], cache_v[ly], cache_logf[ly], page_table, b_forget[ly], sb, ss,
                              SAMPLE_PAGES_PER_STEP)

    mp = _mix_params(w_ssd_br[ly], w_attn_br[ly], w_out[ly], norm2_g[ly], w_router_group[ly], b_router_group[ly],
                     w_router_expert[ly], b_router_expert[ly])
    h1_p, hn_p, route_p = _mix(y_p, o_p, u_p, xp, mp, 256)
    h1_s, hn_s, route_s = _mix(y_s, o_s, u_s, xs, mp, 256)
    out_p = _moe(hn_p, h1_p, route_p, w_exp_gate[ly], w_exp_up[ly], w_exp_down[ly], final_norm_g,
                 MOE_ROWS_PER_BLOCK)
    out_s = _moe(hn_s, h1_s, route_s, w_exp_gate[ly], w_exp_up[ly], w_exp_down[ly], final_norm_g,
                 MOE_ROWS_PER_BLOCK_FEW_TOKENS)

    def with_meta(meta_rows, rows, width):
        m = jnp.broadcast_to(meta_rows[None], (nb, N_META, width))
        return jnp.concatenate([m, rows.reshape(nb, sl, width)], axis=1)[None]

    def kv_with_meta(t_meta, t_rows):
        m = jnp.broadcast_to(t_meta.reshape(ATTN_KV_HEADS, HEAD_DIM, 1, N_META), (ATTN_KV_HEADS, HEAD_DIM, nb, N_META))
        full = jnp.concatenate([m, t_rows.reshape(ATTN_KV_HEADS, HEAD_DIM, nb, sl)], axis=3)
        return jnp.transpose(full, (2, 3, 0, 1))[None]

    lf_m = slab_m[:, SSD_HEADS:SSD_HEADS + ATTN_HEADS]
    lf_p = slab_p[:, SSD_HEADS:SSD_HEADS + ATTN_HEADS]
    lf_s = slab_s[:, SSD_HEADS:SSD_HEADS + ATTN_HEADS]
    kv_shape_s = (1, sb, ss, ATTN_KV_HEADS, HEAD_DIM)
    state_shape = (SSD_HEADS, SSD_HEADDIM, D_STATE)
    tail = CONV_W - 1
    return (
        out_p.reshape(nb, sl, D_MODEL),
        out_s.reshape(sb, ss, D_MODEL),
        kv_with_meta(kt_m, kt_p),
        kv_with_meta(vt_m, vt_p),
        with_meta(lf_m, lf_p, ATTN_HEADS),
        ssm_p.reshape((1, nb) + state_shape),
        u_p.reshape(nb, sl, U_COLS)[None, :, sl - tail:, U_XBC:U_XBC + CONV_DIM],
        _ucols(u_s, U_K, KV_DIM).reshape(kv_shape_s),
        _ucols(u_s, U_V, KV_DIM).reshape(kv_shape_s),
        lf_s.reshape(1, sb, ss, ATTN_HEADS),
        ssm_s.reshape((1, sb) + state_shape),
        u_s.reshape(sb, ss, U_COLS)[None, :, ss - tail:, U_XBC:U_XBC + CONV_DIM],
    )
```
